```python
import jax, jax.numpy as jnp
from jax import lax
import numpy as np

D_MODEL = 2048
BATCH = 2
SEQ = 4096
DEPTH = 1

MEM_LEN = 256
EPS = 1e-6
NEG_INF = -1e30
D_FF = 5632
MIX_WIDTH = D_MODEL
GLA_WIDTH = MIX_WIDTH // 2
MOBA_WIDTH = MIX_WIDTH - GLA_WIDTH
GLA_HEADS = 4
GLA_DV = GLA_WIDTH // GLA_HEADS
GLA_DK = GLA_DV // 2
GLA_QK = GLA_HEADS * GLA_DK
GLA_GATE_RANK = 16
GLA_GATE_TAU = 16.0
GLA_CHUNK = 64
MOBA_DH = 128
MOBA_HEADS = MOBA_WIDTH // MOBA_DH
MOBA_BLOCK = 256
MOBA_TOPK = 3
MOBA_QCHUNK = 32
XATTN_HEADS = 4
XATTN_DH = 128
XATTN_WIDTH = XATTN_HEADS * XATTN_DH
IN_SPLITS = (GLA_QK, GLA_QK, GLA_WIDTH, GLA_WIDTH, GLA_GATE_RANK, MOBA_WIDTH, MOBA_WIDTH, MOBA_WIDTH)
W_IN_COLS = sum(IN_SPLITS)

kernel_name = "hymba_gla_moba_macaron_layer"


def rms_norm(x, g):
    xf = x.astype(jnp.float32)
    y = xf * lax.rsqrt(jnp.mean(xf * xf, axis=-1, keepdims=True) + EPS)
    return (y * g.astype(jnp.float32)).astype(x.dtype)


def swiglu(h, w_gate, w_up, w_down):
    return (jax.nn.silu(h @ w_gate) * (h @ w_up)) @ w_down


def split_offsets(sizes):
    offs, acc = [], 0
    for s in sizes[:-1]:
        acc += s
        offs.append(acc)
    return offs


def gla_group(q, k, v, r, gate_lr, w_gate2, b_gate2, g_out):
    B, T, H, dk = q.shape
    dv = v.shape[-1]
    C = GLA_CHUNK
    N = T // C
    log_a = jax.nn.log_sigmoid((gate_lr @ w_gate2 + b_gate2).astype(jnp.float32)) / GLA_GATE_TAU
    log_a = log_a.reshape(B, T, H, dk)

    def chunks(a):
        return a.astype(jnp.float32).reshape(B, N, C, H, -1).transpose(0, 3, 1, 2, 4)

    qc = chunks(q) * (dk ** -0.5)
    kc, vc, ac = chunks(k), chunks(v), chunks(log_a)
    bcum = jnp.cumsum(ac, axis=3)
    b_last = bcum[:, :, :, -1:, :]
    q_dec = qc * jnp.exp(bcum)
    k_inv = kc * jnp.exp(-bcum)
    k_tail = kc * jnp.exp(b_last - bcum)
    causal = jnp.tril(jnp.ones((C, C), dtype=bool))
    att = jnp.where(causal, jnp.einsum('bhnid,bhnjd->bhnij', q_dec, k_inv), 0.0)
    o_intra = jnp.einsum('bhnij,bhnjv->bhniv', att, vc)
    upd = jnp.einsum('bhncd,bhncv->bhndv', k_tail, vc)
    decay = jnp.exp(b_last[:, :, :, 0, :])

    def step(S, inp):
        u, dcy = inp
        return dcy[..., None] * S + u, S

    S0 = jnp.zeros((B, H, dk, dv), jnp.float32)
    _, S_prev = lax.scan(step, S0, (jnp.moveaxis(upd, 2, 0), jnp.moveaxis(decay, 2, 0)))
    S_prev = jnp.moveaxis(S_prev, 0, 2)
    o_inter = jnp.einsum('bhncd,bhndv->bhncv', q_dec, S_prev)
    o = (o_intra + o_inter).transpose(0, 2, 3, 1, 4).reshape(B, T, H, dv)
    o = rms_norm(o, g_out) * jax.nn.silu(r.astype(jnp.float32))
    return o.reshape(B, T, H * dv).astype(q.dtype)


def moba_group(q, k, v, g_q, g_k):
    B, T, H, dh = q.shape
    BS, QC = MOBA_BLOCK, MOBA_QCHUNK
    NB = -(-T // BS)
    Tp = NB * BS
    n_sel = min(MOBA_TOPK, NB)
    scale = dh ** -0.5
    qh = rms_norm(q, g_q).transpose(0, 2, 1, 3)
    kh = rms_norm(k, g_k).transpose(0, 2, 1, 3)
    vh = v.transpose(0, 2, 1, 3)
    pad = ((0, 0), (0, 0), (0, Tp - T), (0, 0))
    kb = jnp.pad(kh, pad).reshape(B, H, NB, BS, dh)
    vb = jnp.pad(vh, pad).reshape(B, H, NB, BS, dh)
    k_mean = jnp.mean(kb, axis=3)
    qblk = jnp.arange(T) // BS
    gate = jnp.einsum('bhtd,bhnd->bhtn', qh, k_mean).astype(jnp.float32)
    past = jnp.arange(NB)[None, :] < qblk[:, None]
    gate = jnp.where(past, gate, NEG_INF)
    _, idx = lax.top_k(gate, n_sel)
    valid = idx < qblk[None, None, :, None]

    NQ = T // QC

    def qchunks(a):
        return jnp.moveaxis(a.reshape((B, H, NQ, QC) + a.shape[3:]), 2, 0)

    bi = jnp.arange(B)[:, None, None, None]
    hi = jnp.arange(H)[None, :, None, None]

    def one_chunk(args):
        q_c, idx_c, valid_c, c = args
        k_g = kb[bi, hi, idx_c]
        v_g = vb[bi, hi, idx_c]
        s_past = jnp.einsum('bhqd,bhqjkd->bhqjk', q_c, k_g).astype(jnp.float32) * scale
        s_past = jnp.where(valid_c[..., None], s_past, NEG_INF).reshape(B, H, QC, n_sel * BS)
        blk = (c * QC) // BS
        k_own = lax.dynamic_index_in_dim(kb, blk, axis=2, keepdims=False)
        v_own = lax.dynamic_index_in_dim(vb, blk, axis=2, keepdims=False)
        q_pos = c * QC + jnp.arange(QC)
        k_pos = blk * BS + jnp.arange(BS)
        s_own = jnp.einsum('bhqd,bhkd->bhqk', q_c, k_own).astype(jnp.float32) * scale
        s_own = jnp.where(k_pos[None, :] <= q_pos[:, None], s_own, NEG_INF)
        p = jax.nn.softmax(jnp.concatenate([s_past, s_own], axis=-1), axis=-1).astype(v_g.dtype)
        p_past = p[..., :n_sel * BS].reshape(B, H, QC, n_sel, BS)
        return (jnp.einsum('bhqjk,bhqjkd->bhqd', p_past, v_g)
                + jnp.einsum('bhqk,bhkd->bhqd', p[..., n_sel * BS:], v_own))

    out = lax.map(one_chunk, (qchunks(qh), qchunks(idx), qchunks(valid), jnp.arange(NQ)))
    out = jnp.moveaxis(out, 0, 2).reshape(B, H, T, dh)
    return out.transpose(0, 2, 1, 3).reshape(B, T, H * dh)


def memory_cross_attn(h, mem_n, w_q, w_kv, w_o, g_q, g_k):
    B, T, _ = h.shape
    M = mem_n.shape[1]
    q = rms_norm((h @ w_q).reshape(B, T, XATTN_HEADS, XATTN_DH), g_q)
    kv = (mem_n @ w_kv).reshape(B, M, 2, XATTN_HEADS, XATTN_DH)
    k = rms_norm(kv[:, :, 0], g_k)
    v = kv[:, :, 1]
    s = jnp.einsum('bthd,bmhd->bhtm', q, k).astype(jnp.float32) * (XATTN_DH ** -0.5)
    p = jax.nn.softmax(s, axis=-1).astype(v.dtype)
    o = jnp.einsum('bhtm,bmhd->bthd', p, v).reshape(B, T, XATTN_WIDTH)
    return o @ w_o


def setup_inputs(seed: int = 0) -> dict:
    key = jax.random.key(seed)
    ks = jax.random.split(key, 32)
    L = DEPTH

    def w(k, shape, fan_in):
        return jax.random.normal(k, shape, jnp.float32) * (fan_in ** -0.5)

    def gain(k, shape):
        return 1.0 + 0.05 * jax.random.normal(k, shape, jnp.float32)

    return {
        "x": jax.random.normal(ks[0], (BATCH, SEQ, D_MODEL), jnp.float32),
        "mem": jax.random.normal(ks[1], (BATCH, MEM_LEN, D_MODEL), jnp.float32),
        "ffn1_norm": gain(ks[2], (L, D_MODEL)),
        "ffn1_w_gate": w(ks[3], (L, D_MODEL, D_FF), D_MODEL),
        "ffn1_w_up": w(ks[4], (L, D_MODEL, D_FF), D_MODEL),
        "ffn1_w_down": w(ks[5], (L, D_FF, D_MODEL), D_FF),
        "mix_norm": gain(ks[6], (L, D_MODEL)),
        "w_in": w(ks[7], (L, D_MODEL, W_IN_COLS), D_MODEL),
        "gla_w_gate2": w(ks[8], (L, GLA_GATE_RANK, GLA_QK), GLA_GATE_RANK),
        "gla_b_gate2": 0.1 * jax.random.normal(ks[9], (L, GLA_QK), jnp.float32),
        "gla_out_norm": gain(ks[10], (L, GLA_DV)),
        "moba_q_norm": gain(ks[11], (L, MOBA_DH)),
        "moba_k_norm": gain(ks[12], (L, MOBA_DH)),
        "w_out": w(ks[13], (L, MIX_WIDTH, D_MODEL), MIX_WIDTH),
        "xattn_norm": gain(ks[14], (L, D_MODEL)),
        "mem_norm": gain(ks[15], (L, D_MODEL)),
        "xattn_w_q": w(ks[16], (L, D_MODEL, XATTN_WIDTH), D_MODEL),
        "xattn_w_kv": w(ks[17], (L, D_MODEL, 2 * XATTN_WIDTH), D_MODEL),
        "xattn_w_o": w(ks[18], (L, XATTN_WIDTH, D_MODEL), XATTN_WIDTH),
        "xattn_q_norm": gain(ks[19], (L, XATTN_DH)),
        "xattn_k_norm": gain(ks[20], (L, XATTN_DH)),
        "ffn2_norm": gain(ks[21], (L, D_MODEL)),
        "ffn2_w_gate": w(ks[22], (L, D_MODEL, D_FF), D_MODEL),
        "ffn2_w_up": w(ks[23], (L, D_MODEL, D_FF), D_MODEL),
        "ffn2_w_down": w(ks[24], (L, D_FF, D_MODEL), D_FF),
    }


def reference(x, mem, ffn1_norm, ffn1_w_gate, ffn1_w_up, ffn1_w_down, mix_norm, w_in,
              gla_w_gate2, gla_b_gate2, gla_out_norm, moba_q_norm, moba_k_norm, w_out,
              xattn_norm, mem_norm, xattn_w_q, xattn_w_kv, xattn_w_o, xattn_q_norm,
              xattn_k_norm, ffn2_norm, ffn2_w_gate, ffn2_w_up, ffn2_w_down):
    B, T, _ = x.shape
    offs = split_offsets(IN_SPLITS)
    for l in range(DEPTH):
        x = x + 0.5 * swiglu(rms_norm(x, ffn1_norm[l]), ffn1_w_gate[l], ffn1_w_up[l], ffn1_w_down[l])
        h = rms_norm(x, mix_norm[l])
        u = h @ w_in[l]
        g_q, g_k, g_v, g_r, g_lr, m_q, m_k, m_v = jnp.split(u, offs, axis=-1)
        o_gla = gla_group(g_q.reshape(B, T, GLA_HEADS, GLA_DK), g_k.reshape(B, T, GLA_HEADS, GLA_DK),
                          g_v.reshape(B, T, GLA_HEADS, GLA_DV), g_r.reshape(B, T, GLA_HEADS, GLA_DV),
                          g_lr, gla_w_gate2[l], gla_b_gate2[l], gla_out_norm[l])
        o_moba = moba_group(m_q.reshape(B, T, MOBA_HEADS, MOBA_DH), m_k.reshape(B, T, MOBA_HEADS, MOBA_DH),
                            m_v.reshape(B, T, MOBA_HEADS, MOBA_DH), moba_q_norm[l], moba_k_norm[l])
        x = x + jnp.concatenate([o_gla, o_moba], axis=-1) @ w_out[l]
        x = x + memory_cross_attn(rms_norm(x, xattn_norm[l]), rms_norm(mem, mem_norm[l]),
                                  xattn_w_q[l], xattn_w_kv[l], xattn_w_o[l],
                                  xattn_q_norm[l], xattn_k_norm[l])
        x = x + 0.5 * swiglu(rms_norm(x, ffn2_norm[l]), ffn2_w_gate[l], ffn2_w_up[l], ffn2_w_down[l])
    return x
```

```python
import functools

import jax
import jax.numpy as jnp
from jax import lax
from jax.experimental import pallas as pl
from jax.experimental.pallas import tpu as pltpu

F32 = jnp.float32
BF16 = jnp.bfloat16

EPS = 1e-6
NEG_INF = -1e30
D_FF = 5632
GLA_HEADS = 4
GLA_DV = 256
GLA_DK = 128
GLA_QK = GLA_HEADS * GLA_DK
GLA_WIDTH = GLA_HEADS * GLA_DV
GLA_GATE_RANK = 16
GLA_GATE_TAU = 16.0
GLA_CHUNK = 64
MOBA_DH = 128
MOBA_HEADS = 8
MOBA_WIDTH = MOBA_HEADS * MOBA_DH
MOBA_BLOCK = 256
MOBA_TOPK = 3
XATTN_HEADS = 4
XATTN_DH = 128
XATTN_WIDTH = XATTN_HEADS * XATTN_DH

V7X_VMEM_BYTES = 64 * 1024 * 1024
VMEM_LIMIT = V7X_VMEM_BYTES - 8 * 1024 * 1024

FFN_TM = 512
FFN_TF = 512
PROJ_TM = 512
PROJ_TN = 512
GLA_TC = 512
OUTX_TM = 512


def _params(*semantics):
    return pltpu.CompilerParams(dimension_semantics=semantics, vmem_limit_bytes=VMEM_LIMIT)


def _rms(x, g):
    return x * lax.rsqrt(jnp.mean(x * x, axis=-1, keepdims=True) + EPS) * g


def _dot(a, b):
    return jnp.dot(a, b, preferred_element_type=F32)


def _dot_nt(a, b):
    return lax.dot_general(a, b, (((1,), (1,)), ((), ())), preferred_element_type=F32)


def _silu(x):
    return x * (1.0 / (1.0 + jnp.exp(-x)))


def _ffn_kernel(x_ref, g_ref, wg_ref, wu_ref, wd_ref, o_ref, h_ref, acc_ref):
    f = pl.program_id(1)

    @pl.when(f == 0)
    def _():
        h_ref[...] = _rms(x_ref[...], g_ref[...]).astype(BF16)
        acc_ref[...] = jnp.zeros_like(acc_ref)

    h = h_ref[...]
    gate = _dot(h, wg_ref[...].astype(BF16))
    up = _dot(h, wu_ref[...].astype(BF16))
    act = (_silu(gate) * up).astype(BF16)
    acc_ref[...] += _dot(act, wd_ref[...].astype(BF16))

    @pl.when(f == pl.num_programs(1) - 1)
    def _():
        o_ref[...] = x_ref[...] + 0.5 * acc_ref[...]


def _ffn(x, g, wg, wu, wd):
    n, d = x.shape
    grid = (n // FFN_TM, D_FF // FFN_TF)
    return pl.pallas_call(
        _ffn_kernel,
        grid=grid,
        in_specs=[
            pl.BlockSpec((FFN_TM, d), lambda i, f: (i, 0)),
            pl.BlockSpec((1, d), lambda i, f: (0, 0)),
            pl.BlockSpec((d, FFN_TF), lambda i, f: (0, f)),
            pl.BlockSpec((d, FFN_TF), lambda i, f: (0, f)),
            pl.BlockSpec((FFN_TF, d), lambda i, f: (f, 0)),
        ],
        out_specs=pl.BlockSpec((FFN_TM, d), lambda i, f: (i, 0)),
        out_shape=jax.ShapeDtypeStruct((n, d), F32),
        scratch_shapes=[pltpu.VMEM((FFN_TM, d), BF16), pltpu.VMEM((FFN_TM, d), F32)],
        compiler_params=_params("parallel", "arbitrary"),
        name="ffn",
    )(x, g.reshape(1, d), wg, wu, wd)


_U_MQ = (GLA_QK * 2 + GLA_WIDTH * 2) // PROJ_TN
_U_MK = _U_MQ + MOBA_WIDTH // PROJ_TN
_U_MV = _U_MK + MOBA_WIDTH // PROJ_TN
U_COLS = GLA_QK * 2 + GLA_WIDTH * 2 + 3 * MOBA_WIDTH


def _head_rms(y, g):
    parts = [_rms(y[:, k * MOBA_DH:(k + 1) * MOBA_DH], g) for k in range(y.shape[1] // MOBA_DH)]
    return jnp.concatenate(parts, axis=1)


def _proj_kernel(x_ref, g_ref, w_ref, wlr_ref, wg2_ref, bg2_ref, gq_ref, gk_ref,
                 u_ref, la_ref, h_ref):
    j = pl.program_id(1)

    @pl.when(j == 0)
    def _():
        h = _rms(x_ref[...], g_ref[...]).astype(BF16)
        h_ref[...] = h
        lr = _dot(h, wlr_ref[...].astype(BF16))
        pre = _dot(lr.astype(BF16), wg2_ref[...].astype(BF16)) + bg2_ref[...]
        log_sig = -(jnp.maximum(-pre, 0.0) + jnp.log(1.0 + jnp.exp(-jnp.abs(pre))))
        la_ref[...] = log_sig * (1.0 / GLA_GATE_TAU)

    y = _dot(h_ref[...], w_ref[...].astype(BF16))
    is_q = jnp.logical_and(j >= _U_MQ, j < _U_MK)
    is_k = jnp.logical_and(j >= _U_MK, j < _U_MV)

    @pl.when(is_q)
    def _():
        u_ref[...] = _head_rms(y, gq_ref[...])

    @pl.when(is_k)
    def _():
        u_ref[...] = _head_rms(y, gk_ref[...])

    @pl.when(jnp.logical_not(jnp.logical_or(is_q, is_k)))
    def _():
        u_ref[...] = y


def _proj(x1, g, w_main, w_lr, wg2, bg2, gq, gk):
    n, d = x1.shape
    grid = (n // PROJ_TM, U_COLS // PROJ_TN)
    const = lambda i, j: (0, 0)
    return pl.pallas_call(
        _proj_kernel,
        grid=grid,
        in_specs=[
            pl.BlockSpec((PROJ_TM, d), lambda i, j: (i, 0)),
            pl.BlockSpec((1, d), const),
            pl.BlockSpec((d, PROJ_TN), lambda i, j: (0, j)),
            pl.BlockSpec((d, GLA_GATE_RANK), const),
            pl.BlockSpec((GLA_GATE_RANK, GLA_QK), const),
            pl.BlockSpec((1, GLA_QK), const),
            pl.BlockSpec((1, MOBA_DH), const),
            pl.BlockSpec((1, MOBA_DH), const),
        ],
        out_specs=[
            pl.BlockSpec((PROJ_TM, PROJ_TN), lambda i, j: (i, j)),
            pl.BlockSpec((PROJ_TM, GLA_QK), lambda i, j: (i, 0)),
        ],
        out_shape=[jax.ShapeDtypeStruct((n, U_COLS), F32),
                   jax.ShapeDtypeStruct((n, GLA_QK), F32)],
        scratch_shapes=[pltpu.VMEM((PROJ_TM, d), BF16)],
        compiler_params=_params("parallel", "arbitrary"),
        name="mixproj",
    )(x1, g.reshape(1, d), w_main, w_lr, wg2, bg2.reshape(1, -1), gq.reshape(1, -1), gk.reshape(1, -1))


def _dot_exact_rhs(m, a):
    a1 = a.astype(BF16)
    r1 = a - a1.astype(F32)
    a2 = r1.astype(BF16)
    a3 = (r1 - a2.astype(F32)).astype(BF16)
    return _dot(m, a1) + _dot(m, a2) + _dot(m, a3)


def _gla_kernel(q_ref, k_ref, v_ref, r_ref, la_ref, go_ref, o_ref, s_ref):
    t = pl.program_id(2)
    tc = GLA_TC
    nchunk = tc // GLA_CHUNK

    @pl.when(t == 0)
    def _():
        s_ref[...] = jnp.zeros_like(s_ref)

    row = lax.broadcasted_iota(jnp.int32, (tc, tc), 0)
    col = lax.broadcasted_iota(jnp.int32, (tc, tc), 1)
    same_chunk = (row // GLA_CHUNK) == (col // GLA_CHUNK)
    causal = jnp.logical_and(same_chunk, col <= row)

    la = la_ref[...]
    bcum = _dot_exact_rhs(causal.astype(BF16), la)
    btot = _dot_exact_rhs(same_chunk.astype(BF16), la)

    k = k_ref[...]
    q_dec = (q_ref[...] * (GLA_DK ** -0.5)) * jnp.exp(bcum)
    k_inv = k * jnp.exp(-bcum)
    k_tail_t = (k * jnp.exp(btot - bcum)).T
    decay_t = jnp.exp(btot).T
    v = v_ref[...].astype(BF16)

    q_dec_b = q_dec.astype(BF16)
    att = jnp.where(causal, _dot_nt(q_dec_b, k_inv.astype(BF16)), 0.0)
    o = _dot(att.astype(BF16), v)

    lane_chunk = lax.broadcasted_iota(jnp.int32, (GLA_DK, tc), 1) // GLA_CHUNK
    state = s_ref[...]
    inter = []
    for c in range(nchunk):
        rows = slice(c * GLA_CHUNK, (c + 1) * GLA_CHUNK)
        inter.append(_dot(q_dec_b[rows], state.astype(BF16)))
        kt_c = jnp.where(lane_chunk == c, k_tail_t, 0.0).astype(BF16)
        dec_c = decay_t[:, c * GLA_CHUNK:c * GLA_CHUNK + 1]
        state = dec_c * state + _dot(kt_c, v)
    s_ref[...] = state
    o = o + jnp.concatenate(inter, axis=0)

    o = _rms(o, go_ref[...]) * _silu(r_ref[...])
    o_ref[...] = o


def _gla(u, la, g_out, batch, seq):
    n = u.shape[0]
    nt = seq // GLA_TC
    tok = lambda b, h, t: b * nt + t
    k_blk0 = GLA_QK // GLA_DK
    v_blk0 = 2 * GLA_QK // GLA_DV
    r_blk0 = v_blk0 + GLA_WIDTH // GLA_DV
    return pl.pallas_call(
        _gla_kernel,
        grid=(batch, GLA_HEADS, nt),
        in_specs=[
            pl.BlockSpec((GLA_TC, GLA_DK), lambda b, h, t: (tok(b, h, t), h)),
            pl.BlockSpec((GLA_TC, GLA_DK), lambda b, h, t: (tok(b, h, t), k_blk0 + h)),
            pl.BlockSpec((GLA_TC, GLA_DV), lambda b, h, t: (tok(b, h, t), v_blk0 + h)),
            pl.BlockSpec((GLA_TC, GLA_DV), lambda b, h, t: (tok(b, h, t), r_blk0 + h)),
            pl.BlockSpec((GLA_TC, GLA_DK), lambda b, h, t: (tok(b, h, t), h)),
            pl.BlockSpec((1, GLA_DV), lambda b, h, t: (0, 0)),
        ],
        out_specs=pl.BlockSpec((GLA_TC, GLA_DV), lambda b, h, t: (tok(b, h, t), h)),
        out_shape=jax.ShapeDtypeStruct((n, GLA_WIDTH), F32),
        scratch_shapes=[pltpu.VMEM((GLA_DK, GLA_DV), F32)],
        compiler_params=_params("parallel", "parallel", "arbitrary"),
        name="gla",
    )(u, u, u, u, la, g_out.reshape(1, -1))


def _moba_kernel(q_ref, k_ref, v_ref, o_ref, kmean_ref):
    qi = pl.program_id(2)
    bs = MOBA_BLOCK
    nb = k_ref.shape[0] // bs
    scale = MOBA_DH ** -0.5

    @pl.when(qi == 0)
    def _():
        kmean_ref[...] = jnp.sum(k_ref[...].reshape(nb, bs, MOBA_DH), axis=1) * (1.0 / bs)

    q = q_ref[...]
    qb = q.astype(BF16)

    gate = lax.dot_general(q, kmean_ref[...], (((1,), (1,)), ((), ())),
                           precision=lax.Precision.HIGHEST, preferred_element_type=F32)
    blk = lax.broadcasted_iota(jnp.int32, (bs, nb), 1)
    past = blk < qi
    gate = jnp.where(past, gate, NEG_INF)
    rank = jnp.zeros((bs, nb), jnp.int32)
    for j in range(nb):
        gj = gate[:, j:j + 1]
        ahead = jnp.logical_or(gj > gate, jnp.logical_and(gj == gate, blk > j))
        rank = rank + ahead.astype(jnp.int32)
    chosen = jnp.logical_and(past, rank < MOBA_TOPK).astype(F32)

    start = pl.multiple_of(qi * bs, bs)
    s = _dot_nt(qb, k_ref[pl.ds(start, bs), :].astype(BF16)) * scale
    rq = lax.broadcasted_iota(jnp.int32, (bs, bs), 0)
    ck = lax.broadcasted_iota(jnp.int32, (bs, bs), 1)
    s = jnp.where(ck <= rq, s, NEG_INF)
    m0 = jnp.max(s, axis=-1, keepdims=True)
    p = jnp.exp(s - m0)
    l0 = jnp.sum(p, axis=-1, keepdims=True)
    acc0 = _dot(p.astype(BF16), v_ref[pl.ds(start, bs), :].astype(BF16))

    def body(j, carry):
        m, l, acc = carry
        off = pl.multiple_of(j * bs, bs)
        sel = jnp.sum(jnp.where(blk == j, chosen, 0.0), axis=-1, keepdims=True) > 0.5
        s = _dot_nt(qb, k_ref[pl.ds(off, bs), :].astype(BF16)) * scale
        s = jnp.where(sel, s, NEG_INF)
        m_new = jnp.maximum(m, jnp.max(s, axis=-1, keepdims=True))
        alpha = jnp.exp(m - m_new)
        p = jnp.exp(s - m_new)
        l = alpha * l + jnp.sum(p, axis=-1, keepdims=True)
        acc = alpha * acc + _dot(p.astype(BF16), v_ref[pl.ds(off, bs), :].astype(BF16))
        return m_new, l, acc

    m, l, acc = lax.fori_loop(0, qi, body, (m0, l0, acc0))
    o_ref[...] = acc * (1.0 / l)


def _moba(u, batch, seq):
    n = u.shape[0]
    nb = seq // MOBA_BLOCK
    q_blk0 = (2 * GLA_QK + 2 * GLA_WIDTH) // MOBA_DH
    k_blk0 = q_blk0 + MOBA_HEADS
    v_blk0 = k_blk0 + MOBA_HEADS
    return pl.pallas_call(
        _moba_kernel,
        grid=(batch, MOBA_HEADS, nb),
        in_specs=[
            pl.BlockSpec((MOBA_BLOCK, MOBA_DH), lambda b, h, i: (b * nb + i, q_blk0 + h)),
            pl.BlockSpec((seq, MOBA_DH), lambda b, h, i: (b, k_blk0 + h)),
            pl.BlockSpec((seq, MOBA_DH), lambda b, h, i: (b, v_blk0 + h)),
        ],
        out_specs=pl.BlockSpec((MOBA_BLOCK, MOBA_DH), lambda b, h, i: (b * nb + i, h)),
        out_shape=jax.ShapeDtypeStruct((n, MOBA_WIDTH), F32),
        scratch_shapes=[pltpu.VMEM((nb, MOBA_DH), F32)],
        compiler_params=_params("parallel", "parallel", "arbitrary"),
        name="moba",
    )(u, u, u)


def _memkv_kernel(mem_ref, g_ref, wkv_ref, gk_ref, k_ref, v_ref):
    h = _rms(mem_ref[...], g_ref[...]).astype(BF16)
    kv = _dot(h, wkv_ref[...])
    gk = gk_ref[...]
    ks = [_rms(kv[:, i * XATTN_DH:(i + 1) * XATTN_DH], gk) for i in range(XATTN_HEADS)]
    k_ref[...] = jnp.concatenate(ks, axis=1)
    v_ref[...] = kv[:, XATTN_WIDTH:]


def _memkv(mem, g, wkv_b, gk):
    m, d = mem.shape
    return pl.pallas_call(
        _memkv_kernel,
        out_shape=[jax.ShapeDtypeStruct((m, XATTN_WIDTH), F32),
                   jax.ShapeDtypeStruct((m, XATTN_WIDTH), F32)],
        compiler_params=pltpu.CompilerParams(vmem_limit_bytes=VMEM_LIMIT),
        name="memkv",
    )(mem, g.reshape(1, d), wkv_b, gk.reshape(1, -1))


def _outx_kernel(x1_ref, og_ref, om_ref, wout_ref, gx_ref, wq_ref, gq_ref, kx_ref, vx_ref, wo_ref,
                 o_ref):
    x2 = (x1_ref[...]
          + _dot(og_ref[...].astype(BF16), wout_ref[:GLA_WIDTH, :])
          + _dot(om_ref[...].astype(BF16), wout_ref[GLA_WIDTH:, :]))
    h = _rms(x2, gx_ref[...]).astype(BF16)
    q = _dot(h, wq_ref[...])
    gq = gq_ref[...]
    scale = XATTN_DH ** -0.5
    heads = []
    for i in range(XATTN_HEADS):
        cols = slice(i * XATTN_DH, (i + 1) * XATTN_DH)
        qh = _rms(q[:, cols], gq).astype(BF16)
        s = _dot_nt(qh, kx_ref[:, cols].astype(BF16)) * scale
        p = jnp.exp(s - jnp.max(s, axis=-1, keepdims=True))
        p = p * (1.0 / jnp.sum(p, axis=-1, keepdims=True))
        heads.append(_dot(p.astype(BF16), vx_ref[:, cols].astype(BF16)))
    o = jnp.concatenate(heads, axis=1).astype(BF16)
    o_ref[...] = x2 + _dot(o, wo_ref[...])


def _outx(x1, og, om, wout_b, gx, wq_b, gq, kx, vx, wo_b, seq, mem_len):
    n, d = x1.shape
    tiles_per_seq = seq // OUTX_TM
    const = lambda i: (0, 0)
    return pl.pallas_call(
        _outx_kernel,
        grid=(n // OUTX_TM,),
        in_specs=[
            pl.BlockSpec((OUTX_TM, d), lambda i: (i, 0)),
            pl.BlockSpec((OUTX_TM, GLA_WIDTH), lambda i: (i, 0)),
            pl.BlockSpec((OUTX_TM, MOBA_WIDTH), lambda i: (i, 0)),
            pl.BlockSpec((GLA_WIDTH + MOBA_WIDTH, d), const),
            pl.BlockSpec((1, d), const),
            pl.BlockSpec((d, XATTN_WIDTH), const),
            pl.BlockSpec((1, XATTN_DH), const),
            pl.BlockSpec((mem_len, XATTN_WIDTH), lambda i: (i // tiles_per_seq, 0)),
            pl.BlockSpec((mem_len, XATTN_WIDTH), lambda i: (i // tiles_per_seq, 0)),
            pl.BlockSpec((XATTN_WIDTH, d), const),
        ],
        out_specs=pl.BlockSpec((OUTX_TM, d), lambda i: (i, 0)),
        out_shape=jax.ShapeDtypeStruct((n, d), F32),
        compiler_params=_params("parallel"),
        name="outx",
    )(x1, og, om, wout_b, gx.reshape(1, d), wq_b, gq.reshape(1, -1), kx, vx, wo_b)


def kernel(x, mem, ffn1_norm, ffn1_w_gate, ffn1_w_up, ffn1_w_down, mix_norm, w_in, gla_w_gate2, gla_b_gate2, gla_out_norm, moba_q_norm, moba_k_norm, w_out, xattn_norm, mem_norm, xattn_w_q, xattn_w_kv, xattn_w_o, xattn_q_norm, xattn_k_norm, ffn2_norm, ffn2_w_gate, ffn2_w_up, ffn2_w_down):
    batch, seq, d = x.shape
    mem_len = mem.shape[1]
    depth = ffn1_norm.shape[0]
    assert seq % GLA_TC == 0 and seq % MOBA_BLOCK == 0 and seq % OUTX_TM == 0
    assert (batch * seq) % FFN_TM == 0 and (batch * seq) % PROJ_TM == 0

    xf = x.reshape(batch * seq, d)
    memf = mem.reshape(batch * mem_len, d)
    lr0 = 2 * GLA_QK + 2 * GLA_WIDTH
    for l in range(depth):
        w_main = jnp.concatenate([w_in[l][:, :lr0], w_in[l][:, lr0 + GLA_GATE_RANK:]], axis=1)
        w_lr = w_in[l][:, lr0:lr0 + GLA_GATE_RANK]

        x1 = _ffn(xf, ffn1_norm[l], ffn1_w_gate[l], ffn1_w_up[l], ffn1_w_down[l])
        u, la = _proj(x1, mix_norm[l], w_main, w_lr, gla_w_gate2[l], gla_b_gate2[l],
                      moba_q_norm[l], moba_k_norm[l])
        o_gla = _gla(u, la, gla_out_norm[l], batch, seq)
        o_moba = _moba(u, batch, seq)
        kx, vx = _memkv(memf, mem_norm[l], xattn_w_kv[l].astype(BF16), xattn_k_norm[l])
        x3 = _outx(x1, o_gla, o_moba, w_out[l].astype(BF16), xattn_norm[l],
                   xattn_w_q[l].astype(BF16), xattn_q_norm[l], kx, vx,
                   xattn_w_o[l].astype(BF16), seq, mem_len)
        xf = _ffn(x3, ffn2_norm[l], ffn2_w_gate[l], ffn2_w_up[l], ffn2_w_down[l])
    return xf.reshape(batch, seq, d)
```

```python
import jax
import jax.numpy as jnp
from jax import lax
from jax.experimental import pallas as pl
from jax.experimental.pallas import tpu as pltpu

F32 = jnp.float32
BF16 = jnp.bfloat16

EPS = 1e-6
NEG_INF = -1e30
D_FF = 5632
GLA_HEADS = 4
GLA_DV = 256
GLA_DK = 128
GLA_QK = GLA_HEADS * GLA_DK
GLA_WIDTH = GLA_HEADS * GLA_DV
GLA_COLS = 2 * GLA_QK + 2 * GLA_WIDTH
GLA_GATE_RANK = 16
GLA_GATE_TAU = 16.0
GLA_CHUNK = 64
MOBA_DH = 128
MOBA_HEADS = 8
MOBA_WIDTH = MOBA_HEADS * MOBA_DH
MOBA_BLOCK = 256
MOBA_TOPK = 3
XATTN_HEADS = 4
XATTN_DH = 128
XATTN_WIDTH = XATTN_HEADS * XATTN_DH

V7X_VMEM_BYTES = 64 * 1024 * 1024
VMEM_LIMIT = V7X_VMEM_BYTES - 6 * 1024 * 1024
MXU_DIM = 256

FFN_TM = 1024
FFN_TF = 256
PROJ_TM = 1024
PROJ_TN = 512
GLA_TC = 512
MOBA_G = 2
OUTX_TM = 512


def _params(*semantics):
    return pltpu.CompilerParams(dimension_semantics=semantics, vmem_limit_bytes=VMEM_LIMIT)


def _rms(x, g):
    return x * lax.rsqrt(jnp.mean(x * x, axis=-1, keepdims=True) + EPS) * g


def _dot(a, b):
    return jnp.dot(a, b, preferred_element_type=F32)


def _dot_nt(a, b):
    return lax.dot_general(a, b, (((1,), (1,)), ((), ())), preferred_element_type=F32)


def _silu(x):
    return x * (1.0 / (1.0 + jnp.exp(-x)))


def _ffn_kernel(x_ref, g_ref, wg_ref, wu_ref, wd_ref, o_ref, h_ref):
    f = pl.program_id(1)

    @pl.when(f == 0)
    def _():
        h_ref[...] = _rms(x_ref[...], g_ref[...]).astype(BF16)
        o_ref[...] = jnp.zeros_like(o_ref)

    h = h_ref[...]
    gate = _dot(h, wg_ref[...].astype(BF16))
    up = _dot(h, wu_ref[...].astype(BF16))
    act = (_silu(gate) * up).astype(BF16)
    o_ref[...] += _dot(act, wd_ref[...].astype(BF16))

    @pl.when(f == pl.num_programs(1) - 1)
    def _():
        o_ref[...] = x_ref[...] + 0.5 * o_ref[...]


def _ffn(x, g, wg, wu, wd):
    n, d = x.shape
    grid = (n // FFN_TM, D_FF // FFN_TF)
    return pl.pallas_call(
        _ffn_kernel,
        grid=grid,
        in_specs=[
            pl.BlockSpec((FFN_TM, d), lambda i, f: (i, 0)),
            pl.BlockSpec((1, d), lambda i, f: (0, 0)),
            pl.BlockSpec((d, FFN_TF), lambda i, f: (0, f)),
            pl.BlockSpec((d, FFN_TF), lambda i, f: (0, f)),
            pl.BlockSpec((FFN_TF, d), lambda i, f: (f, 0)),
        ],
        out_specs=pl.BlockSpec((FFN_TM, d), lambda i, f: (i, 0)),
        out_shape=jax.ShapeDtypeStruct((n, d), F32),
        scratch_shapes=[pltpu.VMEM((FFN_TM, d), BF16)],
        compiler_params=_params("parallel", "arbitrary"),
        name="ffn",
    )(x, g.reshape(1, d), wg, wu, wd)


def _projg_kernel(x_ref, g_ref, w_ref, wlr_ref, wg2_ref, bg2_ref, u_ref, la_ref, hn_ref):
    @pl.when(pl.program_id(1) == 0)
    def _():
        h = _rms(x_ref[...], g_ref[...]).astype(BF16)
        hn_ref[...] = h
        lr = _dot(h, wlr_ref[...].astype(BF16))
        pre = _dot(lr.astype(BF16), wg2_ref[...].astype(BF16)) + bg2_ref[...]
        log_sig = -(jnp.maximum(-pre, 0.0) + jnp.log(1.0 + jnp.exp(-jnp.abs(pre))))
        la_ref[...] = log_sig * (1.0 / GLA_GATE_TAU)

    u_ref[...] = _dot(hn_ref[...], w_ref[...].astype(BF16))


def _projg(x1, g, w_in, w_lr, wg2, bg2):
    n, d = x1.shape
    const = lambda i, j: (0, 0)
    return pl.pallas_call(
        _projg_kernel,
        grid=(n // PROJ_TM, GLA_COLS // PROJ_TN),
        in_specs=[
            pl.BlockSpec((PROJ_TM, d), lambda i, j: (i, 0)),
            pl.BlockSpec((1, d), const),
            pl.BlockSpec((d, PROJ_TN), lambda i, j: (0, j)),
            pl.BlockSpec((d, GLA_GATE_RANK), const),
            pl.BlockSpec((GLA_GATE_RANK, GLA_QK), const),
            pl.BlockSpec((1, GLA_QK), const),
        ],
        out_specs=[
            pl.BlockSpec((PROJ_TM, PROJ_TN), lambda i, j: (i, j)),
            pl.BlockSpec((PROJ_TM, GLA_QK), lambda i, j: (i, 0)),
            pl.BlockSpec((PROJ_TM, d), lambda i, j: (i, 0)),
        ],
        out_shape=[jax.ShapeDtypeStruct((n, GLA_COLS), F32),
                   jax.ShapeDtypeStruct((n, GLA_QK), F32),
                   jax.ShapeDtypeStruct((n, d), BF16)],
        compiler_params=_params("parallel", "arbitrary"),
        name="projg",
    )(x1, g.reshape(1, d), w_in, w_lr, wg2, bg2.reshape(1, -1))


def _projq_kernel(hn_ref, wt_ref, g_ref, qt_ref):
    yt = _dot_nt(wt_ref[...].astype(BF16), hn_ref[...])
    g = g_ref[...]
    parts = []
    for k in range(PROJ_TN // MOBA_DH):
        p = yt[k * MOBA_DH:(k + 1) * MOBA_DH, :]
        inv = lax.rsqrt(jnp.mean(p * p, axis=0, keepdims=True) + EPS)
        parts.append(p * inv * g)
    qt_ref[...] = jnp.concatenate(parts, axis=0)


def _projk_kernel(hn_ref, w_ref, g_ref, kb_ref, kmean_ref):
    y = _dot(hn_ref[...], w_ref[...].astype(BF16))
    g = g_ref[...]
    parts = [_rms(y[:, k * MOBA_DH:(k + 1) * MOBA_DH], g) for k in range(PROJ_TN // MOBA_DH)]
    kn = jnp.concatenate(parts, axis=1)
    kb_ref[...] = kn.astype(BF16)
    blocks = PROJ_TM // MOBA_BLOCK
    kmean_ref[0] = jnp.sum(kn.reshape(blocks, MOBA_BLOCK, PROJ_TN), axis=1) * (1.0 / MOBA_BLOCK)


def _projv_kernel(hn_ref, wt_ref, v3_ref):
    yt = _dot_nt(wt_ref[...].astype(BF16), hn_ref[...]).astype(BF16)
    for t in range(PROJ_TM // MOBA_BLOCK):
        v3_ref[t] = yt[:, t * MOBA_BLOCK:(t + 1) * MOBA_BLOCK]


def _proj_moba(hn, w_q_t, w_k, w_v_t, gq, gk):
    n, d = hn.shape
    grid = (n // PROJ_TM, MOBA_WIDTH // PROJ_TN)
    hn_spec = pl.BlockSpec((PROJ_TM, d), lambda i, j: (i, 0))
    wt_spec = pl.BlockSpec((PROJ_TN, d), lambda i, j: (j, 0))
    params = _params("parallel", "arbitrary")
    blocks = PROJ_TM // MOBA_BLOCK

    q_t = pl.pallas_call(
        _projq_kernel, grid=grid,
        in_specs=[hn_spec, wt_spec, pl.BlockSpec((MOBA_DH, 1), lambda i, j: (0, 0))],
        out_specs=pl.BlockSpec((PROJ_TN, PROJ_TM), lambda i, j: (j, i)),
        out_shape=jax.ShapeDtypeStruct((MOBA_WIDTH, n), F32),
        compiler_params=params, name="projq",
    )(hn, w_q_t, gq.reshape(-1, 1))

    kb, kmean = pl.pallas_call(
        _projk_kernel, grid=grid,
        in_specs=[hn_spec, pl.BlockSpec((d, PROJ_TN), lambda i, j: (0, j)),
                  pl.BlockSpec((1, MOBA_DH), lambda i, j: (0, 0))],
        out_specs=[pl.BlockSpec((PROJ_TM, PROJ_TN), lambda i, j: (i, j)),
                   pl.BlockSpec((1, blocks, PROJ_TN), lambda i, j: (i, 0, j))],
        out_shape=[jax.ShapeDtypeStruct((n, MOBA_WIDTH), BF16),
                   jax.ShapeDtypeStruct((n // PROJ_TM, blocks, MOBA_WIDTH), F32)],
        compiler_params=params, name="projk",
    )(hn, w_k, gk.reshape(1, -1))

    v3 = pl.pallas_call(
        _projv_kernel, grid=grid,
        in_specs=[hn_spec, wt_spec],
        out_specs=pl.BlockSpec((blocks, PROJ_TN, MOBA_BLOCK), lambda i, j: (i, j, 0)),
        out_shape=jax.ShapeDtypeStruct((n // MOBA_BLOCK, MOBA_WIDTH, MOBA_BLOCK), BF16),
        compiler_params=params, name="projv",
    )(hn, w_v_t)
    return q_t, kb, kmean.reshape(n // MOBA_BLOCK, MOBA_WIDTH), v3


def _dot_exact_rhs(m, a):
    a1 = a.astype(BF16)
    r1 = a - a1.astype(F32)
    a2 = r1.astype(BF16)
    a3 = (r1 - a2.astype(F32)).astype(BF16)
    return _dot(m, a1) + _dot(m, a2) + _dot(m, a3)


def _gla_kernel(q_ref, k_ref, v_ref, r_ref, la_ref, go_ref, o_ref, s_ref):
    t = pl.program_id(2)
    tc = GLA_TC
    nchunk = tc // GLA_CHUNK

    @pl.when(t == 0)
    def _():
        s_ref[...] = jnp.zeros_like(s_ref)

    row = lax.broadcasted_iota(jnp.int32, (tc, tc), 0)
    col = lax.broadcasted_iota(jnp.int32, (tc, tc), 1)
    same_chunk = (row // GLA_CHUNK) == (col // GLA_CHUNK)
    causal = jnp.logical_and(same_chunk, col <= row)

    la = la_ref[...]
    bcum = _dot_exact_rhs(causal.astype(BF16), la)
    btot = _dot_exact_rhs(same_chunk.astype(BF16), la)

    k = k_ref[...]
    q_dec = (q_ref[...] * (GLA_DK ** -0.5)) * jnp.exp(bcum)
    k_inv = k * jnp.exp(-bcum)
    k_tail_t = (k * jnp.exp(btot - bcum)).T
    decay_t = jnp.exp(btot).T
    v = v_ref[...].astype(BF16)

    q_dec_b = q_dec.astype(BF16)
    att = jnp.where(causal, _dot_nt(q_dec_b, k_inv.astype(BF16)), 0.0)
    o = _dot(att.astype(BF16), v)

    lane_chunk = lax.broadcasted_iota(jnp.int32, (GLA_DK, tc), 1) // GLA_CHUNK
    state = s_ref[...]
    inter = []
    for c in range(nchunk):
        rows = slice(c * GLA_CHUNK, (c + 1) * GLA_CHUNK)
        inter.append(_dot(q_dec_b[rows], state.astype(BF16)))
        kt_c = jnp.where(lane_chunk == c, k_tail_t, 0.0).astype(BF16)
        dec_c = decay_t[:, c * GLA_CHUNK:c * GLA_CHUNK + 1]
        state = dec_c * state + _dot(kt_c, v)
    s_ref[...] = state
    o = o + jnp.concatenate(inter, axis=0)

    o = _rms(o, go_ref[...]) * _silu(r_ref[...])
    o_ref[...] = o


def _gla(u, la, g_out, batch, seq):
    n = u.shape[0]
    nt = seq // GLA_TC
    tok = lambda b, h, t: b * nt + t
    k_blk0 = GLA_QK // GLA_DK
    v_blk0 = 2 * GLA_QK // GLA_DV
    r_blk0 = v_blk0 + GLA_WIDTH // GLA_DV
    return pl.pallas_call(
        _gla_kernel,
        grid=(batch, GLA_HEADS, nt),
        in_specs=[
            pl.BlockSpec((GLA_TC, GLA_DK), lambda b, h, t: (tok(b, h, t), h)),
            pl.BlockSpec((GLA_TC, GLA_DK), lambda b, h, t: (tok(b, h, t), k_blk0 + h)),
            pl.BlockSpec((GLA_TC, GLA_DV), lambda b, h, t: (tok(b, h, t), v_blk0 + h)),
            pl.BlockSpec((GLA_TC, GLA_DV), lambda b, h, t: (tok(b, h, t), r_blk0 + h)),
            pl.BlockSpec((GLA_TC, GLA_DK), lambda b, h, t: (tok(b, h, t), h)),
            pl.BlockSpec((1, GLA_DV), lambda b, h, t: (0, 0)),
        ],
        out_specs=pl.BlockSpec((GLA_TC, GLA_DV), lambda b, h, t: (tok(b, h, t), h)),
        out_shape=jax.ShapeDtypeStruct((n, GLA_WIDTH), F32),
        scratch_shapes=[pltpu.VMEM((GLA_DK, GLA_DV), F32)],
        compiler_params=_params("parallel", "parallel", "arbitrary"),
        name="gla",
    )(u, u, u, u, la, g_out.reshape(1, -1))


def _moba_kernel(qt_ref, kb_ref, v3_ref, kmean_ref, o_ref, kaug_ref):
    qi = pl.program_id(2)
    bs = MOBA_BLOCK
    nb = kb_ref.shape[0] // bs
    dh = MOBA_DH
    scale = dh ** -0.5

    @pl.when(qi == 0)
    def _():
        seq = kb_ref.shape[0]
        row_blk = lax.broadcasted_iota(jnp.int32, (seq, MXU_DIM - dh), 0) // bs
        lane = lax.broadcasted_iota(jnp.int32, (seq, MXU_DIM - dh), 1)
        onehot = (row_blk == lane).astype(BF16)
        for g in range(MOBA_G):
            kaug_ref[g, :, :dh] = kb_ref[:, g * dh:(g + 1) * dh]
            kaug_ref[g, :, dh:] = onehot

    blk = lax.broadcasted_iota(jnp.int32, (nb, bs), 0)
    past = blk < qi
    key_row = lax.broadcasted_iota(jnp.int32, (bs, bs), 0)
    qry_col = lax.broadcasted_iota(jnp.int32, (bs, bs), 1)
    own_causal = key_row <= qry_col
    ones_rows = jnp.ones((16, bs), BF16)
    own = pl.multiple_of(qi * bs, bs)

    q_aug, carry0 = [], []
    for g in range(MOBA_G):
        qt = qt_ref[g * dh:(g + 1) * dh, :]
        gate = jnp.dot(kmean_ref[:, g * dh:(g + 1) * dh], qt,
                       precision=lax.Precision.HIGHEST, preferred_element_type=F32)
        gate = jnp.where(past, gate, NEG_INF)
        rank = jnp.zeros((nb, bs), jnp.int32)
        for j in range(nb):
            gj = gate[j:j + 1, :]
            ahead = jnp.logical_or(gj > gate, jnp.logical_and(gj == gate, blk > j))
            rank = rank + ahead.astype(jnp.int32)
        chosen = jnp.logical_and(past, rank < MOBA_TOPK)
        bias = jnp.where(chosen, 0.0, NEG_INF)
        qa = jnp.concatenate([(qt * scale).astype(BF16), bias.astype(BF16),
                              jnp.zeros((MXU_DIM - dh - nb, bs), BF16)], axis=0)
        q_aug.append(qa)

        s = _dot(kaug_ref[g, pl.ds(own, bs), :dh], qa[:dh])
        s = jnp.where(own_causal, s, NEG_INF)
        m = jnp.max(s, axis=0, keepdims=True)
        p = jnp.exp(s - m).astype(BF16)
        l = _dot(ones_rows, p)[0:1]
        acc = _dot(v3_ref[qi, g * dh:(g + 1) * dh, :], p)
        carry0.append((m, l, acc))

    def body(t, carry):
        out = []
        for g in range(MOBA_G):
            m, l, acc = carry[g]
            off = pl.multiple_of(t * 2 * bs, 2 * bs)
            s = _dot(kaug_ref[g, pl.ds(off, 2 * bs), :], q_aug[g])
            m_new = jnp.maximum(m, jnp.max(s, axis=0, keepdims=True))
            alpha = jnp.exp(m - m_new)
            p = jnp.exp(s - m_new).astype(BF16)
            l = alpha * l + _dot(jnp.ones((16, 2 * bs), BF16), p)[0:1]
            pv = (_dot(v3_ref[2 * t, g * dh:(g + 1) * dh, :], p[:bs])
                  + _dot(v3_ref[2 * t + 1, g * dh:(g + 1) * dh, :], p[bs:]))
            out.append((m_new, l, alpha * acc + pv))
        return tuple(out)

    carry = lax.fori_loop(0, (qi + 1) // 2, body, tuple(carry0))
    for g in range(MOBA_G):
        m, l, acc = carry[g]
        o_ref[:, g * dh:(g + 1) * dh] = (acc * (1.0 / l)).T


def _moba(q_t, kb, kmean, v3, batch, seq):
    n = kb.shape[0]
    nb = seq // MOBA_BLOCK
    gw = MOBA_G * MOBA_DH
    return pl.pallas_call(
        _moba_kernel,
        grid=(batch, MOBA_HEADS // MOBA_G, nb),
        in_specs=[
            pl.BlockSpec((gw, MOBA_BLOCK), lambda b, h, i: (h, b * nb + i)),
            pl.BlockSpec((seq, gw), lambda b, h, i: (b, h)),
            pl.BlockSpec((nb, gw, MOBA_BLOCK), lambda b, h, i: (b, h, 0)),
            pl.BlockSpec((nb, gw), lambda b, h, i: (b, h)),
        ],
        out_specs=pl.BlockSpec((MOBA_BLOCK, gw), lambda b, h, i: (b * nb + i, h)),
        out_shape=jax.ShapeDtypeStruct((n, MOBA_WIDTH), F32),
        scratch_shapes=[pltpu.VMEM((MOBA_G, seq, MXU_DIM), BF16)],
        compiler_params=_params("parallel", "parallel", "arbitrary"),
        name="moba",
    )(q_t, kb, v3, kmean)


def _memkv_kernel(mem_ref, g_ref, wkv_ref, gk_ref, k_ref, v_ref):
    h = _rms(mem_ref[...], g_ref[...]).astype(BF16)
    kv = _dot(h, wkv_ref[...])
    gk = gk_ref[...]
    ks = [_rms(kv[:, i * XATTN_DH:(i + 1) * XATTN_DH], gk) for i in range(XATTN_HEADS)]
    k_ref[...] = jnp.concatenate(ks, axis=1)
    v_ref[...] = kv[:, XATTN_WIDTH:]


def _memkv(mem, g, wkv_b, gk):
    m, d = mem.shape
    return pl.pallas_call(
        _memkv_kernel,
        out_shape=[jax.ShapeDtypeStruct((m, XATTN_WIDTH), F32),
                   jax.ShapeDtypeStruct((m, XATTN_WIDTH), F32)],
        compiler_params=pltpu.CompilerParams(vmem_limit_bytes=VMEM_LIMIT),
        name="memkv",
    )(mem, g.reshape(1, d), wkv_b, gk.reshape(1, -1))


def _outx_kernel(x1_ref, og_ref, om_ref, wout_ref, gx_ref, wq_ref, gq_ref, kx_ref, vx_ref, wo_ref,
                 o_ref):
    x2 = (x1_ref[...]
          + _dot(og_ref[...].astype(BF16), wout_ref[:GLA_WIDTH, :])
          + _dot(om_ref[...].astype(BF16), wout_ref[GLA_WIDTH:, :]))
    h = _rms(x2, gx_ref[...]).astype(BF16)
    q = _dot(h, wq_ref[...])
    gq = gq_ref[...]
    scale = XATTN_DH ** -0.5
    heads = []
    for i in range(XATTN_HEADS):
        cols = slice(i * XATTN_DH, (i + 1) * XATTN_DH)
        qh = _rms(q[:, cols], gq).astype(BF16)
        s = _dot_nt(qh, kx_ref[:, cols].astype(BF16)) * scale
        p = jnp.exp(s - jnp.max(s, axis=-1, keepdims=True))
        p = p * (1.0 / jnp.sum(p, axis=-1, keepdims=True))
        heads.append(_dot(p.astype(BF16), vx_ref[:, cols].astype(BF16)))
    o = jnp.concatenate(heads, axis=1).astype(BF16)
    o_ref[...] = x2 + _dot(o, wo_ref[...])


def _outx(x1, og, om, wout_b, gx, wq_b, gq, kx, vx, wo_b, seq, mem_len):
    n, d = x1.shape
    tiles_per_seq = seq // OUTX_TM
    const = lambda i: (0, 0)
    return pl.pallas_call(
        _outx_kernel,
        grid=(n // OUTX_TM,),
        in_specs=[
            pl.BlockSpec((OUTX_TM, d), lambda i: (i, 0)),
            pl.BlockSpec((OUTX_TM, GLA_WIDTH), lambda i: (i, 0)),
            pl.BlockSpec((OUTX_TM, MOBA_WIDTH), lambda i: (i, 0)),
            pl.BlockSpec((GLA_WIDTH + MOBA_WIDTH, d), const),
            pl.BlockSpec((1, d), const),
            pl.BlockSpec((d, XATTN_WIDTH), const),
            pl.BlockSpec((1, XATTN_DH), const),
            pl.BlockSpec((mem_len, XATTN_WIDTH), lambda i: (i // tiles_per_seq, 0)),
            pl.BlockSpec((mem_len, XATTN_WIDTH), lambda i: (i // tiles_per_seq, 0)),
            pl.BlockSpec((XATTN_WIDTH, d), const),
        ],
        out_specs=pl.BlockSpec((OUTX_TM, d), lambda i: (i, 0)),
        out_shape=jax.ShapeDtypeStruct((n, d), F32),
        compiler_params=_params("parallel"),
        name="outx",
    )(x1, og, om, wout_b, gx.reshape(1, d), wq_b, gq.reshape(1, -1), kx, vx, wo_b)


def kernel(x, mem, ffn1_norm, ffn1_w_gate, ffn1_w_up, ffn1_w_down, mix_norm, w_in, gla_w_gate2, gla_b_gate2, gla_out_norm, moba_q_norm, moba_k_norm, w_out, xattn_norm, mem_norm, xattn_w_q, xattn_w_kv, xattn_w_o, xattn_q_norm, xattn_k_norm, ffn2_norm, ffn2_w_gate, ffn2_w_up, ffn2_w_down):
    batch, seq, d = x.shape
    mem_len = mem.shape[1]
    depth = ffn1_norm.shape[0]
    n = batch * seq
    assert seq % GLA_TC == 0 and seq % (2 * MOBA_BLOCK) == 0 and seq % OUTX_TM == 0
    assert seq // MOBA_BLOCK + MOBA_DH <= MXU_DIM and seq // MOBA_BLOCK == 16
    assert n % FFN_TM == 0 and n % PROJ_TM == 0 and PROJ_TM % MOBA_BLOCK == 0

    xf = x.reshape(n, d)
    memf = mem.reshape(batch * mem_len, d)
    lr0 = GLA_COLS
    mq0 = lr0 + GLA_GATE_RANK
    for l in range(depth):
        w_lr = w_in[l][:, lr0:mq0]
        w_mq_t = w_in[l][:, mq0:mq0 + MOBA_WIDTH].T
        w_mk = w_in[l][:, mq0 + MOBA_WIDTH:mq0 + 2 * MOBA_WIDTH]
        w_mv_t = w_in[l][:, mq0 + 2 * MOBA_WIDTH:].T

        x1 = _ffn(xf, ffn1_norm[l], ffn1_w_gate[l], ffn1_w_up[l], ffn1_w_down[l])
        u, la, hn = _projg(x1, mix_norm[l], w_in[l], w_lr, gla_w_gate2[l], gla_b_gate2[l])
        q_t, kb, kmean, v3 = _proj_moba(hn, w_mq_t, w_mk, w_mv_t, moba_q_norm[l], moba_k_norm[l])
        o_gla = _gla(u, la, gla_out_norm[l], batch, seq)
        o_moba = _moba(q_t, kb, kmean, v3, batch, seq)
        kx, vx = _memkv(memf, mem_norm[l], xattn_w_kv[l].astype(BF16), xattn_k_norm[l])
        x3 = _outx(x1, o_gla, o_moba, w_out[l].astype(BF16), xattn_norm[l],
                   xattn_w_q[l].astype(BF16), xattn_q_norm[l], kx, vx,
                   xattn_w_o[l].astype(BF16), seq, mem_len)
        xf = _ffn(x3, ffn2_norm[l], ffn2_w_gate[l], ffn2_w_up[l], ffn2_w_down[l])
    return xf.reshape(batch, seq, d)
```

```python
import jax
import jax.numpy as jnp
from jax import lax
from jax.experimental import pallas as pl
from jax.experimental.pallas import tpu as pltpu

F32 = jnp.float32
BF16 = jnp.bfloat16

EPS = 1e-6
NEG_INF = -1e30
D_FF = 5632
GLA_HEADS = 4
GLA_DV = 256
GLA_DK = 128
GLA_QK = GLA_HEADS * GLA_DK
GLA_WIDTH = GLA_HEADS * GLA_DV
GLA_COLS = 2 * GLA_QK + 2 * GLA_WIDTH
GLA_GATE_RANK = 16
GLA_GATE_TAU = 16.0
GLA_CHUNK = 64
MOBA_DH = 128
MOBA_HEADS = 8
MOBA_WIDTH = MOBA_HEADS * MOBA_DH
MOBA_BLOCK = 256
MOBA_TOPK = 3
XATTN_HEADS = 4
XATTN_DH = 128
XATTN_WIDTH = XATTN_HEADS * XATTN_DH

V7X_VMEM_BYTES = 64 * 1024 * 1024
VMEM_LIMIT = V7X_VMEM_BYTES - 6 * 1024 * 1024
MXU_DIM = 256

FFN_TM = 1024
FFN_TF = 256
PROJ_TM = 1024
PROJ_TN = 512
GLA_TC = 512
MOBA_G = 2
MOBA_TQ = 2 * MOBA_BLOCK
OUTX_TM = 512


def _params(*semantics):
    return pltpu.CompilerParams(dimension_semantics=semantics, vmem_limit_bytes=VMEM_LIMIT)


def _rms(x, g):
    return x * lax.rsqrt(jnp.mean(x * x, axis=-1, keepdims=True) + EPS) * g


def _dot(a, b):
    return jnp.dot(a, b, preferred_element_type=F32)


def _dot_nt(a, b):
    return lax.dot_general(a, b, (((1,), (1,)), ((), ())), preferred_element_type=F32)


def _silu(x):
    return x * (1.0 / (1.0 + jnp.exp(-x)))


def _ffn_kernel(x_ref, g_ref, wg_ref, wu_ref, wd_ref, o_ref, h_ref):
    f = pl.program_id(1)

    @pl.when(f == 0)
    def _():
        h_ref[...] = _rms(x_ref[...], g_ref[...]).astype(BF16)
        o_ref[...] = jnp.zeros_like(o_ref)

    h = h_ref[...]
    gate = _dot(h, wg_ref[...].astype(BF16))
    up = _dot(h, wu_ref[...].astype(BF16))
    act = (_silu(gate) * up).astype(BF16)
    o_ref[...] += _dot(act, wd_ref[...].astype(BF16))

    @pl.when(f == pl.num_programs(1) - 1)
    def _():
        o_ref[...] = x_ref[...] + 0.5 * o_ref[...]


def _ffn(x, g, wg, wu, wd):
    n, d = x.shape
    grid = (n // FFN_TM, D_FF // FFN_TF)
    return pl.pallas_call(
        _ffn_kernel,
        grid=grid,
        in_specs=[
            pl.BlockSpec((FFN_TM, d), lambda i, f: (i, 0)),
            pl.BlockSpec((1, d), lambda i, f: (0, 0)),
            pl.BlockSpec((d, FFN_TF), lambda i, f: (0, f)),
            pl.BlockSpec((d, FFN_TF), lambda i, f: (0, f)),
            pl.BlockSpec((FFN_TF, d), lambda i, f: (f, 0)),
        ],
        out_specs=pl.BlockSpec((FFN_TM, d), lambda i, f: (i, 0)),
        out_shape=jax.ShapeDtypeStruct((n, d), F32),
        scratch_shapes=[pltpu.VMEM((FFN_TM, d), BF16)],
        compiler_params=_params("parallel", "arbitrary"),
        name="ffn",
    )(x, g.reshape(1, d), wg, wu, wd)


def _projg_kernel(x_ref, g_ref, w_ref, wlr_ref, wg2_ref, bg2_ref, u_ref, la_ref, hn_ref):
    @pl.when(pl.program_id(1) == 0)
    def _():
        h = _rms(x_ref[...], g_ref[...]).astype(BF16)
        hn_ref[...] = h
        lr = _dot(h, wlr_ref[...].astype(BF16))
        pre = _dot(lr.astype(BF16), wg2_ref[...].astype(BF16)) + bg2_ref[...]
        log_sig = -(jnp.maximum(-pre, 0.0) + jnp.log(1.0 + jnp.exp(-jnp.abs(pre))))
        la_ref[...] = log_sig * (1.0 / GLA_GATE_TAU)

    u_ref[...] = _dot(hn_ref[...], w_ref[...].astype(BF16))


def _projg(x1, g, w_in, layer, w_lr, wg2, bg2):
    n, d = x1.shape
    const = lambda i, j: (0, 0)
    return pl.pallas_call(
        _projg_kernel,
        grid=(n // PROJ_TM, GLA_COLS // PROJ_TN),
        in_specs=[
            pl.BlockSpec((PROJ_TM, d), lambda i, j: (i, 0)),
            pl.BlockSpec((1, d), const),
            pl.BlockSpec((None, d, PROJ_TN), lambda i, j: (layer, 0, j)),
            pl.BlockSpec((d, GLA_GATE_RANK), const),
            pl.BlockSpec((GLA_GATE_RANK, GLA_QK), const),
            pl.BlockSpec((1, GLA_QK), const),
        ],
        out_specs=[
            pl.BlockSpec((PROJ_TM, PROJ_TN), lambda i, j: (i, j)),
            pl.BlockSpec((PROJ_TM, GLA_QK), lambda i, j: (i, 0)),
            pl.BlockSpec((PROJ_TM, d), lambda i, j: (i, 0)),
        ],
        out_shape=[jax.ShapeDtypeStruct((n, GLA_COLS), F32),
                   jax.ShapeDtypeStruct((n, GLA_QK), F32),
                   jax.ShapeDtypeStruct((n, d), BF16)],
        compiler_params=_params("parallel", "arbitrary"),
        name="projg",
    )(x1, g.reshape(1, d), w_in, w_lr, wg2, bg2.reshape(1, -1))


def _projq_kernel(hn_ref, wt_ref, g_ref, qt_ref):
    yt = _dot_nt(wt_ref[...].astype(BF16), hn_ref[...])
    g = g_ref[...]
    parts = []
    for k in range(PROJ_TN // MOBA_DH):
        p = yt[k * MOBA_DH:(k + 1) * MOBA_DH, :]
        inv = lax.rsqrt(jnp.mean(p * p, axis=0, keepdims=True) + EPS)
        parts.append(p * inv * g)
    qt_ref[...] = jnp.concatenate(parts, axis=0)


def _projk_kernel(hn_ref, w_ref, g_ref, kb_ref, kmean_ref):
    y = _dot(hn_ref[...], w_ref[...].astype(BF16))
    g = g_ref[...]
    parts = [_rms(y[:, k * MOBA_DH:(k + 1) * MOBA_DH], g) for k in range(PROJ_TN // MOBA_DH)]
    kn = jnp.concatenate(parts, axis=1)
    kb_ref[...] = kn.astype(BF16)
    blocks = PROJ_TM // MOBA_BLOCK
    kmean_ref[0] = jnp.sum(kn.reshape(blocks, MOBA_BLOCK, PROJ_TN), axis=1) * (1.0 / MOBA_BLOCK)


def _projv_kernel(hn_ref, wt_ref, v3_ref):
    yt = _dot_nt(wt_ref[...].astype(BF16), hn_ref[...]).astype(BF16)
    for t in range(PROJ_TM // MOBA_BLOCK):
        v3_ref[t] = yt[:, t * MOBA_BLOCK:(t + 1) * MOBA_BLOCK]


def _proj_moba(hn, w_q_t, w_k, w_v_t, gq, gk):
    n, d = hn.shape
    grid = (n // PROJ_TM, MOBA_WIDTH // PROJ_TN)
    hn_spec = pl.BlockSpec((PROJ_TM, d), lambda i, j: (i, 0))
    wt_spec = pl.BlockSpec((PROJ_TN, d), lambda i, j: (j, 0))
    params = _params("parallel", "arbitrary")
    blocks = PROJ_TM // MOBA_BLOCK

    q_t = pl.pallas_call(
        _projq_kernel, grid=grid,
        in_specs=[hn_spec, wt_spec, pl.BlockSpec((MOBA_DH, 1), lambda i, j: (0, 0))],
        out_specs=pl.BlockSpec((PROJ_TN, PROJ_TM), lambda i, j: (j, i)),
        out_shape=jax.ShapeDtypeStruct((MOBA_WIDTH, n), F32),
        compiler_params=params, name="projq",
    )(hn, w_q_t, gq.reshape(-1, 1))

    kb, kmean = pl.pallas_call(
        _projk_kernel, grid=grid,
        in_specs=[hn_spec, pl.BlockSpec((d, PROJ_TN), lambda i, j: (0, j)),
                  pl.BlockSpec((1, MOBA_DH), lambda i, j: (0, 0))],
        out_specs=[pl.BlockSpec((PROJ_TM, PROJ_TN), lambda i, j: (i, j)),
                   pl.BlockSpec((1, blocks, PROJ_TN), lambda i, j: (i, 0, j))],
        out_shape=[jax.ShapeDtypeStruct((n, MOBA_WIDTH), BF16),
                   jax.ShapeDtypeStruct((n // PROJ_TM, blocks, MOBA_WIDTH), F32)],
        compiler_params=params, name="projk",
    )(hn, w_k, gk.reshape(1, -1))

    v3 = pl.pallas_call(
        _projv_kernel, grid=grid,
        in_specs=[hn_spec, wt_spec],
        out_specs=pl.BlockSpec((blocks, PROJ_TN, MOBA_BLOCK), lambda i, j: (i, j, 0)),
        out_shape=jax.ShapeDtypeStruct((n // MOBA_BLOCK, MOBA_WIDTH, MOBA_BLOCK), BF16),
        compiler_params=params, name="projv",
    )(hn, w_v_t)
    return q_t, kb, kmean.reshape(n // MOBA_BLOCK, MOBA_WIDTH), v3


def _dot_exact_rhs(m, a):
    a1 = a.astype(BF16)
    r1 = a - a1.astype(F32)
    a2 = r1.astype(BF16)
    a3 = (r1 - a2.astype(F32)).astype(BF16)
    return _dot(m, a1) + _dot(m, a2) + _dot(m, a3)


def _gla_kernel(q_ref, k_ref, v_ref, r_ref, la_ref, go_ref, o_ref, s_ref):
    t = pl.program_id(2)
    tc = GLA_TC
    nchunk = tc // GLA_CHUNK

    @pl.when(t == 0)
    def _():
        s_ref[...] = jnp.zeros_like(s_ref)

    row = lax.broadcasted_iota(jnp.int32, (tc, tc), 0)
    col = lax.broadcasted_iota(jnp.int32, (tc, tc), 1)
    causal = jnp.logical_and((row // GLA_CHUNK) == (col // GLA_CHUNK), col <= row)

    la = la_ref[...]
    crow = lax.broadcasted_iota(jnp.int32, (GLA_CHUNK, GLA_CHUNK), 0)
    ccol = lax.broadcasted_iota(jnp.int32, (GLA_CHUNK, GLA_CHUNK), 1)
    tri = (ccol <= crow).astype(BF16)
    la_wide = jnp.concatenate([la[c * GLA_CHUNK:(c + 1) * GLA_CHUNK] for c in range(nchunk)], axis=1)
    bc_wide = _dot_exact_rhs(tri, la_wide)
    bcum = jnp.concatenate([bc_wide[:, c * GLA_DK:(c + 1) * GLA_DK] for c in range(nchunk)], axis=0)
    btot = jnp.concatenate(
        [jnp.broadcast_to(bc_wide[GLA_CHUNK - 1:GLA_CHUNK, c * GLA_DK:(c + 1) * GLA_DK],
                          (GLA_CHUNK, GLA_DK)) for c in range(nchunk)], axis=0)

    k = k_ref[...]
    q_dec = (q_ref[...] * (GLA_DK ** -0.5)) * jnp.exp(bcum)
    k_inv = k * jnp.exp(-bcum)
    k_tail_t = (k * jnp.exp(btot - bcum)).T
    decay_t = jnp.exp(btot).T
    v = v_ref[...].astype(BF16)

    q_dec_b = q_dec.astype(BF16)
    att = jnp.where(causal, _dot_nt(q_dec_b, k_inv.astype(BF16)), 0.0)
    o = _dot(att.astype(BF16), v)

    lane_chunk = lax.broadcasted_iota(jnp.int32, (GLA_DK, tc), 1) // GLA_CHUNK
    state = s_ref[...]
    inter = []
    for c in range(nchunk):
        rows = slice(c * GLA_CHUNK, (c + 1) * GLA_CHUNK)
        inter.append(_dot(q_dec_b[rows], state.astype(BF16)))
        kt_c = jnp.where(lane_chunk == c, k_tail_t, 0.0).astype(BF16)
        dec_c = decay_t[:, c * GLA_CHUNK:c * GLA_CHUNK + 1]
        state = dec_c * state + _dot(kt_c, v)
    s_ref[...] = state
    o = o + jnp.concatenate(inter, axis=0)

    o = _rms(o, go_ref[...]) * _silu(r_ref[...])
    o_ref[...] = o


def _gla(u, la, g_out, batch, seq):
    n = u.shape[0]
    nt = seq // GLA_TC
    tok = lambda b, h, t: b * nt + t
    k_blk0 = GLA_QK // GLA_DK
    v_blk0 = 2 * GLA_QK // GLA_DV
    r_blk0 = v_blk0 + GLA_WIDTH // GLA_DV
    return pl.pallas_call(
        _gla_kernel,
        grid=(batch, GLA_HEADS, nt),
        in_specs=[
            pl.BlockSpec((GLA_TC, GLA_DK), lambda b, h, t: (tok(b, h, t), h)),
            pl.BlockSpec((GLA_TC, GLA_DK), lambda b, h, t: (tok(b, h, t), k_blk0 + h)),
            pl.BlockSpec((GLA_TC, GLA_DV), lambda b, h, t: (tok(b, h, t), v_blk0 + h)),
            pl.BlockSpec((GLA_TC, GLA_DV), lambda b, h, t: (tok(b, h, t), r_blk0 + h)),
            pl.BlockSpec((GLA_TC, GLA_DK), lambda b, h, t: (tok(b, h, t), h)),
            pl.BlockSpec((1, GLA_DV), lambda b, h, t: (0, 0)),
        ],
        out_specs=pl.BlockSpec((GLA_TC, GLA_DV), lambda b, h, t: (tok(b, h, t), h)),
        out_shape=jax.ShapeDtypeStruct((n, GLA_WIDTH), F32),
        scratch_shapes=[pltpu.VMEM((GLA_DK, GLA_DV), F32)],
        compiler_params=_params("parallel", "parallel", "arbitrary"),
        name="gla",
    )(u, u, u, u, la, g_out.reshape(1, -1))


def _moba_kernel(qt_ref, kb_ref, v3_ref, kmean_ref, o_ref, kaug_ref):
    ti = pl.program_id(2)
    bs = MOBA_BLOCK
    tq = MOBA_TQ
    nb = kb_ref.shape[0] // bs
    dh = MOBA_DH
    scale = dh ** -0.5

    @pl.when(ti == 0)
    def _():
        seq = kb_ref.shape[0]
        row_blk = lax.broadcasted_iota(jnp.int32, (seq, MXU_DIM - dh), 0) // bs
        lane = lax.broadcasted_iota(jnp.int32, (seq, MXU_DIM - dh), 1)
        onehot = (row_blk == lane).astype(BF16)
        for g in range(MOBA_G):
            kaug_ref[g, :, :dh] = kb_ref[:, g * dh:(g + 1) * dh]
            kaug_ref[g, :, dh:] = onehot

    blk = lax.broadcasted_iota(jnp.int32, (nb, tq), 0)
    q_blk = 2 * ti + (lax.broadcasted_iota(jnp.int32, (nb, tq), 1) // bs)
    past = blk < q_blk
    key_row = lax.broadcasted_iota(jnp.int32, (tq, tq), 0)
    qry_col = lax.broadcasted_iota(jnp.int32, (tq, tq), 1)
    causal = key_row <= qry_col
    same_blk = (key_row // bs) == (qry_col // bs)
    ones_rows = jnp.ones((16, tq), BF16)
    diag = pl.multiple_of(ti * tq, tq)

    def pv_dot(g, t, p):
        return (_dot(v3_ref[2 * t, g * dh:(g + 1) * dh, :], p[:bs])
                + _dot(v3_ref[2 * t + 1, g * dh:(g + 1) * dh, :], p[bs:]))

    q_aug, carry0 = [], []
    for g in range(MOBA_G):
        qt = qt_ref[g * dh:(g + 1) * dh, :]
        gate = jnp.dot(kmean_ref[:, g * dh:(g + 1) * dh], qt,
                       precision=lax.Precision.HIGHEST, preferred_element_type=F32)
        gate = jnp.where(past, gate, NEG_INF)
        rank = jnp.zeros((nb, tq), jnp.int32)
        for j in range(nb):
            gj = gate[j:j + 1, :]
            ahead = jnp.logical_or(gj > gate, jnp.logical_and(gj == gate, blk > j))
            rank = rank + ahead.astype(jnp.int32)
        chosen = jnp.logical_and(past, rank < MOBA_TOPK)
        bias = jnp.where(chosen, 0.0, NEG_INF)
        qa = jnp.concatenate([(qt * scale).astype(BF16), bias.astype(BF16),
                              jnp.zeros((MXU_DIM - dh - nb, tq), BF16)], axis=0)
        q_aug.append(qa)

        first_chosen = jnp.sum(jnp.where(blk == 2 * ti, chosen.astype(F32), 0.0),
                               axis=0, keepdims=True) > 0.5
        allowed = jnp.logical_and(causal, jnp.logical_or(same_blk, first_chosen))
        s = _dot(kaug_ref[g, pl.ds(diag, tq), :dh], qa[:dh])
        s = jnp.where(allowed, s, NEG_INF)
        m = jnp.max(s, axis=0, keepdims=True)
        p = jnp.exp(s - m).astype(BF16)
        l = _dot(ones_rows, p)[0:1]
        carry0.append((m, l, pv_dot(g, ti, p)))

    def body(t, carry):
        out = []
        for g in range(MOBA_G):
            m, l, acc = carry[g]
            off = pl.multiple_of(t * tq, tq)
            s = _dot(kaug_ref[g, pl.ds(off, tq), :], q_aug[g])
            m_new = jnp.maximum(m, jnp.max(s, axis=0, keepdims=True))
            alpha = jnp.exp(m - m_new)
            p = jnp.exp(s - m_new).astype(BF16)
            l = alpha * l + _dot(ones_rows, p)[0:1]
            out.append((m_new, l, alpha * acc + pv_dot(g, t, p)))
        return tuple(out)

    carry = lax.fori_loop(0, ti, body, tuple(carry0))
    for g in range(MOBA_G):
        m, l, acc = carry[g]
        o_ref[:, g * dh:(g + 1) * dh] = (acc * (1.0 / l)).T


def _moba(q_t, kb, kmean, v3, batch, seq):
    n = kb.shape[0]
    nb = seq // MOBA_BLOCK
    nt = seq // MOBA_TQ
    gw = MOBA_G * MOBA_DH
    return pl.pallas_call(
        _moba_kernel,
        grid=(batch, MOBA_HEADS // MOBA_G, nt),
        in_specs=[
            pl.BlockSpec((gw, MOBA_TQ), lambda b, h, i: (h, b * nt + i)),
            pl.BlockSpec((seq, gw), lambda b, h, i: (b, h)),
            pl.BlockSpec((nb, gw, MOBA_BLOCK), lambda b, h, i: (b, h, 0)),
            pl.BlockSpec((nb, gw), lambda b, h, i: (b, h)),
        ],
        out_specs=pl.BlockSpec((MOBA_TQ, gw), lambda b, h, i: (b * nt + i, h)),
        out_shape=jax.ShapeDtypeStruct((n, MOBA_WIDTH), F32),
        scratch_shapes=[pltpu.VMEM((MOBA_G, seq, MXU_DIM), BF16)],
        compiler_params=_params("parallel", "parallel", "arbitrary"),
        name="moba",
    )(q_t, kb, v3, kmean)


def _memkv_kernel(mem_ref, g_ref, wkv_ref, gk_ref, k_ref, v_ref):
    h = _rms(mem_ref[...], g_ref[...]).astype(BF16)
    kv = _dot(h, wkv_ref[...])
    gk = gk_ref[...]
    ks = [_rms(kv[:, i * XATTN_DH:(i + 1) * XATTN_DH], gk) for i in range(XATTN_HEADS)]
    k_ref[...] = jnp.concatenate(ks, axis=1)
    v_ref[...] = kv[:, XATTN_WIDTH:]


def _memkv(mem, g, wkv_b, gk):
    m, d = mem.shape
    return pl.pallas_call(
        _memkv_kernel,
        out_shape=[jax.ShapeDtypeStruct((m, XATTN_WIDTH), F32),
                   jax.ShapeDtypeStruct((m, XATTN_WIDTH), F32)],
        compiler_params=pltpu.CompilerParams(vmem_limit_bytes=VMEM_LIMIT),
        name="memkv",
    )(mem, g.reshape(1, d), wkv_b, gk.reshape(1, -1))


def _outx_kernel(x1_ref, og_ref, om_ref, wout_ref, gx_ref, wq_ref, gq_ref, kx_ref, vx_ref, wo_ref,
                 o_ref):
    x2 = (x1_ref[...]
          + _dot(og_ref[...].astype(BF16), wout_ref[:GLA_WIDTH, :])
          + _dot(om_ref[...].astype(BF16), wout_ref[GLA_WIDTH:, :]))
    h = _rms(x2, gx_ref[...]).astype(BF16)
    q = _dot(h, wq_ref[...])
    gq = gq_ref[...]
    scale = XATTN_DH ** -0.5
    heads = []
    for i in range(XATTN_HEADS):
        cols = slice(i * XATTN_DH, (i + 1) * XATTN_DH)
        qh = _rms(q[:, cols], gq).astype(BF16)
        s = _dot_nt(qh, kx_ref[:, cols].astype(BF16)) * scale
        p = jnp.exp(s - jnp.max(s, axis=-1, keepdims=True))
        p = p * (1.0 / jnp.sum(p, axis=-1, keepdims=True))
        heads.append(_dot(p.astype(BF16), vx_ref[:, cols].astype(BF16)))
    o = jnp.concatenate(heads, axis=1).astype(BF16)
    o_ref[...] = x2 + _dot(o, wo_ref[...])


def _outx(x1, og, om, wout_b, gx, wq_b, gq, kx, vx, wo_b, seq, mem_len):
    n, d = x1.shape
    tiles_per_seq = seq // OUTX_TM
    const = lambda i: (0, 0)
    return pl.pallas_call(
        _outx_kernel,
        grid=(n // OUTX_TM,),
        in_specs=[
            pl.BlockSpec((OUTX_TM, d), lambda i: (i, 0)),
            pl.BlockSpec((OUTX_TM, GLA_WIDTH), lambda i: (i, 0)),
            pl.BlockSpec((OUTX_TM, MOBA_WIDTH), lambda i: (i, 0)),
            pl.BlockSpec((GLA_WIDTH + MOBA_WIDTH, d), const),
            pl.BlockSpec((1, d), const),
            pl.BlockSpec((d, XATTN_WIDTH), const),
            pl.BlockSpec((1, XATTN_DH), const),
            pl.BlockSpec((mem_len, XATTN_WIDTH), lambda i: (i // tiles_per_seq, 0)),
            pl.BlockSpec((mem_len, XATTN_WIDTH), lambda i: (i // tiles_per_seq, 0)),
            pl.BlockSpec((XATTN_WIDTH, d), const),
        ],
        out_specs=pl.BlockSpec((OUTX_TM, d), lambda i: (i, 0)),
        out_shape=jax.ShapeDtypeStruct((n, d), F32),
        compiler_params=_params("parallel"),
        name="outx",
    )(x1, og, om, wout_b, gx.reshape(1, d), wq_b, gq.reshape(1, -1), kx, vx, wo_b)


def kernel(x, mem, ffn1_norm, ffn1_w_gate, ffn1_w_up, ffn1_w_down, mix_norm, w_in, gla_w_gate2, gla_b_gate2, gla_out_norm, moba_q_norm, moba_k_norm, w_out, xattn_norm, mem_norm, xattn_w_q, xattn_w_kv, xattn_w_o, xattn_q_norm, xattn_k_norm, ffn2_norm, ffn2_w_gate, ffn2_w_up, ffn2_w_down):
    batch, seq, d = x.shape
    mem_len = mem.shape[1]
    depth = ffn1_norm.shape[0]
    n = batch * seq
    assert seq % GLA_TC == 0 and seq % (2 * MOBA_BLOCK) == 0 and seq % OUTX_TM == 0
    assert seq // MOBA_BLOCK + MOBA_DH <= MXU_DIM and seq // MOBA_BLOCK == 16
    assert n % FFN_TM == 0 and n % PROJ_TM == 0 and PROJ_TM % MOBA_BLOCK == 0

    xf = x.reshape(n, d)
    memf = mem.reshape(batch * mem_len, d)
    lr0 = GLA_COLS
    mq0 = lr0 + GLA_GATE_RANK
    for l in range(depth):
        w_lr = w_in[l][:, lr0:mq0]
        w_mq_t = w_in[l][:, mq0:mq0 + MOBA_WIDTH].T
        w_mk = w_in[l][:, mq0 + MOBA_WIDTH:mq0 + 2 * MOBA_WIDTH]
        w_mv_t = w_in[l][:, mq0 + 2 * MOBA_WIDTH:].T

        x1 = _ffn(xf, ffn1_norm[l], ffn1_w_gate[l], ffn1_w_up[l], ffn1_w_down[l])
        u, la, hn = _projg(x1, mix_norm[l], w_in, l, w_lr, gla_w_gate2[l], gla_b_gate2[l])
        q_t, kb, kmean, v3 = _proj_moba(hn, w_mq_t, w_mk, w_mv_t, moba_q_norm[l], moba_k_norm[l])
        o_gla = _gla(u, la, gla_out_norm[l], batch, seq)
        o_moba = _moba(q_t, kb, kmean, v3, batch, seq)
        kx, vx = _memkv(memf, mem_norm[l], xattn_w_kv[l].astype(BF16), xattn_k_norm[l])
        x3 = _outx(x1, o_gla, o_moba, w_out[l].astype(BF16), xattn_norm[l],
                   xattn_w_q[l].astype(BF16), xattn_q_norm[l], kx, vx,
                   xattn_w_o[l].astype(BF16), seq, mem_len)
        xf = _ffn(x3, ffn2_norm[l], ffn2_w_gate[l], ffn2_w_up[l], ffn2_w_down[l])
    return xf.reshape(batch, seq, d)
```

```python
import jax
import jax.numpy as jnp
from jax import lax
from jax.experimental import pallas as pl
from jax.experimental.pallas import tpu as pltpu

F32 = jnp.float32
BF16 = jnp.bfloat16

EPS = 1e-6
NEG_INF = -1e30
D_FF = 5632
GLA_HEADS = 4
GLA_DV = 256
GLA_DK = 128
GLA_QK = GLA_HEADS * GLA_DK
GLA_WIDTH = GLA_HEADS * GLA_DV
GLA_COLS = 2 * GLA_QK + 2 * GLA_WIDTH
GLA_GATE_RANK = 16
GLA_GATE_TAU = 16.0
GLA_CHUNK = 64
MOBA_DH = 128
MOBA_HEADS = 8
MOBA_WIDTH = MOBA_HEADS * MOBA_DH
MOBA_BLOCK = 256
MOBA_TOPK = 3
XATTN_HEADS = 4
XATTN_DH = 128
XATTN_WIDTH = XATTN_HEADS * XATTN_DH

V7X_VMEM_BYTES = 64 * 1024 * 1024
VMEM_LIMIT = V7X_VMEM_BYTES - 6 * 1024 * 1024
MXU_DIM = 256

FFN_TM = 1024
FFN_TF = 256
PROJ_TM = 1024
PROJ_TN = 512
GLA_TC = 512
MOBA_G = 2
MOBA_TQ = 2 * MOBA_BLOCK
MOBA_VPAD = 16
MOBA_VROWS = MOBA_DH + MOBA_VPAD
OUTX_TM = 512


def _params(*semantics):
    return pltpu.CompilerParams(dimension_semantics=semantics, vmem_limit_bytes=VMEM_LIMIT)


def _rms(x, g):
    return x * lax.rsqrt(jnp.mean(x * x, axis=-1, keepdims=True) + EPS) * g


def _dot(a, b):
    return jnp.dot(a, b, preferred_element_type=F32)


def _dot_nt(a, b):
    return lax.dot_general(a, b, (((1,), (1,)), ((), ())), preferred_element_type=F32)


def _silu(x):
    return x * (1.0 / (1.0 + jnp.exp(-x)))


def _ffn_kernel(x_ref, g_ref, wg_ref, wu_ref, wd_ref, o_ref, h_ref):
    f = pl.program_id(1)

    @pl.when(f == 0)
    def _():
        h_ref[...] = _rms(x_ref[...], g_ref[...]).astype(BF16)
        o_ref[...] = jnp.zeros_like(o_ref)

    h = h_ref[...]
    gate = _dot(h, wg_ref[...].astype(BF16))
    up = _dot(h, wu_ref[...].astype(BF16))
    act = (_silu(gate) * up).astype(BF16)
    o_ref[...] += _dot(act, wd_ref[...].astype(BF16))

    @pl.when(f == pl.num_programs(1) - 1)
    def _():
        o_ref[...] = x_ref[...] + 0.5 * o_ref[...]


def _ffn(x, g, wg, wu, wd):
    n, d = x.shape
    grid = (n // FFN_TM, D_FF // FFN_TF)
    return pl.pallas_call(
        _ffn_kernel,
        grid=grid,
        in_specs=[
            pl.BlockSpec((FFN_TM, d), lambda i, f: (i, 0)),
            pl.BlockSpec((1, d), lambda i, f: (0, 0)),
            pl.BlockSpec((d, FFN_TF), lambda i, f: (0, f)),
            pl.BlockSpec((d, FFN_TF), lambda i, f: (0, f)),
            pl.BlockSpec((FFN_TF, d), lambda i, f: (f, 0)),
        ],
        out_specs=pl.BlockSpec((FFN_TM, d), lambda i, f: (i, 0)),
        out_shape=jax.ShapeDtypeStruct((n, d), F32),
        scratch_shapes=[pltpu.VMEM((FFN_TM, d), BF16)],
        compiler_params=_params("parallel", "arbitrary"),
        name="ffn",
    )(x, g.reshape(1, d), wg, wu, wd)


def _projg_kernel(x_ref, g_ref, wt_ref, wlr_ref, wg2_ref, bg2_ref, u_ref, la_ref, hn_ref):
    @pl.when(pl.program_id(1) == 0)
    def _():
        h = _rms(x_ref[...], g_ref[...]).astype(BF16)
        hn_ref[...] = h
        lr = _dot_nt(h, wlr_ref[...].astype(BF16))
        pre = _dot(lr.astype(BF16), wg2_ref[...].astype(BF16)) + bg2_ref[...]
        log_sig = -(jnp.maximum(-pre, 0.0) + jnp.log(1.0 + jnp.exp(-jnp.abs(pre))))
        la_ref[...] = log_sig * (1.0 / GLA_GATE_TAU)

    u_ref[...] = _dot_nt(hn_ref[...], wt_ref[...].astype(BF16))


def _projg(x1, g, w_in_t, layer, w_lr_t, wg2, bg2):
    n, d = x1.shape
    const = lambda i, j: (0, 0)
    return pl.pallas_call(
        _projg_kernel,
        grid=(n // PROJ_TM, GLA_COLS // PROJ_TN),
        in_specs=[
            pl.BlockSpec((PROJ_TM, d), lambda i, j: (i, 0)),
            pl.BlockSpec((1, d), const),
            pl.BlockSpec((None, PROJ_TN, d), lambda i, j: (layer, j, 0)),
            pl.BlockSpec((GLA_GATE_RANK, d), const),
            pl.BlockSpec((GLA_GATE_RANK, GLA_QK), const),
            pl.BlockSpec((1, GLA_QK), const),
        ],
        out_specs=[
            pl.BlockSpec((PROJ_TM, PROJ_TN), lambda i, j: (i, j)),
            pl.BlockSpec((PROJ_TM, GLA_QK), lambda i, j: (i, 0)),
            pl.BlockSpec((PROJ_TM, d), lambda i, j: (i, 0)),
        ],
        out_shape=[jax.ShapeDtypeStruct((n, GLA_COLS), F32),
                   jax.ShapeDtypeStruct((n, GLA_QK), F32),
                   jax.ShapeDtypeStruct((n, d), BF16)],
        compiler_params=_params("parallel", "arbitrary"),
        name="projg",
    )(x1, g.reshape(1, d), w_in_t, w_lr_t, wg2, bg2.reshape(1, -1))


def _projq_kernel(hn_ref, wt_ref, g_ref, qt_ref):
    yt = _dot_nt(wt_ref[...].astype(BF16), hn_ref[...])
    g = g_ref[...]
    parts = []
    for k in range(PROJ_TN // MOBA_DH):
        p = yt[k * MOBA_DH:(k + 1) * MOBA_DH, :]
        inv = lax.rsqrt(jnp.mean(p * p, axis=0, keepdims=True) + EPS)
        parts.append(p * inv * g)
    qt_ref[...] = jnp.concatenate(parts, axis=0)


def _projk_kernel(hn_ref, wt_ref, g_ref, kb_ref, kmean_ref):
    y = _dot_nt(hn_ref[...], wt_ref[...].astype(BF16))
    g = g_ref[...]
    parts = [_rms(y[:, k * MOBA_DH:(k + 1) * MOBA_DH], g) for k in range(PROJ_TN // MOBA_DH)]
    kn = jnp.concatenate(parts, axis=1)
    kb_ref[...] = kn.astype(BF16)
    blocks = PROJ_TM // MOBA_BLOCK
    kmean_ref[0] = jnp.sum(kn.reshape(blocks, MOBA_BLOCK, PROJ_TN), axis=1) * (1.0 / MOBA_BLOCK)


def _projv_kernel(hn_ref, wt_ref, v3_ref):
    yt = _dot_nt(wt_ref[...].astype(BF16), hn_ref[...]).astype(BF16)
    ones = jnp.ones((MOBA_VPAD, PROJ_TM), BF16)
    parts = []
    for k in range(PROJ_TN // MOBA_DH):
        parts += [yt[k * MOBA_DH:(k + 1) * MOBA_DH], ones]
    ya = jnp.concatenate(parts, axis=0)
    for t in range(PROJ_TM // MOBA_BLOCK):
        v3_ref[t] = ya[:, t * MOBA_BLOCK:(t + 1) * MOBA_BLOCK]


def _proj_moba(hn, w_q_t, w_k_t, w_v_t, gq, gk):
    n, d = hn.shape
    grid = (n // PROJ_TM, MOBA_WIDTH // PROJ_TN)
    hn_spec = pl.BlockSpec((PROJ_TM, d), lambda i, j: (i, 0))
    wt_spec = pl.BlockSpec((PROJ_TN, d), lambda i, j: (j, 0))
    params = _params("parallel", "arbitrary")
    blocks = PROJ_TM // MOBA_BLOCK

    q_t = pl.pallas_call(
        _projq_kernel, grid=grid,
        in_specs=[hn_spec, wt_spec, pl.BlockSpec((MOBA_DH, 1), lambda i, j: (0, 0))],
        out_specs=pl.BlockSpec((PROJ_TN, PROJ_TM), lambda i, j: (j, i)),
        out_shape=jax.ShapeDtypeStruct((MOBA_WIDTH, n), F32),
        compiler_params=params, name="projq",
    )(hn, w_q_t, gq.reshape(-1, 1))

    kb, kmean = pl.pallas_call(
        _projk_kernel, grid=grid,
        in_specs=[hn_spec, wt_spec, pl.BlockSpec((1, MOBA_DH), lambda i, j: (0, 0))],
        out_specs=[pl.BlockSpec((PROJ_TM, PROJ_TN), lambda i, j: (i, j)),
                   pl.BlockSpec((1, blocks, PROJ_TN), lambda i, j: (i, 0, j))],
        out_shape=[jax.ShapeDtypeStruct((n, MOBA_WIDTH), BF16),
                   jax.ShapeDtypeStruct((n // PROJ_TM, blocks, MOBA_WIDTH), F32)],
        compiler_params=params, name="projk",
    )(hn, w_k_t, gk.reshape(1, -1))

    v3 = pl.pallas_call(
        _projv_kernel, grid=grid,
        in_specs=[hn_spec, wt_spec],
        out_specs=pl.BlockSpec((blocks, PROJ_TN // MOBA_DH * MOBA_VROWS, MOBA_BLOCK),
                               lambda i, j: (i, j, 0)),
        out_shape=jax.ShapeDtypeStruct((n // MOBA_BLOCK, MOBA_HEADS * MOBA_VROWS, MOBA_BLOCK), BF16),
        compiler_params=params, name="projv",
    )(hn, w_v_t)
    return q_t, kb, kmean.reshape(n // MOBA_BLOCK, MOBA_WIDTH), v3


def _dot_exact_rhs(m, a):
    a1 = a.astype(BF16)
    r1 = a - a1.astype(F32)
    a2 = r1.astype(BF16)
    a3 = (r1 - a2.astype(F32)).astype(BF16)
    return _dot(m, a1) + _dot(m, a2) + _dot(m, a3)


def _gla_kernel(q_ref, k_ref, v_ref, r_ref, la_ref, go_ref, o_ref, s_ref):
    t = pl.program_id(2)
    tc = GLA_TC
    nchunk = tc // GLA_CHUNK

    @pl.when(t == 0)
    def _():
        s_ref[...] = jnp.zeros_like(s_ref)

    row = lax.broadcasted_iota(jnp.int32, (tc, tc), 0)
    col = lax.broadcasted_iota(jnp.int32, (tc, tc), 1)
    causal = jnp.logical_and((row // GLA_CHUNK) == (col // GLA_CHUNK), col <= row)

    la = la_ref[...]
    crow = lax.broadcasted_iota(jnp.int32, (GLA_CHUNK, GLA_CHUNK), 0)
    ccol = lax.broadcasted_iota(jnp.int32, (GLA_CHUNK, GLA_CHUNK), 1)
    tri = (ccol <= crow).astype(BF16)
    la_wide = jnp.concatenate([la[c * GLA_CHUNK:(c + 1) * GLA_CHUNK] for c in range(nchunk)], axis=1)
    bc_wide = _dot_exact_rhs(tri, la_wide)
    bcum = jnp.concatenate([bc_wide[:, c * GLA_DK:(c + 1) * GLA_DK] for c in range(nchunk)], axis=0)
    btot = jnp.concatenate(
        [jnp.broadcast_to(bc_wide[GLA_CHUNK - 1:GLA_CHUNK, c * GLA_DK:(c + 1) * GLA_DK],
                          (GLA_CHUNK, GLA_DK)) for c in range(nchunk)], axis=0)

    k = k_ref[...]
    q_dec = (q_ref[...] * (GLA_DK ** -0.5)) * jnp.exp(bcum)
    k_inv = k * jnp.exp(-bcum)
    k_tail_t = (k * jnp.exp(btot - bcum)).T
    decay_t = jnp.exp(btot).T
    v = v_ref[...].astype(BF16)

    q_dec_b = q_dec.astype(BF16)
    att = jnp.where(causal, _dot_nt(q_dec_b, k_inv.astype(BF16)), 0.0)
    o = _dot(att.astype(BF16), v)

    lane_chunk = lax.broadcasted_iota(jnp.int32, (GLA_DK, tc), 1) // GLA_CHUNK
    state = s_ref[...]
    inter = []
    for c in range(nchunk):
        rows = slice(c * GLA_CHUNK, (c + 1) * GLA_CHUNK)
        inter.append(_dot(q_dec_b[rows], state.astype(BF16)))
        kt_c = jnp.where(lane_chunk == c, k_tail_t, 0.0).astype(BF16)
        dec_c = decay_t[:, c * GLA_CHUNK:c * GLA_CHUNK + 1]
        state = dec_c * state + _dot(kt_c, v)
    s_ref[...] = state
    o = o + jnp.concatenate(inter, axis=0)

    o = _rms(o, go_ref[...]) * _silu(r_ref[...])
    o_ref[...] = o


def _gla(u, la, g_out, batch, seq):
    n = u.shape[0]
    nt = seq // GLA_TC
    tok = lambda b, h, t: b * nt + t
    k_blk0 = GLA_QK // GLA_DK
    v_blk0 = 2 * GLA_QK // GLA_DV
    r_blk0 = v_blk0 + GLA_WIDTH // GLA_DV
    return pl.pallas_call(
        _gla_kernel,
        grid=(batch, GLA_HEADS, nt),
        in_specs=[
            pl.BlockSpec((GLA_TC, GLA_DK), lambda b, h, t: (tok(b, h, t), h)),
            pl.BlockSpec((GLA_TC, GLA_DK), lambda b, h, t: (tok(b, h, t), k_blk0 + h)),
            pl.BlockSpec((GLA_TC, GLA_DV), lambda b, h, t: (tok(b, h, t), v_blk0 + h)),
            pl.BlockSpec((GLA_TC, GLA_DV), lambda b, h, t: (tok(b, h, t), r_blk0 + h)),
            pl.BlockSpec((GLA_TC, GLA_DK), lambda b, h, t: (tok(b, h, t), h)),
            pl.BlockSpec((1, GLA_DV), lambda b, h, t: (0, 0)),
        ],
        out_specs=pl.BlockSpec((GLA_TC, GLA_DV), lambda b, h, t: (tok(b, h, t), h)),
        out_shape=jax.ShapeDtypeStruct((n, GLA_WIDTH), F32),
        scratch_shapes=[pltpu.VMEM((GLA_DK, GLA_DV), F32)],
        compiler_params=_params("parallel", "parallel", "arbitrary"),
        name="gla",
    )(u, u, u, u, la, g_out.reshape(1, -1))


def _moba_kernel(qt_ref, kb_ref, v3_ref, kmean_ref, o_ref, kaug_ref, snext_ref):
    ti = pl.program_id(2)
    last_tile = pl.num_programs(2) - 1
    bs = MOBA_BLOCK
    tq = MOBA_TQ
    nb = kb_ref.shape[0] // bs
    dh = MOBA_DH
    vr = MOBA_VROWS
    scale = dh ** -0.5

    @pl.when(ti == 0)
    def _():
        seq = kb_ref.shape[0]
        row_blk = lax.broadcasted_iota(jnp.int32, (seq, MXU_DIM - dh), 0) // bs
        lane = lax.broadcasted_iota(jnp.int32, (seq, MXU_DIM - dh), 1)
        onehot = (row_blk == lane).astype(BF16)
        for g in range(MOBA_G):
            kaug_ref[g, :, :dh] = kb_ref[:, g * dh:(g + 1) * dh]
            kaug_ref[g, :, dh:] = onehot

    blk = lax.broadcasted_iota(jnp.int32, (nb, tq), 0)
    q_blk = 2 * ti + (lax.broadcasted_iota(jnp.int32, (nb, tq), 1) // bs)
    past = blk < q_blk
    key_row = lax.broadcasted_iota(jnp.int32, (tq, tq), 0)
    qry_col = lax.broadcasted_iota(jnp.int32, (tq, tq), 1)
    causal = key_row <= qry_col
    same_blk = (key_row // bs) == (qry_col // bs)
    diag = pl.multiple_of(ti * tq, tq)

    def pv_dot(g, t, p):
        return (_dot(v3_ref[2 * t, g * vr:(g + 1) * vr, :], p[:bs])
                + _dot(v3_ref[2 * t + 1, g * vr:(g + 1) * vr, :], p[bs:]))

    q_aug, state = [], []
    for g in range(MOBA_G):
        qt = qt_ref[g * dh:(g + 1) * dh, :]
        gate = jnp.dot(kmean_ref[:, g * dh:(g + 1) * dh], qt,
                       precision=lax.Precision.HIGHEST, preferred_element_type=F32)
        gate = jnp.where(past, gate, NEG_INF)
        rank = jnp.zeros((nb, tq), jnp.int32)
        for j in range(nb):
            gj = gate[j:j + 1, :]
            ahead = jnp.logical_or(gj > gate, jnp.logical_and(gj == gate, blk > j))
            rank = rank + ahead.astype(jnp.int32)
        chosen = jnp.logical_and(past, rank < MOBA_TOPK)
        bias = jnp.where(chosen, 0.0, NEG_INF)
        qa = jnp.concatenate([(qt * scale).astype(BF16), bias.astype(BF16),
                              jnp.zeros((MXU_DIM - dh - nb, tq), BF16)], axis=0)
        q_aug.append(qa)

        first_chosen = jnp.sum(jnp.where(blk == 2 * ti, chosen.astype(F32), 0.0),
                               axis=0, keepdims=True) > 0.5
        allowed = jnp.logical_and(causal, jnp.logical_or(same_blk, first_chosen))
        s = _dot(kaug_ref[g, pl.ds(diag, tq), :dh], qa[:dh])
        s = jnp.where(allowed, s, NEG_INF)
        m = jnp.max(s, axis=0, keepdims=True)
        p = jnp.exp(s - m).astype(BF16)
        state.append((m, pv_dot(g, ti, p)))

    def score(g, t):
        off = pl.multiple_of(t * tq, tq)
        s = _dot(kaug_ref[g, pl.ds(off, tq), :], q_aug[g])
        return s, jnp.max(s, axis=0, keepdims=True)

    def consume(g, t, s, tile_max, m, acc, valid=None):
        if valid is not None:
            tile_max = jnp.where(valid, tile_max, NEG_INF)
        m_new = jnp.maximum(m, tile_max)
        alpha = jnp.exp(m - m_new)
        shift = m_new if valid is None else jnp.where(valid, m_new, -NEG_INF)
        p = jnp.exp(s - shift).astype(BF16)
        return m_new, alpha * acc + pv_dot(g, t, p)

    carry0 = []
    for g in range(MOBA_G):
        s, tile_max = score(g, 0)
        snext_ref[g] = s
        carry0.append(state[g] + (tile_max,))

    def body(u, carry):
        out = []
        for g in range(MOBA_G):
            m, acc, max_a = carry[g]
            s_b, max_b = score(g, 2 * u + 1)
            m, acc = consume(g, 2 * u, snext_ref[g], max_a, m, acc)
            s_c, max_c = score(g, jnp.minimum(2 * u + 2, last_tile))
            snext_ref[g] = s_c
            m, acc = consume(g, 2 * u + 1, s_b, max_b, m, acc, valid=2 * u + 1 < ti)
            out.append((m, acc, max_c))
        return tuple(out)

    carry = lax.fori_loop(0, (ti + 1) // 2, body, tuple(carry0))
    for g in range(MOBA_G):
        m, acc, _ = carry[g]
        o_ref[:, g * dh:(g + 1) * dh] = (acc[:dh] * (1.0 / acc[dh:dh + 1])).T


def _moba(q_t, kb, kmean, v3, batch, seq):
    n = kb.shape[0]
    nb = seq // MOBA_BLOCK
    nt = seq // MOBA_TQ
    gw = MOBA_G * MOBA_DH
    return pl.pallas_call(
        _moba_kernel,
        grid=(batch, MOBA_HEADS // MOBA_G, nt),
        in_specs=[
            pl.BlockSpec((gw, MOBA_TQ), lambda b, h, i: (h, b * nt + i)),
            pl.BlockSpec((seq, gw), lambda b, h, i: (b, h)),
            pl.BlockSpec((nb, MOBA_G * MOBA_VROWS, MOBA_BLOCK), lambda b, h, i: (b, h, 0)),
            pl.BlockSpec((nb, gw), lambda b, h, i: (b, h)),
        ],
        out_specs=pl.BlockSpec((MOBA_TQ, gw), lambda b, h, i: (b * nt + i, h)),
        out_shape=jax.ShapeDtypeStruct((n, MOBA_WIDTH), F32),
        scratch_shapes=[pltpu.VMEM((MOBA_G, seq, MXU_DIM), BF16),
                        pltpu.VMEM((MOBA_G, MOBA_TQ, MOBA_TQ), F32)],
        compiler_params=_params("parallel", "parallel", "arbitrary"),
        name="moba",
    )(q_t, kb, v3, kmean)


def _memkv_kernel(mem_ref, g_ref, wkv_ref, gk_ref, k_ref, v_ref):
    h = _rms(mem_ref[...], g_ref[...]).astype(BF16)
    kv = _dot(h, wkv_ref[...])
    gk = gk_ref[...]
    ks = [_rms(kv[:, i * XATTN_DH:(i + 1) * XATTN_DH], gk) for i in range(XATTN_HEADS)]
    k_ref[...] = jnp.concatenate(ks, axis=1)
    v_ref[...] = kv[:, XATTN_WIDTH:]


def _memkv(mem, g, wkv_b, gk):
    m, d = mem.shape
    return pl.pallas_call(
        _memkv_kernel,
        out_shape=[jax.ShapeDtypeStruct((m, XATTN_WIDTH), F32),
                   jax.ShapeDtypeStruct((m, XATTN_WIDTH), F32)],
        compiler_params=pltpu.CompilerParams(vmem_limit_bytes=VMEM_LIMIT),
        name="memkv",
    )(mem, g.reshape(1, d), wkv_b, gk.reshape(1, -1))


def _outx_kernel(x1_ref, og_ref, om_ref, wout_ref, gx_ref, wq_ref, gq_ref, kx_ref, vx_ref, wo_ref,
                 o_ref):
    x2 = (x1_ref[...]
          + _dot(og_ref[...].astype(BF16), wout_ref[:GLA_WIDTH, :])
          + _dot(om_ref[...].astype(BF16), wout_ref[GLA_WIDTH:, :]))
    h = _rms(x2, gx_ref[...]).astype(BF16)
    q = _dot(h, wq_ref[...])
    gq = gq_ref[...]
    scale = XATTN_DH ** -0.5
    heads = []
    for i in range(XATTN_HEADS):
        cols = slice(i * XATTN_DH, (i + 1) * XATTN_DH)
        qh = _rms(q[:, cols], gq).astype(BF16)
        s = _dot_nt(qh, kx_ref[:, cols].astype(BF16)) * scale
        p = jnp.exp(s - jnp.max(s, axis=-1, keepdims=True))
        p = p * (1.0 / jnp.sum(p, axis=-1, keepdims=True))
        heads.append(_dot(p.astype(BF16), vx_ref[:, cols].astype(BF16)))
    o = jnp.concatenate(heads, axis=1).astype(BF16)
    o_ref[...] = x2 + _dot(o, wo_ref[...])


def _outx(x1, og, om, wout_b, gx, wq_b, gq, kx, vx, wo_b, seq, mem_len):
    n, d = x1.shape
    tiles_per_seq = seq // OUTX_TM
    const = lambda i: (0, 0)
    return pl.pallas_call(
        _outx_kernel,
        grid=(n // OUTX_TM,),
        in_specs=[
            pl.BlockSpec((OUTX_TM, d), lambda i: (i, 0)),
            pl.BlockSpec((OUTX_TM, GLA_WIDTH), lambda i: (i, 0)),
            pl.BlockSpec((OUTX_TM, MOBA_WIDTH), lambda i: (i, 0)),
            pl.BlockSpec((GLA_WIDTH + MOBA_WIDTH, d), const),
            pl.BlockSpec((1, d), const),
            pl.BlockSpec((d, XATTN_WIDTH), const),
            pl.BlockSpec((1, XATTN_DH), const),
            pl.BlockSpec((mem_len, XATTN_WIDTH), lambda i: (i // tiles_per_seq, 0)),
            pl.BlockSpec((mem_len, XATTN_WIDTH), lambda i: (i // tiles_per_seq, 0)),
            pl.BlockSpec((XATTN_WIDTH, d), const),
        ],
        out_specs=pl.BlockSpec((OUTX_TM, d), lambda i: (i, 0)),
        out_shape=jax.ShapeDtypeStruct((n, d), F32),
        compiler_params=_params("parallel"),
        name="outx",
    )(x1, og, om, wout_b, gx.reshape(1, d), wq_b, gq.reshape(1, -1), kx, vx, wo_b)


def kernel(x, mem, ffn1_norm, ffn1_w_gate, ffn1_w_up, ffn1_w_down, mix_norm, w_in, gla_w_gate2, gla_b_gate2, gla_out_norm, moba_q_norm, moba_k_norm, w_out, xattn_norm, mem_norm, xattn_w_q, xattn_w_kv, xattn_w_o, xattn_q_norm, xattn_k_norm, ffn2_norm, ffn2_w_gate, ffn2_w_up, ffn2_w_down):
    batch, seq, d = x.shape
    mem_len = mem.shape[1]
    depth = ffn1_norm.shape[0]
    n = batch * seq
    assert seq % GLA_TC == 0 and seq % (2 * MOBA_BLOCK) == 0 and seq % OUTX_TM == 0
    assert seq // MOBA_BLOCK + MOBA_DH <= MXU_DIM and seq // MOBA_BLOCK == 16
    assert n % FFN_TM == 0 and n % PROJ_TM == 0 and PROJ_TM % MOBA_BLOCK == 0

    xf = x.reshape(n, d)
    memf = mem.reshape(batch * mem_len, d)
    lr0 = GLA_COLS
    mq0 = lr0 + GLA_GATE_RANK
    w_in_t = jnp.swapaxes(w_in, 1, 2)
    for l in range(depth):
        w_lr_t = w_in_t[l, lr0:mq0]
        w_mq_t = w_in_t[l, mq0:mq0 + MOBA_WIDTH]
        w_mk_t = w_in_t[l, mq0 + MOBA_WIDTH:mq0 + 2 * MOBA_WIDTH]
        w_mv_t = w_in_t[l, mq0 + 2 * MOBA_WIDTH:]

        x1 = _ffn(xf, ffn1_norm[l], ffn1_w_gate[l], ffn1_w_up[l], ffn1_w_down[l])
        u, la, hn = _projg(x1, mix_norm[l], w_in_t, l, w_lr_t, gla_w_gate2[l], gla_b_gate2[l])
        q_t, kb, kmean, v3 = _proj_moba(hn, w_mq_t, w_mk_t, w_mv_t, moba_q_norm[l], moba_k_norm[l])
        o_gla = _gla(u, la, gla_out_norm[l], batch, seq)
        o_moba = _moba(q_t, kb, kmean, v3, batch, seq)
        kx, vx = _memkv(memf, mem_norm[l], xattn_w_kv[l].astype(BF16), xattn_k_norm[l])
        x3 = _outx(x1, o_gla, o_moba, w_out[l].astype(BF16), xattn_norm[l],
                   xattn_w_q[l].astype(BF16), xattn_q_norm[l], kx, vx,
                   xattn_w_o[l].astype(BF16), seq, mem_len)
        xf = _ffn(x3, ffn2_norm[l], ffn2_w_gate[l], ffn2_w_up[l], ffn2_w_down[l])
    return xf.reshape(batch, seq, d)
```

```python
import jax
import jax.numpy as jnp
from jax import lax
from jax.experimental import pallas as pl
from jax.experimental.pallas import tpu as pltpu

F32 = jnp.float32
BF16 = jnp.bfloat16

EPS = 1e-6
NEG_INF = -1e30
D_FF = 5632
GLA_HEADS = 4
GLA_DV = 256
GLA_DK = 128
GLA_QK = GLA_HEADS * GLA_DK
GLA_WIDTH = GLA_HEADS * GLA_DV
GLA_COLS = 2 * GLA_QK + 2 * GLA_WIDTH
GLA_GATE_RANK = 16
GLA_GATE_TAU = 16.0
GLA_CHUNK = 64
MOBA_DH = 128
MOBA_HEADS = 8
MOBA_WIDTH = MOBA_HEADS * MOBA_DH
MOBA_BLOCK = 256
MOBA_TOPK = 3
XATTN_HEADS = 4
XATTN_DH = 128
XATTN_WIDTH = XATTN_HEADS * XATTN_DH

V7X_VMEM_BYTES = 64 * 1024 * 1024
VMEM_LIMIT = V7X_VMEM_BYTES - 6 * 1024 * 1024
MXU_DIM = 256

FFN_TM = 1024
FFN_TF = 256
NORM_TM = 1024
PROJ_TM = 2048
PROJ_TN = 512
GLA_TC = 512
MOBA_G = 2
MOBA_TQ = 2 * MOBA_BLOCK
MOBA_VPAD = 16
MOBA_VROWS = MOBA_DH + MOBA_VPAD
OUTX_TM = 512


def _params(*semantics):
    return pltpu.CompilerParams(dimension_semantics=semantics, vmem_limit_bytes=VMEM_LIMIT)


def _rms(x, g):
    return x * lax.rsqrt(jnp.mean(x * x, axis=-1, keepdims=True) + EPS) * g


def _dot(a, b):
    return jnp.dot(a, b, preferred_element_type=F32)


def _dot_nt(a, b):
    return lax.dot_general(a, b, (((1,), (1,)), ((), ())), preferred_element_type=F32)


def _silu(x):
    return x * (1.0 / (1.0 + jnp.exp(-x)))


def _ffn_kernel(x_ref, g_ref, wg_ref, wu_ref, wd_ref, o_ref, h_ref):
    f = pl.program_id(1)

    @pl.when(f == 0)
    def _():
        h_ref[...] = _rms(x_ref[...], g_ref[...]).astype(BF16)
        o_ref[...] = jnp.zeros_like(o_ref)

    h = h_ref[...]
    gate = _dot(h, wg_ref[...].astype(BF16))
    up = _dot(h, wu_ref[...].astype(BF16))
    act = (_silu(gate) * up).astype(BF16)
    o_ref[...] += _dot(act, wd_ref[...].astype(BF16))

    @pl.when(f == pl.num_programs(1) - 1)
    def _():
        o_ref[...] = x_ref[...] + 0.5 * o_ref[...]


def _ffn(x, g, wg, wu, wd):
    n, d = x.shape
    grid = (n // FFN_TM, D_FF // FFN_TF)
    return pl.pallas_call(
        _ffn_kernel,
        grid=grid,
        in_specs=[
            pl.BlockSpec((FFN_TM, d), lambda i, f: (i, 0)),
            pl.BlockSpec((1, d), lambda i, f: (0, 0)),
            pl.BlockSpec((d, FFN_TF), lambda i, f: (0, f)),
            pl.BlockSpec((d, FFN_TF), lambda i, f: (0, f)),
            pl.BlockSpec((FFN_TF, d), lambda i, f: (f, 0)),
        ],
        out_specs=pl.BlockSpec((FFN_TM, d), lambda i, f: (i, 0)),
        out_shape=jax.ShapeDtypeStruct((n, d), F32),
        scratch_shapes=[pltpu.VMEM((FFN_TM, d), BF16)],
        compiler_params=_params("parallel", "arbitrary"),
        name="ffn",
    )(x, g.reshape(1, d), wg, wu, wd)


def _mixnorm_kernel(x_ref, g_ref, wlr_ref, wg2_ref, bg2_ref, hn_ref, la_ref):
    h = _rms(x_ref[...], g_ref[...]).astype(BF16)
    hn_ref[...] = h
    lr = _dot_nt(h, wlr_ref[...].astype(BF16))
    pre = _dot(lr.astype(BF16), wg2_ref[...].astype(BF16)) + bg2_ref[...]
    log_sig = -(jnp.maximum(-pre, 0.0) + jnp.log(1.0 + jnp.exp(-jnp.abs(pre))))
    la_ref[...] = log_sig * (1.0 / GLA_GATE_TAU)


def _mixnorm(x1, g, w_lr_t, wg2, bg2):
    n, d = x1.shape
    const = lambda i: (0, 0)
    return pl.pallas_call(
        _mixnorm_kernel,
        grid=(n // NORM_TM,),
        in_specs=[
            pl.BlockSpec((NORM_TM, d), lambda i: (i, 0)),
            pl.BlockSpec((1, d), const),
            pl.BlockSpec((GLA_GATE_RANK, d), const),
            pl.BlockSpec((GLA_GATE_RANK, GLA_QK), const),
            pl.BlockSpec((1, GLA_QK), const),
        ],
        out_specs=[pl.BlockSpec((NORM_TM, d), lambda i: (i, 0)),
                   pl.BlockSpec((NORM_TM, GLA_QK), lambda i: (i, 0))],
        out_shape=[jax.ShapeDtypeStruct((n, d), BF16),
                   jax.ShapeDtypeStruct((n, GLA_QK), F32)],
        compiler_params=_params("parallel"),
        name="mixnorm",
    )(x1, g.reshape(1, d), w_lr_t, wg2, bg2.reshape(1, -1))


def _projg_kernel(hn_ref, wt_ref, u_ref):
    u_ref[...] = _dot_nt(hn_ref[...], wt_ref[...].astype(BF16))


def _projg(hn, w_in_t, layer):
    n, d = hn.shape
    return pl.pallas_call(
        _projg_kernel,
        grid=(n // PROJ_TM, GLA_COLS // PROJ_TN),
        in_specs=[
            pl.BlockSpec((PROJ_TM, d), lambda i, j: (i, 0)),
            pl.BlockSpec((None, PROJ_TN, d), lambda i, j: (layer, j, 0)),
        ],
        out_specs=pl.BlockSpec((PROJ_TM, PROJ_TN), lambda i, j: (i, j)),
        out_shape=jax.ShapeDtypeStruct((n, GLA_COLS), F32),
        compiler_params=_params("parallel", "arbitrary"),
        name="projg",
    )(hn, w_in_t)


def _projq_kernel(hn_ref, wt_ref, g_ref, qt_ref):
    yt = _dot_nt(wt_ref[...].astype(BF16), hn_ref[...])
    g = g_ref[...]
    parts = []
    for k in range(PROJ_TN // MOBA_DH):
        p = yt[k * MOBA_DH:(k + 1) * MOBA_DH, :]
        inv = lax.rsqrt(jnp.mean(p * p, axis=0, keepdims=True) + EPS)
        parts.append(p * inv * g)
    qt_ref[...] = jnp.concatenate(parts, axis=0)


def _projk_kernel(hn_ref, wt_ref, g_ref, kb_ref, kmean_ref):
    y = _dot_nt(hn_ref[...], wt_ref[...].astype(BF16))
    g = g_ref[...]
    parts = [_rms(y[:, k * MOBA_DH:(k + 1) * MOBA_DH], g) for k in range(PROJ_TN // MOBA_DH)]
    kn = jnp.concatenate(parts, axis=1)
    kb_ref[...] = kn.astype(BF16)
    blocks = PROJ_TM // MOBA_BLOCK
    kmean_ref[0] = jnp.sum(kn.reshape(blocks, MOBA_BLOCK, PROJ_TN), axis=1) * (1.0 / MOBA_BLOCK)


def _projv_kernel(hn_ref, wt_ref, v3_ref):
    yt = _dot_nt(wt_ref[...].astype(BF16), hn_ref[...]).astype(BF16)
    ones = jnp.ones((MOBA_VPAD, PROJ_TM), BF16)
    parts = []
    for k in range(PROJ_TN // MOBA_DH):
        parts += [yt[k * MOBA_DH:(k + 1) * MOBA_DH], ones]
    ya = jnp.concatenate(parts, axis=0)
    for t in range(PROJ_TM // MOBA_BLOCK):
        v3_ref[t] = ya[:, t * MOBA_BLOCK:(t + 1) * MOBA_BLOCK]


def _proj_moba(hn, w_q_t, w_k_t, w_v_t, gq, gk):
    n, d = hn.shape
    grid = (n // PROJ_TM, MOBA_WIDTH // PROJ_TN)
    hn_spec = pl.BlockSpec((PROJ_TM, d), lambda i, j: (i, 0))
    wt_spec = pl.BlockSpec((PROJ_TN, d), lambda i, j: (j, 0))
    params = _params("parallel", "arbitrary")
    blocks = PROJ_TM // MOBA_BLOCK

    q_t = pl.pallas_call(
        _projq_kernel, grid=grid,
        in_specs=[hn_spec, wt_spec, pl.BlockSpec((MOBA_DH, 1), lambda i, j: (0, 0))],
        out_specs=pl.BlockSpec((PROJ_TN, PROJ_TM), lambda i, j: (j, i)),
        out_shape=jax.ShapeDtypeStruct((MOBA_WIDTH, n), F32),
        compiler_params=params, name="projq",
    )(hn, w_q_t, gq.reshape(-1, 1))

    kb, kmean = pl.pallas_call(
        _projk_kernel, grid=grid,
        in_specs=[hn_spec, wt_spec, pl.BlockSpec((1, MOBA_DH), lambda i, j: (0, 0))],
        out_specs=[pl.BlockSpec((PROJ_TM, PROJ_TN), lambda i, j: (i, j)),
                   pl.BlockSpec((1, blocks, PROJ_TN), lambda i, j: (i, 0, j))],
        out_shape=[jax.ShapeDtypeStruct((n, MOBA_WIDTH), BF16),
                   jax.ShapeDtypeStruct((n // PROJ_TM, blocks, MOBA_WIDTH), F32)],
        compiler_params=params, name="projk",
    )(hn, w_k_t, gk.reshape(1, -1))

    v3 = pl.pallas_call(
        _projv_kernel, grid=grid,
        in_specs=[hn_spec, wt_spec],
        out_specs=pl.BlockSpec((blocks, PROJ_TN // MOBA_DH * MOBA_VROWS, MOBA_BLOCK),
                               lambda i, j: (i, j, 0)),
        out_shape=jax.ShapeDtypeStruct((n // MOBA_BLOCK, MOBA_HEADS * MOBA_VROWS, MOBA_BLOCK), BF16),
        compiler_params=params, name="projv",
    )(hn, w_v_t)
    return q_t, kb, kmean.reshape(n // MOBA_BLOCK, MOBA_WIDTH), v3


def _dot_exact_rhs(m, a):
    a1 = a.astype(BF16)
    r1 = a - a1.astype(F32)
    a2 = r1.astype(BF16)
    a3 = (r1 - a2.astype(F32)).astype(BF16)
    return _dot(m, a1) + _dot(m, a2) + _dot(m, a3)


def _gla_kernel(q_ref, k_ref, v_ref, r_ref, la_ref, go_ref, o_ref, s_ref, upd_ref):
    t = pl.program_id(2)
    tc = GLA_TC
    nchunk = tc // GLA_CHUNK

    @pl.when(t == 0)
    def _():
        s_ref[...] = jnp.zeros_like(s_ref)

    row = lax.broadcasted_iota(jnp.int32, (tc, tc), 0)
    col = lax.broadcasted_iota(jnp.int32, (tc, tc), 1)
    causal = jnp.logical_and((row // GLA_CHUNK) == (col // GLA_CHUNK), col <= row)

    la = la_ref[...]
    crow = lax.broadcasted_iota(jnp.int32, (GLA_CHUNK, GLA_CHUNK), 0)
    ccol = lax.broadcasted_iota(jnp.int32, (GLA_CHUNK, GLA_CHUNK), 1)
    tri = (ccol <= crow).astype(BF16)
    la_wide = jnp.concatenate([la[c * GLA_CHUNK:(c + 1) * GLA_CHUNK] for c in range(nchunk)], axis=1)
    bc_wide = _dot_exact_rhs(tri, la_wide)
    bcum = jnp.concatenate([bc_wide[:, c * GLA_DK:(c + 1) * GLA_DK] for c in range(nchunk)], axis=0)
    btot = jnp.concatenate(
        [jnp.broadcast_to(bc_wide[GLA_CHUNK - 1:GLA_CHUNK, c * GLA_DK:(c + 1) * GLA_DK],
                          (GLA_CHUNK, GLA_DK)) for c in range(nchunk)], axis=0)

    k = k_ref[...]
    q_dec = (q_ref[...] * (GLA_DK ** -0.5)) * jnp.exp(bcum)
    k_inv = k * jnp.exp(-bcum)
    k_tail_t = (k * jnp.exp(btot - bcum)).T
    decay_t = jnp.exp(btot).T
    v = v_ref[...].astype(BF16)

    q_dec_b = q_dec.astype(BF16)
    att = jnp.where(causal, _dot_nt(q_dec_b, k_inv.astype(BF16)), 0.0)
    o = _dot(att.astype(BF16), v)

    lane_chunk = lax.broadcasted_iota(jnp.int32, (GLA_DK, tc), 1) // GLA_CHUNK
    for c in range(nchunk):
        kt_c = jnp.where(lane_chunk == c, k_tail_t, 0.0).astype(BF16)
        upd_ref[c] = _dot(kt_c, v)

    state = s_ref[...]
    inter = []
    for c in range(nchunk):
        rows = slice(c * GLA_CHUNK, (c + 1) * GLA_CHUNK)
        inter.append(_dot(q_dec_b[rows], state.astype(BF16)))
        dec_c = decay_t[:, c * GLA_CHUNK:c * GLA_CHUNK + 1]
        state = dec_c * state + upd_ref[c]
    s_ref[...] = state
    o = o + jnp.concatenate(inter, axis=0)

    o = _rms(o, go_ref[...]) * _silu(r_ref[...])
    o_ref[...] = o


def _gla(u, la, g_out, batch, seq):
    n = u.shape[0]
    nt = seq // GLA_TC
    tok = lambda b, h, t: b * nt + t
    k_blk0 = GLA_QK // GLA_DK
    v_blk0 = 2 * GLA_QK // GLA_DV
    r_blk0 = v_blk0 + GLA_WIDTH // GLA_DV
    return pl.pallas_call(
        _gla_kernel,
        grid=(batch, GLA_HEADS, nt),
        in_specs=[
            pl.BlockSpec((GLA_TC, GLA_DK), lambda b, h, t: (tok(b, h, t), h)),
            pl.BlockSpec((GLA_TC, GLA_DK), lambda b, h, t: (tok(b, h, t), k_blk0 + h)),
            pl.BlockSpec((GLA_TC, GLA_DV), lambda b, h, t: (tok(b, h, t), v_blk0 + h)),
            pl.BlockSpec((GLA_TC, GLA_DV), lambda b, h, t: (tok(b, h, t), r_blk0 + h)),
            pl.BlockSpec((GLA_TC, GLA_DK), lambda b, h, t: (tok(b, h, t), h)),
            pl.BlockSpec((1, GLA_DV), lambda b, h, t: (0, 0)),
        ],
        out_specs=pl.BlockSpec((GLA_TC, GLA_DV), lambda b, h, t: (tok(b, h, t), h)),
        out_shape=jax.ShapeDtypeStruct((n, GLA_WIDTH), F32),
        scratch_shapes=[pltpu.VMEM((GLA_DK, GLA_DV), F32),
                        pltpu.VMEM((GLA_TC // GLA_CHUNK, GLA_DK, GLA_DV), F32)],
        compiler_params=_params("parallel", "parallel", "arbitrary"),
        name="gla",
    )(u, u, u, u, la, g_out.reshape(1, -1))


def _moba_kernel(qt_ref, kb_ref, v3_ref, kmean_ref, o_ref, kaug_ref, snext_ref):
    ti = pl.program_id(2)
    last_tile = pl.num_programs(2) - 1
    bs = MOBA_BLOCK
    tq = MOBA_TQ
    nb = kb_ref.shape[0] // bs
    dh = MOBA_DH
    vr = MOBA_VROWS
    scale = dh ** -0.5

    @pl.when(ti == 0)
    def _():
        seq = kb_ref.shape[0]
        row_blk = lax.broadcasted_iota(jnp.int32, (seq, MXU_DIM - dh), 0) // bs
        lane = lax.broadcasted_iota(jnp.int32, (seq, MXU_DIM - dh), 1)
        onehot = (row_blk == lane).astype(BF16)
        for g in range(MOBA_G):
            kaug_ref[g, :, :dh] = kb_ref[:, g * dh:(g + 1) * dh]
            kaug_ref[g, :, dh:] = onehot

    blk = lax.broadcasted_iota(jnp.int32, (nb, tq), 0)
    q_blk = 2 * ti + (lax.broadcasted_iota(jnp.int32, (nb, tq), 1) // bs)
    past = blk < q_blk
    key_row = lax.broadcasted_iota(jnp.int32, (tq, tq), 0)
    qry_col = lax.broadcasted_iota(jnp.int32, (tq, tq), 1)
    causal = key_row <= qry_col
    same_blk = (key_row // bs) == (qry_col // bs)
    diag = pl.multiple_of(ti * tq, tq)

    def pv_dot(g, t, p):
        return (_dot(v3_ref[2 * t, g * vr:(g + 1) * vr, :], p[:bs])
                + _dot(v3_ref[2 * t + 1, g * vr:(g + 1) * vr, :], p[bs:]))

    q_aug, state = [], []
    for g in range(MOBA_G):
        qt = qt_ref[g * dh:(g + 1) * dh, :]
        gate = jnp.dot(kmean_ref[:, g * dh:(g + 1) * dh], qt,
                       precision=lax.Precision.HIGHEST, preferred_element_type=F32)
        gate = jnp.where(past, gate, NEG_INF)
        rank = jnp.zeros((nb, tq), jnp.int32)
        for j in range(nb):
            gj = gate[j:j + 1, :]
            ahead = jnp.logical_or(gj > gate, jnp.logical_and(gj == gate, blk > j))
            rank = rank + ahead.astype(jnp.int32)
        chosen = jnp.logical_and(past, rank < MOBA_TOPK)
        bias = jnp.where(chosen, 0.0, NEG_INF)
        qa = jnp.concatenate([(qt * scale).astype(BF16), bias.astype(BF16),
                              jnp.zeros((MXU_DIM - dh - nb, tq), BF16)], axis=0)
        q_aug.append(qa)

        first_chosen = jnp.sum(jnp.where(blk == 2 * ti, chosen.astype(F32), 0.0),
                               axis=0, keepdims=True) > 0.5
        allowed = jnp.logical_and(causal, jnp.logical_or(same_blk, first_chosen))
        s = _dot(kaug_ref[g, pl.ds(diag, tq), :dh], qa[:dh])
        s = jnp.where(allowed, s, NEG_INF)
        m = jnp.max(s, axis=0, keepdims=True)
        p = jnp.exp(s - m).astype(BF16)
        state.append((m, pv_dot(g, ti, p)))

    def score(g, t):
        off = pl.multiple_of(t * tq, tq)
        s = _dot(kaug_ref[g, pl.ds(off, tq), :], q_aug[g])
        return s, jnp.max(s, axis=0, keepdims=True)

    def consume(g, t, s, tile_max, m, acc, valid=None):
        if valid is not None:
            tile_max = jnp.where(valid, tile_max, NEG_INF)
        m_new = jnp.maximum(m, tile_max)
        alpha = jnp.exp(m - m_new)
        shift = m_new if valid is None else jnp.where(valid, m_new, -NEG_INF)
        p = jnp.exp(s - shift).astype(BF16)
        return m_new, alpha * acc + pv_dot(g, t, p)

    carry0 = []
    for g in range(MOBA_G):
        s, tile_max = score(g, 0)
        snext_ref[g] = s
        carry0.append(state[g] + (tile_max,))

    def body(u, carry):
        out = []
        for g in range(MOBA_G):
            m, acc, max_a = carry[g]
            s_b, max_b = score(g, 2 * u + 1)
            m, acc = consume(g, 2 * u, snext_ref[g], max_a, m, acc)
            s_c, max_c = score(g, jnp.minimum(2 * u + 2, last_tile))
            snext_ref[g] = s_c
            m, acc = consume(g, 2 * u + 1, s_b, max_b, m, acc, valid=2 * u + 1 < ti)
            out.append((m, acc, max_c))
        return tuple(out)

    carry = lax.fori_loop(0, (ti + 1) // 2, body, tuple(carry0))
    for g in range(MOBA_G):
        m, acc, _ = carry[g]
        o_ref[:, g * dh:(g + 1) * dh] = (acc[:dh] * (1.0 / acc[dh:dh + 1])).T


def _moba(q_t, kb, kmean, v3, batch, seq):
    n = kb.shape[0]
    nb = seq // MOBA_BLOCK
    nt = seq // MOBA_TQ
    gw = MOBA_G * MOBA_DH
    return pl.pallas_call(
        _moba_kernel,
        grid=(batch, MOBA_HEADS // MOBA_G, nt),
        in_specs=[
            pl.BlockSpec((gw, MOBA_TQ), lambda b, h, i: (h, b * nt + i)),
            pl.BlockSpec((seq, gw), lambda b, h, i: (b, h)),
            pl.BlockSpec((nb, MOBA_G * MOBA_VROWS, MOBA_BLOCK), lambda b, h, i: (b, h, 0)),
            pl.BlockSpec((nb, gw), lambda b, h, i: (b, h)),
        ],
        out_specs=pl.BlockSpec((MOBA_TQ, gw), lambda b, h, i: (b * nt + i, h)),
        out_shape=jax.ShapeDtypeStruct((n, MOBA_WIDTH), F32),
        scratch_shapes=[pltpu.VMEM((MOBA_G, seq, MXU_DIM), BF16),
                        pltpu.VMEM((MOBA_G, MOBA_TQ, MOBA_TQ), F32)],
        compiler_params=_params("parallel", "parallel", "arbitrary"),
        name="moba",
    )(q_t, kb, v3, kmean)


def _memkv_kernel(mem_ref, g_ref, wkv_ref, gk_ref, k_ref, v_ref):
    h = _rms(mem_ref[...], g_ref[...]).astype(BF16)
    kv = _dot(h, wkv_ref[...])
    gk = gk_ref[...]
    ks = [_rms(kv[:, i * XATTN_DH:(i + 1) * XATTN_DH], gk) for i in range(XATTN_HEADS)]
    k_ref[...] = jnp.concatenate(ks, axis=1)
    v_ref[...] = kv[:, XATTN_WIDTH:]


def _memkv(mem, g, wkv_b, gk):
    m, d = mem.shape
    return pl.pallas_call(
        _memkv_kernel,
        out_shape=[jax.ShapeDtypeStruct((m, XATTN_WIDTH), F32),
                   jax.ShapeDtypeStruct((m, XATTN_WIDTH), F32)],
        compiler_params=pltpu.CompilerParams(vmem_limit_bytes=VMEM_LIMIT),
        name="memkv",
    )(mem, g.reshape(1, d), wkv_b, gk.reshape(1, -1))


def _outx_kernel(x1_ref, og_ref, om_ref, wout_ref, gx_ref, wq_ref, gq_ref, kx_ref, vx_ref, wo_ref,
                 o_ref):
    x2 = (x1_ref[...]
          + _dot(og_ref[...].astype(BF16), wout_ref[:GLA_WIDTH, :])
          + _dot(om_ref[...].astype(BF16), wout_ref[GLA_WIDTH:, :]))
    h = _rms(x2, gx_ref[...]).astype(BF16)
    q = _dot(h, wq_ref[...])
    gq = gq_ref[...]
    scale = XATTN_DH ** -0.5
    heads = []
    for i in range(XATTN_HEADS):
        cols = slice(i * XATTN_DH, (i + 1) * XATTN_DH)
        qh = _rms(q[:, cols], gq).astype(BF16)
        s = _dot_nt(qh, kx_ref[:, cols].astype(BF16)) * scale
        p = jnp.exp(s - jnp.max(s, axis=-1, keepdims=True))
        p = p * (1.0 / jnp.sum(p, axis=-1, keepdims=True))
        heads.append(_dot(p.astype(BF16), vx_ref[:, cols].astype(BF16)))
    o = jnp.concatenate(heads, axis=1).astype(BF16)
    o_ref[...] = x2 + _dot(o, wo_ref[...])


def _outx(x1, og, om, wout_b, gx, wq_b, gq, kx, vx, wo_b, seq, mem_len):
    n, d = x1.shape
    tiles_per_seq = seq // OUTX_TM
    const = lambda i: (0, 0)
    return pl.pallas_call(
        _outx_kernel,
        grid=(n // OUTX_TM,),
        in_specs=[
            pl.BlockSpec((OUTX_TM, d), lambda i: (i, 0)),
            pl.BlockSpec((OUTX_TM, GLA_WIDTH), lambda i: (i, 0)),
            pl.BlockSpec((OUTX_TM, MOBA_WIDTH), lambda i: (i, 0)),
            pl.BlockSpec((GLA_WIDTH + MOBA_WIDTH, d), const),
            pl.BlockSpec((1, d), const),
            pl.BlockSpec((d, XATTN_WIDTH), const),
            pl.BlockSpec((1, XATTN_DH), const),
            pl.BlockSpec((mem_len, XATTN_WIDTH), lambda i: (i // tiles_per_seq, 0)),
            pl.BlockSpec((mem_len, XATTN_WIDTH), lambda i: (i // tiles_per_seq, 0)),
            pl.BlockSpec((XATTN_WIDTH, d), const),
        ],
        out_specs=pl.BlockSpec((OUTX_TM, d), lambda i: (i, 0)),
        out_shape=jax.ShapeDtypeStruct((n, d), F32),
        compiler_params=_params("parallel"),
        name="outx",
    )(x1, og, om, wout_b, gx.reshape(1, d), wq_b, gq.reshape(1, -1), kx, vx, wo_b)


def kernel(x, mem, ffn1_norm, ffn1_w_gate, ffn1_w_up, ffn1_w_down, mix_norm, w_in, gla_w_gate2, gla_b_gate2, gla_out_norm, moba_q_norm, moba_k_norm, w_out, xattn_norm, mem_norm, xattn_w_q, xattn_w_kv, xattn_w_o, xattn_q_norm, xattn_k_norm, ffn2_norm, ffn2_w_gate, ffn2_w_up, ffn2_w_down):
    batch, seq, d = x.shape
    mem_len = mem.shape[1]
    depth = ffn1_norm.shape[0]
    n = batch * seq
    assert seq % GLA_TC == 0 and seq % (2 * MOBA_BLOCK) == 0 and seq % OUTX_TM == 0
    assert seq // MOBA_BLOCK + MOBA_DH <= MXU_DIM and seq // MOBA_BLOCK == 16
    assert n % FFN_TM == 0 and n % NORM_TM == 0 and n % PROJ_TM == 0 and PROJ_TM % MOBA_BLOCK == 0

    xf = x.reshape(n, d)
    memf = mem.reshape(batch * mem_len, d)
    lr0 = GLA_COLS
    mq0 = lr0 + GLA_GATE_RANK
    w_in_t = jnp.swapaxes(w_in, 1, 2)
    for l in range(depth):
        w_lr_t = w_in_t[l, lr0:mq0]
        w_mq_t = w_in_t[l, mq0:mq0 + MOBA_WIDTH]
        w_mk_t = w_in_t[l, mq0 + MOBA_WIDTH:mq0 + 2 * MOBA_WIDTH]
        w_mv_t = w_in_t[l, mq0 + 2 * MOBA_WIDTH:]

        x1 = _ffn(xf, ffn1_norm[l], ffn1_w_gate[l], ffn1_w_up[l], ffn1_w_down[l])
        hn, la = _mixnorm(x1, mix_norm[l], w_lr_t, gla_w_gate2[l], gla_b_gate2[l])
        u = _projg(hn, w_in_t, l)
        q_t, kb, kmean, v3 = _proj_moba(hn, w_mq_t, w_mk_t, w_mv_t, moba_q_norm[l], moba_k_norm[l])
        o_gla = _gla(u, la, gla_out_norm[l], batch, seq)
        o_moba = _moba(q_t, kb, kmean, v3, batch, seq)
        kx, vx = _memkv(memf, mem_norm[l], xattn_w_kv[l].astype(BF16), xattn_k_norm[l])
        x3 = _outx(x1, o_gla, o_moba, w_out[l].astype(BF16), xattn_norm[l],
                   xattn_w_q[l].astype(BF16), xattn_q_norm[l], kx, vx,
                   xattn_w_o[l].astype(BF16), seq, mem_len)
        xf = _ffn(x3, ffn2_norm[l], ffn2_w_gate[l], ffn2_w_up[l], ffn2_w_down[l])
    return xf.reshape(batch, seq, d)
```

```python
import jax
import jax.numpy as jnp
from jax import lax
from jax.experimental import pallas as pl
from jax.experimental.pallas import tpu as pltpu

F32 = jnp.float32
BF16 = jnp.bfloat16

EPS = 1e-6
NEG_INF = -1e30
D_FF = 5632
GLA_HEADS = 4
GLA_DV = 256
GLA_DK = 128
GLA_QK = GLA_HEADS * GLA_DK
GLA_WIDTH = GLA_HEADS * GLA_DV
GLA_COLS = 2 * GLA_QK + 2 * GLA_WIDTH
GLA_GATE_RANK = 16
GLA_GATE_TAU = 16.0
GLA_CHUNK = 64
MOBA_DH = 128
MOBA_HEADS = 8
MOBA_WIDTH = MOBA_HEADS * MOBA_DH
MOBA_BLOCK = 256
MOBA_TOPK = 3
XATTN_HEADS = 4
XATTN_DH = 128
XATTN_WIDTH = XATTN_HEADS * XATTN_DH

V7X_VMEM_BYTES = 64 * 1024 * 1024
VMEM_LIMIT = V7X_VMEM_BYTES - 6 * 1024 * 1024
MXU_DIM = 256

FFN_TM = 1024
FFN_TF = 512
NORM_TM = 1024
PROJ_TM = 2048
PROJ_TN = 512
GLA_TC = 512
MOBA_G = 2
MOBA_TQ = 2 * MOBA_BLOCK
MOBA_VPAD = 16
MOBA_VROWS = MOBA_DH + MOBA_VPAD
OUTX_TM = 512


def _params(*semantics):
    return pltpu.CompilerParams(dimension_semantics=semantics, vmem_limit_bytes=VMEM_LIMIT)


def _rms(x, g):
    return x * lax.rsqrt(jnp.mean(x * x, axis=-1, keepdims=True) + EPS) * g


def _dot(a, b):
    return jnp.dot(a, b, preferred_element_type=F32)


def _dot_nt(a, b):
    return lax.dot_general(a, b, (((1,), (1,)), ((), ())), preferred_element_type=F32)


def _silu(x):
    return x * (1.0 / (1.0 + jnp.exp(-x)))


def _ffn_kernel(x_hbm, g_ref, wg_ref, wu_ref, wd_ref, o_hbm, h_ref, xbuf_ref, acc_ref, xsem, osem):
    i = pl.program_id(0)
    f = pl.program_id(1)
    last_i = pl.num_programs(0) - 1
    last_f = pl.num_programs(1) - 1

    def x_copy(tile):
        return pltpu.make_async_copy(x_hbm.at[pl.ds(tile * FFN_TM, FFN_TM)], xbuf_ref, xsem)

    def o_copy(tile):
        return pltpu.make_async_copy(acc_ref, o_hbm.at[pl.ds(tile * FFN_TM, FFN_TM)], osem)

    @pl.when(jnp.logical_and(i == 0, f == 0))
    def _():
        x_copy(0).start()

    @pl.when(f == 0)
    def _():
        x_copy(i).wait()
        x = xbuf_ref[...]
        h_ref[...] = _rms(x, g_ref[...]).astype(BF16)

        @pl.when(i > 0)
        def _():
            o_copy(i - 1).wait()

        acc_ref[...] = 2.0 * xbuf_ref[...]

    @pl.when(jnp.logical_and(f == 1, i < last_i))
    def _():
        x_copy(i + 1).start()

    h = h_ref[...]
    gate = _dot(h, wg_ref[...].astype(BF16))
    up = _dot(h, wu_ref[...].astype(BF16))
    act = (_silu(gate) * up).astype(BF16)
    acc_ref[...] += _dot(act, wd_ref[...].astype(BF16))

    @pl.when(f == last_f)
    def _():
        acc_ref[...] = 0.5 * acc_ref[...]
        o_copy(i).start()

        @pl.when(i == last_i)
        def _():
            o_copy(i).wait()


def _ffn(x, g, wg, wu, wd):
    n, d = x.shape
    grid = (n // FFN_TM, D_FF // FFN_TF)
    assert grid[1] >= 2
    return pl.pallas_call(
        _ffn_kernel,
        grid=grid,
        in_specs=[
            pl.BlockSpec(memory_space=pl.ANY),
            pl.BlockSpec((1, d), lambda i, f: (0, 0)),
            pl.BlockSpec((d, FFN_TF), lambda i, f: (0, f)),
            pl.BlockSpec((d, FFN_TF), lambda i, f: (0, f)),
            pl.BlockSpec((FFN_TF, d), lambda i, f: (f, 0)),
        ],
        out_specs=pl.BlockSpec(memory_space=pl.ANY),
        out_shape=jax.ShapeDtypeStruct((n, d), F32),
        scratch_shapes=[pltpu.VMEM((FFN_TM, d), BF16), pltpu.VMEM((FFN_TM, d), F32),
                        pltpu.VMEM((FFN_TM, d), F32),
                        pltpu.SemaphoreType.DMA, pltpu.SemaphoreType.DMA],
        compiler_params=_params("arbitrary", "arbitrary"),
        name="ffn",
    )(x, g.reshape(1, d), wg, wu, wd)


def _mixnorm_kernel(x_ref, g_ref, wlr_ref, wg2_ref, bg2_ref, hn_ref, la_ref):
    h = _rms(x_ref[...], g_ref[...]).astype(BF16)
    hn_ref[...] = h
    lr = _dot_nt(h, wlr_ref[...].astype(BF16))
    pre = _dot(lr.astype(BF16), wg2_ref[...].astype(BF16)) + bg2_ref[...]
    log_sig = -(jnp.maximum(-pre, 0.0) + jnp.log(1.0 + jnp.exp(-jnp.abs(pre))))
    la_ref[...] = log_sig * (1.0 / GLA_GATE_TAU)


def _mixnorm(x1, g, w_lr_t, wg2, bg2):
    n, d = x1.shape
    const = lambda i: (0, 0)
    return pl.pallas_call(
        _mixnorm_kernel,
        grid=(n // NORM_TM,),
        in_specs=[
            pl.BlockSpec((NORM_TM, d), lambda i: (i, 0)),
            pl.BlockSpec((1, d), const),
            pl.BlockSpec((GLA_GATE_RANK, d), const),
            pl.BlockSpec((GLA_GATE_RANK, GLA_QK), const),
            pl.BlockSpec((1, GLA_QK), const),
        ],
        out_specs=[pl.BlockSpec((NORM_TM, d), lambda i: (i, 0)),
                   pl.BlockSpec((NORM_TM, GLA_QK), lambda i: (i, 0))],
        out_shape=[jax.ShapeDtypeStruct((n, d), BF16),
                   jax.ShapeDtypeStruct((n, GLA_QK), F32)],
        compiler_params=_params("parallel"),
        name="mixnorm",
    )(x1, g.reshape(1, d), w_lr_t, wg2, bg2.reshape(1, -1))


def _projg_kernel(hn_ref, wt_ref, u_ref):
    u_ref[...] = _dot_nt(hn_ref[...], wt_ref[...].astype(BF16))


def _projg(hn, w_in_t, layer):
    n, d = hn.shape
    return pl.pallas_call(
        _projg_kernel,
        grid=(n // PROJ_TM, GLA_COLS // PROJ_TN),
        in_specs=[
            pl.BlockSpec((PROJ_TM, d), lambda i, j: (i, 0)),
            pl.BlockSpec((None, PROJ_TN, d), lambda i, j: (layer, j, 0)),
        ],
        out_specs=pl.BlockSpec((PROJ_TM, PROJ_TN), lambda i, j: (i, j)),
        out_shape=jax.ShapeDtypeStruct((n, GLA_COLS), F32),
        compiler_params=_params("parallel", "arbitrary"),
        name="projg",
    )(hn, w_in_t)


def _projq_kernel(hn_ref, wt_ref, g_ref, qt_ref):
    yt = _dot_nt(wt_ref[...].astype(BF16), hn_ref[...])
    g = g_ref[...]
    parts = []
    for k in range(PROJ_TN // MOBA_DH):
        p = yt[k * MOBA_DH:(k + 1) * MOBA_DH, :]
        inv = lax.rsqrt(jnp.mean(p * p, axis=0, keepdims=True) + EPS)
        parts.append(p * inv * g)
    qt_ref[...] = jnp.concatenate(parts, axis=0)


def _projk_kernel(hn_ref, wt_ref, g_ref, kb_ref, kmean_ref):
    y = _dot_nt(hn_ref[...], wt_ref[...].astype(BF16))
    g = g_ref[...]
    parts = [_rms(y[:, k * MOBA_DH:(k + 1) * MOBA_DH], g) for k in range(PROJ_TN // MOBA_DH)]
    kn = jnp.concatenate(parts, axis=1)
    kb_ref[...] = kn.astype(BF16)
    blocks = PROJ_TM // MOBA_BLOCK
    kmean_ref[0] = jnp.sum(kn.reshape(blocks, MOBA_BLOCK, PROJ_TN), axis=1) * (1.0 / MOBA_BLOCK)


def _projv_kernel(hn_ref, wt_ref, v3_ref):
    yt = _dot_nt(wt_ref[...].astype(BF16), hn_ref[...]).astype(BF16)
    ones = jnp.ones((MOBA_VPAD, PROJ_TM), BF16)
    parts = []
    for k in range(PROJ_TN // MOBA_DH):
        parts += [yt[k * MOBA_DH:(k + 1) * MOBA_DH], ones]
    ya = jnp.concatenate(parts, axis=0)
    for t in range(PROJ_TM // MOBA_BLOCK):
        v3_ref[t] = ya[:, t * MOBA_BLOCK:(t + 1) * MOBA_BLOCK]


def _proj_moba(hn, w_q_t, w_k_t, w_v_t, gq, gk):
    n, d = hn.shape
    grid = (n // PROJ_TM, MOBA_WIDTH // PROJ_TN)
    hn_spec = pl.BlockSpec((PROJ_TM, d), lambda i, j: (i, 0))
    wt_spec = pl.BlockSpec((PROJ_TN, d), lambda i, j: (j, 0))
    params = _params("parallel", "arbitrary")
    blocks = PROJ_TM // MOBA_BLOCK

    q_t = pl.pallas_call(
        _projq_kernel, grid=grid,
        in_specs=[hn_spec, wt_spec, pl.BlockSpec((MOBA_DH, 1), lambda i, j: (0, 0))],
        out_specs=pl.BlockSpec((PROJ_TN, PROJ_TM), lambda i, j: (j, i)),
        out_shape=jax.ShapeDtypeStruct((MOBA_WIDTH, n), F32),
        compiler_params=params, name="projq",
    )(hn, w_q_t, gq.reshape(-1, 1))

    kb, kmean = pl.pallas_call(
        _projk_kernel, grid=grid,
        in_specs=[hn_spec, wt_spec, pl.BlockSpec((1, MOBA_DH), lambda i, j: (0, 0))],
        out_specs=[pl.BlockSpec((PROJ_TM, PROJ_TN), lambda i, j: (i, j)),
                   pl.BlockSpec((1, blocks, PROJ_TN), lambda i, j: (i, 0, j))],
        out_shape=[jax.ShapeDtypeStruct((n, MOBA_WIDTH), BF16),
                   jax.ShapeDtypeStruct((n // PROJ_TM, blocks, MOBA_WIDTH), F32)],
        compiler_params=params, name="projk",
    )(hn, w_k_t, gk.reshape(1, -1))

    v3 = pl.pallas_call(
        _projv_kernel, grid=grid,
        in_specs=[hn_spec, wt_spec],
        out_specs=pl.BlockSpec((blocks, PROJ_TN // MOBA_DH * MOBA_VROWS, MOBA_BLOCK),
                               lambda i, j: (i, j, 0)),
        out_shape=jax.ShapeDtypeStruct((n // MOBA_BLOCK, MOBA_HEADS * MOBA_VROWS, MOBA_BLOCK), BF16),
        compiler_params=params, name="projv",
    )(hn, w_v_t)
    return q_t, kb, kmean.reshape(n // MOBA_BLOCK, MOBA_WIDTH), v3


def _dot_exact_rhs(m, a):
    a1 = a.astype(BF16)
    r1 = a - a1.astype(F32)
    a2 = r1.astype(BF16)
    a3 = (r1 - a2.astype(F32)).astype(BF16)
    return _dot(m, a1) + _dot(m, a2) + _dot(m, a3)


def _gla_kernel(q_ref, k_ref, v_ref, r_ref, la_ref, go_ref, o_ref, s_ref, upd_ref):
    t = pl.program_id(2)
    tc = GLA_TC
    nchunk = tc // GLA_CHUNK

    @pl.when(t == 0)
    def _():
        s_ref[...] = jnp.zeros_like(s_ref)

    row = lax.broadcasted_iota(jnp.int32, (tc, tc), 0)
    col = lax.broadcasted_iota(jnp.int32, (tc, tc), 1)
    causal = jnp.logical_and((row // GLA_CHUNK) == (col // GLA_CHUNK), col <= row)

    la = la_ref[...]
    crow = lax.broadcasted_iota(jnp.int32, (GLA_CHUNK, GLA_CHUNK), 0)
    ccol = lax.broadcasted_iota(jnp.int32, (GLA_CHUNK, GLA_CHUNK), 1)
    tri = (ccol <= crow).astype(BF16)
    la_wide = jnp.concatenate([la[c * GLA_CHUNK:(c + 1) * GLA_CHUNK] for c in range(nchunk)], axis=1)
    bc_wide = _dot_exact_rhs(tri, la_wide)
    bcum = jnp.concatenate([bc_wide[:, c * GLA_DK:(c + 1) * GLA_DK] for c in range(nchunk)], axis=0)
    btot = jnp.concatenate(
        [jnp.broadcast_to(bc_wide[GLA_CHUNK - 1:GLA_CHUNK, c * GLA_DK:(c + 1) * GLA_DK],
                          (GLA_CHUNK, GLA_DK)) for c in range(nchunk)], axis=0)

    k = k_ref[...]
    q_dec = (q_ref[...] * (GLA_DK ** -0.5)) * jnp.exp(bcum)
    k_inv = k * jnp.exp(-bcum)
    k_tail_t = (k * jnp.exp(btot - bcum)).T
    decay_t = jnp.exp(btot).T
    v = v_ref[...].astype(BF16)

    q_dec_b = q_dec.astype(BF16)
    att = jnp.where(causal, _dot_nt(q_dec_b, k_inv.astype(BF16)), 0.0)
    o = _dot(att.astype(BF16), v)

    lane_chunk = lax.broadcasted_iota(jnp.int32, (GLA_DK, tc), 1) // GLA_CHUNK
    for c in range(nchunk):
        kt_c = jnp.where(lane_chunk == c, k_tail_t, 0.0).astype(BF16)
        upd_ref[c] = _dot(kt_c, v)

    state = s_ref[...]
    inter = []
    for c in range(nchunk):
        rows = slice(c * GLA_CHUNK, (c + 1) * GLA_CHUNK)
        inter.append(_dot(q_dec_b[rows], state.astype(BF16)))
        dec_c = decay_t[:, c * GLA_CHUNK:c * GLA_CHUNK + 1]
        state = dec_c * state + upd_ref[c]
    s_ref[...] = state
    o = o + jnp.concatenate(inter, axis=0)

    o = _rms(o, go_ref[...]) * _silu(r_ref[...])
    o_ref[...] = o


def _gla(u, la, g_out, batch, seq):
    n = u.shape[0]
    nt = seq // GLA_TC
    tok = lambda b, h, t: b * nt + t
    k_blk0 = GLA_QK // GLA_DK
    v_blk0 = 2 * GLA_QK // GLA_DV
    r_blk0 = v_blk0 + GLA_WIDTH // GLA_DV
    return pl.pallas_call(
        _gla_kernel,
        grid=(batch, GLA_HEADS, nt),
        in_specs=[
            pl.BlockSpec((GLA_TC, GLA_DK), lambda b, h, t: (tok(b, h, t), h)),
            pl.BlockSpec((GLA_TC, GLA_DK), lambda b, h, t: (tok(b, h, t), k_blk0 + h)),
            pl.BlockSpec((GLA_TC, GLA_DV), lambda b, h, t: (tok(b, h, t), v_blk0 + h)),
            pl.BlockSpec((GLA_TC, GLA_DV), lambda b, h, t: (tok(b, h, t), r_blk0 + h)),
            pl.BlockSpec((GLA_TC, GLA_DK), lambda b, h, t: (tok(b, h, t), h)),
            pl.BlockSpec((1, GLA_DV), lambda b, h, t: (0, 0)),
        ],
        out_specs=pl.BlockSpec((GLA_TC, GLA_DV), lambda b, h, t: (tok(b, h, t), h)),
        out_shape=jax.ShapeDtypeStruct((n, GLA_WIDTH), F32),
        scratch_shapes=[pltpu.VMEM((GLA_DK, GLA_DV), F32),
                        pltpu.VMEM((GLA_TC // GLA_CHUNK, GLA_DK, GLA_DV), F32)],
        compiler_params=_params("parallel", "parallel", "arbitrary"),
        name="gla",
    )(u, u, u, u, la, g_out.reshape(1, -1))


def _moba_kernel(qt_ref, kb_ref, v3_ref, kmean_ref, o_ref, kaug_ref, snext_ref):
    ti = pl.program_id(2)
    last_tile = pl.num_programs(2) - 1
    bs = MOBA_BLOCK
    tq = MOBA_TQ
    nb = kb_ref.shape[0] // bs
    dh = MOBA_DH
    vr = MOBA_VROWS
    scale = dh ** -0.5

    @pl.when(ti == 0)
    def _():
        seq = kb_ref.shape[0]
        row_blk = lax.broadcasted_iota(jnp.int32, (seq, MXU_DIM - dh), 0) // bs
        lane = lax.broadcasted_iota(jnp.int32, (seq, MXU_DIM - dh), 1)
        onehot = (row_blk == lane).astype(BF16)
        for g in range(MOBA_G):
            kaug_ref[g, :, :dh] = kb_ref[:, g * dh:(g + 1) * dh]
            kaug_ref[g, :, dh:] = onehot

    blk = lax.broadcasted_iota(jnp.int32, (nb, tq), 0)
    q_blk = 2 * ti + (lax.broadcasted_iota(jnp.int32, (nb, tq), 1) // bs)
    past = blk < q_blk
    key_row = lax.broadcasted_iota(jnp.int32, (tq, tq), 0)
    qry_col = lax.broadcasted_iota(jnp.int32, (tq, tq), 1)
    causal = key_row <= qry_col
    same_blk = (key_row // bs) == (qry_col // bs)
    diag = pl.multiple_of(ti * tq, tq)

    def pv_dot(g, t, p):
        return (_dot(v3_ref[2 * t, g * vr:(g + 1) * vr, :], p[:bs])
                + _dot(v3_ref[2 * t + 1, g * vr:(g + 1) * vr, :], p[bs:]))

    q_aug, state = [], []
    for g in range(MOBA_G):
        qt = qt_ref[g * dh:(g + 1) * dh, :]
        gate = jnp.dot(kmean_ref[:, g * dh:(g + 1) * dh], qt,
                       precision=lax.Precision.HIGHEST, preferred_element_type=F32)
        gate = jnp.where(past, gate, NEG_INF)
        rank = jnp.zeros((nb, tq), jnp.int32)
        for j in range(nb):
            gj = gate[j:j + 1, :]
            ahead = jnp.logical_or(gj > gate, jnp.logical_and(gj == gate, blk > j))
            rank = rank + ahead.astype(jnp.int32)
        chosen = jnp.logical_and(past, rank < MOBA_TOPK)
        bias = jnp.where(chosen, 0.0, NEG_INF)
        qa = jnp.concatenate([(qt * scale).astype(BF16), bias.astype(BF16),
                              jnp.zeros((MXU_DIM - dh - nb, tq), BF16)], axis=0)
        q_aug.append(qa)

        first_chosen = jnp.sum(jnp.where(blk == 2 * ti, chosen.astype(F32), 0.0),
                               axis=0, keepdims=True) > 0.5
        allowed = jnp.logical_and(causal, jnp.logical_or(same_blk, first_chosen))
        s = _dot(kaug_ref[g, pl.ds(diag, tq), :dh], qa[:dh])
        s = jnp.where(allowed, s, NEG_INF)
        m = jnp.max(s, axis=0, keepdims=True)
        p = jnp.exp(s - m).astype(BF16)
        state.append((m, pv_dot(g, ti, p)))

    def score(g, t):
        off = pl.multiple_of(t * tq, tq)
        s = _dot(kaug_ref[g, pl.ds(off, tq), :], q_aug[g])
        return s, jnp.max(s, axis=0, keepdims=True)

    def consume(g, t, s, tile_max, m, acc, valid=None):
        if valid is not None:
            tile_max = jnp.where(valid, tile_max, NEG_INF)
        m_new = jnp.maximum(m, tile_max)
        alpha = jnp.exp(m - m_new)
        shift = m_new if valid is None else jnp.where(valid, m_new, -NEG_INF)
        p = jnp.exp(s - shift).astype(BF16)
        return m_new, alpha * acc + pv_dot(g, t, p)

    carry0 = []
    for g in range(MOBA_G):
        s, tile_max = score(g, 0)
        snext_ref[g] = s
        carry0.append(state[g] + (tile_max,))

    def body(u, carry):
        out = []
        for g in range(MOBA_G):
            m, acc, max_a = carry[g]
            s_b, max_b = score(g, 2 * u + 1)
            m, acc = consume(g, 2 * u, snext_ref[g], max_a, m, acc)
            s_c, max_c = score(g, jnp.minimum(2 * u + 2, last_tile))
            snext_ref[g] = s_c
            m, acc = consume(g, 2 * u + 1, s_b, max_b, m, acc, valid=2 * u + 1 < ti)
            out.append((m, acc, max_c))
        return tuple(out)

    carry = lax.fori_loop(0, (ti + 1) // 2, body, tuple(carry0))
    for g in range(MOBA_G):
        m, acc, _ = carry[g]
        o_ref[:, g * dh:(g + 1) * dh] = (acc[:dh] * (1.0 / acc[dh:dh + 1])).T


def _moba(q_t, kb, kmean, v3, batch, seq):
    n = kb.shape[0]
    nb = seq // MOBA_BLOCK
    nt = seq // MOBA_TQ
    gw = MOBA_G * MOBA_DH
    return pl.pallas_call(
        _moba_kernel,
        grid=(batch, MOBA_HEADS // MOBA_G, nt),
        in_specs=[
            pl.BlockSpec((gw, MOBA_TQ), lambda b, h, i: (h, b * nt + i)),
            pl.BlockSpec((seq, gw), lambda b, h, i: (b, h)),
            pl.BlockSpec((nb, MOBA_G * MOBA_VROWS, MOBA_BLOCK), lambda b, h, i: (b, h, 0)),
            pl.BlockSpec((nb, gw), lambda b, h, i: (b, h)),
        ],
        out_specs=pl.BlockSpec((MOBA_TQ, gw), lambda b, h, i: (b * nt + i, h)),
        out_shape=jax.ShapeDtypeStruct((n, MOBA_WIDTH), F32),
        scratch_shapes=[pltpu.VMEM((MOBA_G, seq, MXU_DIM), BF16),
                        pltpu.VMEM((MOBA_G, MOBA_TQ, MOBA_TQ), F32)],
        compiler_params=_params("parallel", "parallel", "arbitrary"),
        name="moba",
    )(q_t, kb, v3, kmean)


def _memkv_kernel(mem_ref, g_ref, wkv_ref, gk_ref, k_ref, v_ref):
    h = _rms(mem_ref[...], g_ref[...]).astype(BF16)
    kv = _dot(h, wkv_ref[...])
    gk = gk_ref[...]
    ks = [_rms(kv[:, i * XATTN_DH:(i + 1) * XATTN_DH], gk) for i in range(XATTN_HEADS)]
    k_ref[...] = jnp.concatenate(ks, axis=1)
    v_ref[...] = kv[:, XATTN_WIDTH:]


def _memkv(mem, g, wkv_b, gk):
    m, d = mem.shape
    return pl.pallas_call(
        _memkv_kernel,
        out_shape=[jax.ShapeDtypeStruct((m, XATTN_WIDTH), F32),
                   jax.ShapeDtypeStruct((m, XATTN_WIDTH), F32)],
        compiler_params=pltpu.CompilerParams(vmem_limit_bytes=VMEM_LIMIT),
        name="memkv",
    )(mem, g.reshape(1, d), wkv_b, gk.reshape(1, -1))


def _outx_kernel(x1_ref, og_ref, om_ref, wout_ref, gx_ref, wq_ref, gq_ref, kx_ref, vx_ref, wo_ref,
                 o_ref):
    x2 = (x1_ref[...]
          + _dot(og_ref[...].astype(BF16), wout_ref[:GLA_WIDTH, :])
          + _dot(om_ref[...].astype(BF16), wout_ref[GLA_WIDTH:, :]))
    h = _rms(x2, gx_ref[...]).astype(BF16)
    q = _dot(h, wq_ref[...])
    gq = gq_ref[...]
    scale = XATTN_DH ** -0.5
    heads = []
    for i in range(XATTN_HEADS):
        cols = slice(i * XATTN_DH, (i + 1) * XATTN_DH)
        qh = _rms(q[:, cols], gq).astype(BF16)
        s = _dot_nt(qh, kx_ref[:, cols].astype(BF16)) * scale
        p = jnp.exp(s - jnp.max(s, axis=-1, keepdims=True))
        p = p * (1.0 / jnp.sum(p, axis=-1, keepdims=True))
        heads.append(_dot(p.astype(BF16), vx_ref[:, cols].astype(BF16)))
    o = jnp.concatenate(heads, axis=1).astype(BF16)
    o_ref[...] = x2 + _dot(o, wo_ref[...])


def _outx(x1, og, om, wout_b, gx, wq_b, gq, kx, vx, wo_b, seq, mem_len):
    n, d = x1.shape
    tiles_per_seq = seq // OUTX_TM
    const = lambda i: (0, 0)
    return pl.pallas_call(
        _outx_kernel,
        grid=(n // OUTX_TM,),
        in_specs=[
            pl.BlockSpec((OUTX_TM, d), lambda i: (i, 0)),
            pl.BlockSpec((OUTX_TM, GLA_WIDTH), lambda i: (i, 0)),
            pl.BlockSpec((OUTX_TM, MOBA_WIDTH), lambda i: (i, 0)),
            pl.BlockSpec((GLA_WIDTH + MOBA_WIDTH, d), const),
            pl.BlockSpec((1, d), const),
            pl.BlockSpec((d, XATTN_WIDTH), const),
            pl.BlockSpec((1, XATTN_DH), const),
            pl.BlockSpec((mem_len, XATTN_WIDTH), lambda i: (i // tiles_per_seq, 0)),
            pl.BlockSpec((mem_len, XATTN_WIDTH), lambda i: (i // tiles_per_seq, 0)),
            pl.BlockSpec((XATTN_WIDTH, d), const),
        ],
        out_specs=pl.BlockSpec((OUTX_TM, d), lambda i: (i, 0)),
        out_shape=jax.ShapeDtypeStruct((n, d), F32),
        compiler_params=_params("parallel"),
        name="outx",
    )(x1, og, om, wout_b, gx.reshape(1, d), wq_b, gq.reshape(1, -1), kx, vx, wo_b)


def kernel(x, mem, ffn1_norm, ffn1_w_gate, ffn1_w_up, ffn1_w_down, mix_norm, w_in, gla_w_gate2, gla_b_gate2, gla_out_norm, moba_q_norm, moba_k_norm, w_out, xattn_norm, mem_norm, xattn_w_q, xattn_w_kv, xattn_w_o, xattn_q_norm, xattn_k_norm, ffn2_norm, ffn2_w_gate, ffn2_w_up, ffn2_w_down):
    batch, seq, d = x.shape
    mem_len = mem.shape[1]
    depth = ffn1_norm.shape[0]
    n = batch * seq
    assert seq % GLA_TC == 0 and seq % (2 * MOBA_BLOCK) == 0 and seq % OUTX_TM == 0
    assert seq // MOBA_BLOCK + MOBA_DH <= MXU_DIM and seq // MOBA_BLOCK == 16
    assert n % FFN_TM == 0 and n % NORM_TM == 0 and n % PROJ_TM == 0 and PROJ_TM % MOBA_BLOCK == 0

    xf = x.reshape(n, d)
    memf = mem.reshape(batch * mem_len, d)
    lr0 = GLA_COLS
    mq0 = lr0 + GLA_GATE_RANK
    w_in_t = jnp.swapaxes(w_in, 1, 2)
    for l in range(depth):
        w_lr_t = w_in_t[l, lr0:mq0]
        w_mq_t = w_in_t[l, mq0:mq0 + MOBA_WIDTH]
        w_mk_t = w_in_t[l, mq0 + MOBA_WIDTH:mq0 + 2 * MOBA_WIDTH]
        w_mv_t = w_in_t[l, mq0 + 2 * MOBA_WIDTH:]

        x1 = _ffn(xf, ffn1_norm[l], ffn1_w_gate[l], ffn1_w_up[l], ffn1_w_down[l])
        hn, la = _mixnorm(x1, mix_norm[l], w_lr_t, gla_w_gate2[l], gla_b_gate2[l])
        u = _projg(hn, w_in_t, l)
        q_t, kb, kmean, v3 = _proj_moba(hn, w_mq_t, w_mk_t, w_mv_t, moba_q_norm[l], moba_k_norm[l])
        o_gla = _gla(u, la, gla_out_norm[l], batch, seq)
        o_moba = _moba(q_t, kb, kmean, v3, batch, seq)
        kx, vx = _memkv(memf, mem_norm[l], xattn_w_kv[l].astype(BF16), xattn_k_norm[l])
        x3 = _outx(x1, o_gla, o_moba, w_out[l].astype(BF16), xattn_norm[l],
                   xattn_w_q[l].astype(BF16), xattn_q_norm[l], kx, vx,
                   xattn_w_o[l].astype(BF16), seq, mem_len)
        xf = _ffn(x3, ffn2_norm[l], ffn2_w_gate[l], ffn2_w_up[l], ffn2_w_down[l])
    return xf.reshape(batch, seq, d)
```

```python
import jax
import jax.numpy as jnp
from jax import lax
from jax.experimental import pallas as pl
from jax.experimental.pallas import tpu as pltpu

F32 = jnp.float32
BF16 = jnp.bfloat16

EPS = 1e-6
NEG_INF = -1e30
D_FF = 5632
GLA_HEADS = 4
GLA_DV = 256
GLA_DK = 128
GLA_QK = GLA_HEADS * GLA_DK
GLA_WIDTH = GLA_HEADS * GLA_DV
GLA_COLS = 2 * GLA_QK + 2 * GLA_WIDTH
GLA_GATE_RANK = 16
GLA_GATE_TAU = 16.0
GLA_CHUNK = 64
MOBA_DH = 128
MOBA_HEADS = 8
MOBA_WIDTH = MOBA_HEADS * MOBA_DH
MOBA_BLOCK = 256
MOBA_TOPK = 3
XATTN_HEADS = 4
XATTN_DH = 128
XATTN_WIDTH = XATTN_HEADS * XATTN_DH

V7X_VMEM_BYTES = 64 * 1024 * 1024
VMEM_LIMIT = V7X_VMEM_BYTES - 6 * 1024 * 1024
MXU_DIM = 256

FFN_TM = 1024
FFN_TF = 512
NORM_TM = 1024
PROJ_TM = 2048
PROJ_TN = 512
GLA_TC = 512
GLA_SUB = 256
GLA_G = 4
MOBA_G = 4
MOBA_TQ = 2 * MOBA_BLOCK
MOBA_VPAD = 16
MOBA_VROWS = MOBA_DH + MOBA_VPAD
OUTX_TM = 512


def _params(*semantics):
    return pltpu.CompilerParams(dimension_semantics=semantics, vmem_limit_bytes=VMEM_LIMIT)


def _rms(x, g):
    return x * lax.rsqrt(jnp.mean(x * x, axis=-1, keepdims=True) + EPS) * g


def _dot(a, b):
    return jnp.dot(a, b, preferred_element_type=F32)


def _dot_nt(a, b):
    return lax.dot_general(a, b, (((1,), (1,)), ((), ())), preferred_element_type=F32)


def _silu(x):
    return x * (1.0 / (1.0 + jnp.exp(-x)))


def _ffn_kernel(x_hbm, g_ref, wg_ref, wu_ref, wd_ref, o_hbm, h_ref, xbuf_ref, acc_ref, xsem, osem):
    i = pl.program_id(0)
    f = pl.program_id(1)
    last_i = pl.num_programs(0) - 1
    last_f = pl.num_programs(1) - 1

    def x_copy(tile):
        return pltpu.make_async_copy(x_hbm.at[pl.ds(tile * FFN_TM, FFN_TM)], xbuf_ref, xsem)

    def o_copy(tile):
        return pltpu.make_async_copy(acc_ref, o_hbm.at[pl.ds(tile * FFN_TM, FFN_TM)], osem)

    @pl.when(jnp.logical_and(i == 0, f == 0))
    def _():
        x_copy(0).start()

    @pl.when(f == 0)
    def _():
        x_copy(i).wait()
        x = xbuf_ref[...]
        h_ref[...] = _rms(x, g_ref[...]).astype(BF16)

        @pl.when(i > 0)
        def _():
            o_copy(i - 1).wait()

        acc_ref[...] = 2.0 * xbuf_ref[...]

    @pl.when(jnp.logical_and(f == 1, i < last_i))
    def _():
        x_copy(i + 1).start()

    h = h_ref[...]
    gate = _dot(h, wg_ref[...].astype(BF16))
    up = _dot(h, wu_ref[...].astype(BF16))
    act = (_silu(gate) * up).astype(BF16)
    acc_ref[...] += _dot(act, wd_ref[...].astype(BF16))

    @pl.when(f == last_f)
    def _():
        acc_ref[...] = 0.5 * acc_ref[...]
        o_copy(i).start()

        @pl.when(i == last_i)
        def _():
            o_copy(i).wait()


def _ffn(x, g, wg, wu, wd):
    n, d = x.shape
    grid = (n // FFN_TM, D_FF // FFN_TF)
    assert grid[1] >= 2
    return pl.pallas_call(
        _ffn_kernel,
        grid=grid,
        in_specs=[
            pl.BlockSpec(memory_space=pl.ANY),
            pl.BlockSpec((1, d), lambda i, f: (0, 0)),
            pl.BlockSpec((d, FFN_TF), lambda i, f: (0, f)),
            pl.BlockSpec((d, FFN_TF), lambda i, f: (0, f)),
            pl.BlockSpec((FFN_TF, d), lambda i, f: (f, 0)),
        ],
        out_specs=pl.BlockSpec(memory_space=pl.ANY),
        out_shape=jax.ShapeDtypeStruct((n, d), F32),
        scratch_shapes=[pltpu.VMEM((FFN_TM, d), BF16), pltpu.VMEM((FFN_TM, d), F32),
                        pltpu.VMEM((FFN_TM, d), F32),
                        pltpu.SemaphoreType.DMA, pltpu.SemaphoreType.DMA],
        compiler_params=_params("arbitrary", "arbitrary"),
        name="ffn",
    )(x, g.reshape(1, d), wg, wu, wd)


def _mixnorm_kernel(x_ref, g_ref, wlr_ref, wg2_ref, bg2_ref, hn_ref, la_ref):
    h = _rms(x_ref[...], g_ref[...]).astype(BF16)
    hn_ref[...] = h
    lr = _dot_nt(h, wlr_ref[...].astype(BF16))
    pre = _dot(lr.astype(BF16), wg2_ref[...].astype(BF16)) + bg2_ref[...]
    log_sig = -(jnp.maximum(-pre, 0.0) + jnp.log(1.0 + jnp.exp(-jnp.abs(pre))))
    la_ref[...] = log_sig * (1.0 / GLA_GATE_TAU)


def _mixnorm(x1, g, w_lr_t, wg2, bg2):
    n, d = x1.shape
    const = lambda i: (0, 0)
    return pl.pallas_call(
        _mixnorm_kernel,
        grid=(n // NORM_TM,),
        in_specs=[
            pl.BlockSpec((NORM_TM, d), lambda i: (i, 0)),
            pl.BlockSpec((1, d), const),
            pl.BlockSpec((GLA_GATE_RANK, d), const),
            pl.BlockSpec((GLA_GATE_RANK, GLA_QK), const),
            pl.BlockSpec((1, GLA_QK), const),
        ],
        out_specs=[pl.BlockSpec((NORM_TM, d), lambda i: (i, 0)),
                   pl.BlockSpec((NORM_TM, GLA_QK), lambda i: (i, 0))],
        out_shape=[jax.ShapeDtypeStruct((n, d), BF16),
                   jax.ShapeDtypeStruct((n, GLA_QK), F32)],
        compiler_params=_params("parallel"),
        name="mixnorm",
    )(x1, g.reshape(1, d), w_lr_t, wg2, bg2.reshape(1, -1))


def _projg_kernel(hn_ref, wt_ref, u_ref):
    u_ref[...] = _dot_nt(hn_ref[...], wt_ref[...].astype(BF16))


def _projg(hn, w_in_t, layer):
    n, d = hn.shape
    return pl.pallas_call(
        _projg_kernel,
        grid=(n // PROJ_TM, GLA_COLS // PROJ_TN),
        in_specs=[
            pl.BlockSpec((PROJ_TM, d), lambda i, j: (i, 0)),
            pl.BlockSpec((None, PROJ_TN, d), lambda i, j: (layer, j, 0)),
        ],
        out_specs=pl.BlockSpec((PROJ_TM, PROJ_TN), lambda i, j: (i, j)),
        out_shape=jax.ShapeDtypeStruct((n, GLA_COLS), F32),
        compiler_params=_params("parallel", "arbitrary"),
        name="projg",
    )(hn, w_in_t)


def _projq_kernel(hn_ref, wt_ref, g_ref, qt_ref):
    yt = _dot_nt(wt_ref[...].astype(BF16), hn_ref[...])
    g = g_ref[...]
    parts = []
    for k in range(PROJ_TN // MOBA_DH):
        p = yt[k * MOBA_DH:(k + 1) * MOBA_DH, :]
        inv = lax.rsqrt(jnp.mean(p * p, axis=0, keepdims=True) + EPS)
        parts.append(p * inv * g)
    qt_ref[...] = jnp.concatenate(parts, axis=0)


def _projk_kernel(hn_ref, wt_ref, g_ref, kb_ref, kmean_ref):
    y = _dot_nt(hn_ref[...], wt_ref[...].astype(BF16))
    g = g_ref[...]
    parts = [_rms(y[:, k * MOBA_DH:(k + 1) * MOBA_DH], g) for k in range(PROJ_TN // MOBA_DH)]
    kn = jnp.concatenate(parts, axis=1)
    kb_ref[...] = kn.astype(BF16)
    blocks = PROJ_TM // MOBA_BLOCK
    kmean_ref[0] = jnp.sum(kn.reshape(blocks, MOBA_BLOCK, PROJ_TN), axis=1) * (1.0 / MOBA_BLOCK)


def _projv_kernel(hn_ref, wt_ref, v3_ref):
    yt = _dot_nt(wt_ref[...].astype(BF16), hn_ref[...]).astype(BF16)
    ones = jnp.ones((MOBA_VPAD, PROJ_TM), BF16)
    parts = []
    for k in range(PROJ_TN // MOBA_DH):
        parts += [yt[k * MOBA_DH:(k + 1) * MOBA_DH], ones]
    ya = jnp.concatenate(parts, axis=0)
    for t in range(PROJ_TM // MOBA_BLOCK):
        v3_ref[t] = ya[:, t * MOBA_BLOCK:(t + 1) * MOBA_BLOCK]


def _proj_moba(hn, w_q_t, w_k_t, w_v_t, gq, gk):
    n, d = hn.shape
    grid = (n // PROJ_TM, MOBA_WIDTH // PROJ_TN)
    hn_spec = pl.BlockSpec((PROJ_TM, d), lambda i, j: (i, 0))
    wt_spec = pl.BlockSpec((PROJ_TN, d), lambda i, j: (j, 0))
    params = _params("parallel", "arbitrary")
    blocks = PROJ_TM // MOBA_BLOCK

    q_t = pl.pallas_call(
        _projq_kernel, grid=grid,
        in_specs=[hn_spec, wt_spec, pl.BlockSpec((MOBA_DH, 1), lambda i, j: (0, 0))],
        out_specs=pl.BlockSpec((PROJ_TN, PROJ_TM), lambda i, j: (j, i)),
        out_shape=jax.ShapeDtypeStruct((MOBA_WIDTH, n), F32),
        compiler_params=params, name="projq",
    )(hn, w_q_t, gq.reshape(-1, 1))

    kb, kmean = pl.pallas_call(
        _projk_kernel, grid=grid,
        in_specs=[hn_spec, wt_spec, pl.BlockSpec((1, MOBA_DH), lambda i, j: (0, 0))],
        out_specs=[pl.BlockSpec((PROJ_TM, PROJ_TN), lambda i, j: (i, j)),
                   pl.BlockSpec((1, blocks, PROJ_TN), lambda i, j: (i, 0, j))],
        out_shape=[jax.ShapeDtypeStruct((n, MOBA_WIDTH), BF16),
                   jax.ShapeDtypeStruct((n // PROJ_TM, blocks, MOBA_WIDTH), F32)],
        compiler_params=params, name="projk",
    )(hn, w_k_t, gk.reshape(1, -1))

    v3 = pl.pallas_call(
        _projv_kernel, grid=grid,
        in_specs=[hn_spec, wt_spec],
        out_specs=pl.BlockSpec((blocks, PROJ_TN // MOBA_DH * MOBA_VROWS, MOBA_BLOCK),
                               lambda i, j: (i, j, 0)),
        out_shape=jax.ShapeDtypeStruct((n // MOBA_BLOCK, MOBA_HEADS * MOBA_VROWS, MOBA_BLOCK), BF16),
        compiler_params=params, name="projv",
    )(hn, w_v_t)
    return q_t, kb, kmean.reshape(n // MOBA_BLOCK, MOBA_WIDTH), v3


def _dot_exact_rhs(m, a):
    a1 = a.astype(BF16)
    r1 = a - a1.astype(F32)
    a2 = r1.astype(BF16)
    a3 = (r1 - a2.astype(F32)).astype(BF16)
    return _dot(m, a1) + _dot(m, a2) + _dot(m, a3)


def _gla_head(q, k, v, r, la, g_out, state, upd_ref):
    tc = GLA_TC
    nchunk = tc // GLA_CHUNK
    sub = GLA_SUB

    crow = lax.broadcasted_iota(jnp.int32, (GLA_CHUNK, GLA_CHUNK), 0)
    ccol = lax.broadcasted_iota(jnp.int32, (GLA_CHUNK, GLA_CHUNK), 1)
    tri = (ccol <= crow).astype(BF16)
    la_wide = jnp.concatenate([la[c * GLA_CHUNK:(c + 1) * GLA_CHUNK] for c in range(nchunk)], axis=1)
    bc_wide = _dot_exact_rhs(tri, la_wide)
    bcum = jnp.concatenate([bc_wide[:, c * GLA_DK:(c + 1) * GLA_DK] for c in range(nchunk)], axis=0)
    btot = jnp.concatenate(
        [jnp.broadcast_to(bc_wide[GLA_CHUNK - 1:GLA_CHUNK, c * GLA_DK:(c + 1) * GLA_DK],
                          (GLA_CHUNK, GLA_DK)) for c in range(nchunk)], axis=0)

    q_dec_b = ((q * (GLA_DK ** -0.5)) * jnp.exp(bcum)).astype(BF16)
    k_inv_b = (k * jnp.exp(-bcum)).astype(BF16)
    k_tail_t = (k * jnp.exp(btot - bcum)).T.astype(BF16)
    decay_t = jnp.exp(btot).T
    vb = v.astype(BF16)

    row = lax.broadcasted_iota(jnp.int32, (sub, sub), 0)
    col = lax.broadcasted_iota(jnp.int32, (sub, sub), 1)
    causal = jnp.logical_and((row // GLA_CHUNK) == (col // GLA_CHUNK), col <= row)
    intra = []
    for p in range(tc // sub):
        rows = slice(p * sub, (p + 1) * sub)
        att = jnp.where(causal, _dot_nt(q_dec_b[rows], k_inv_b[rows]), 0.0)
        intra.append(_dot(att.astype(BF16), vb[rows]))

    for c in range(nchunk):
        rows = slice(c * GLA_CHUNK, (c + 1) * GLA_CHUNK)
        upd_ref[c] = _dot(k_tail_t[:, rows], vb[rows])

    inter = []
    for c in range(nchunk):
        rows = slice(c * GLA_CHUNK, (c + 1) * GLA_CHUNK)
        inter.append(_dot(q_dec_b[rows], state.astype(BF16)))
        dec_c = decay_t[:, c * GLA_CHUNK:c * GLA_CHUNK + 1]
        state = dec_c * state + upd_ref[c]
    o = jnp.concatenate(intra, axis=0) + jnp.concatenate(inter, axis=0)
    return _rms(o, g_out) * _silu(r), state


def _gla_kernel(q_ref, k_ref, v_ref, r_ref, la_ref, go_ref, o_ref, s_ref, upd_ref):
    @pl.when(pl.program_id(2) == 0)
    def _():
        s_ref[...] = jnp.zeros_like(s_ref)

    for g in range(GLA_G):
        qk = slice(g * GLA_DK, (g + 1) * GLA_DK)
        vr = slice(g * GLA_DV, (g + 1) * GLA_DV)
        o, state = _gla_head(q_ref[:, qk], k_ref[:, qk], v_ref[:, vr], r_ref[:, vr], la_ref[:, qk],
                             go_ref[...], s_ref[g], upd_ref.at[g])
        s_ref[g] = state
        o_ref[:, vr] = o


def _gla(u, la, g_out, batch, seq):
    n = u.shape[0]
    nt = seq // GLA_TC
    tok = lambda b, h, t: b * nt + t
    qk_w, v_w = GLA_G * GLA_DK, GLA_G * GLA_DV
    k_blk0 = GLA_QK // qk_w
    v_blk0 = 2 * GLA_QK // v_w
    r_blk0 = v_blk0 + GLA_WIDTH // v_w
    return pl.pallas_call(
        _gla_kernel,
        grid=(batch, GLA_HEADS // GLA_G, nt),
        in_specs=[
            pl.BlockSpec((GLA_TC, qk_w), lambda b, h, t: (tok(b, h, t), h)),
            pl.BlockSpec((GLA_TC, qk_w), lambda b, h, t: (tok(b, h, t), k_blk0 + h)),
            pl.BlockSpec((GLA_TC, v_w), lambda b, h, t: (tok(b, h, t), v_blk0 + h)),
            pl.BlockSpec((GLA_TC, v_w), lambda b, h, t: (tok(b, h, t), r_blk0 + h)),
            pl.BlockSpec((GLA_TC, qk_w), lambda b, h, t: (tok(b, h, t), h)),
            pl.BlockSpec((1, GLA_DV), lambda b, h, t: (0, 0)),
        ],
        out_specs=pl.BlockSpec((GLA_TC, v_w), lambda b, h, t: (tok(b, h, t), h)),
        out_shape=jax.ShapeDtypeStruct((n, GLA_WIDTH), F32),
        scratch_shapes=[pltpu.VMEM((GLA_G, GLA_DK, GLA_DV), F32),
                        pltpu.VMEM((GLA_G, GLA_TC // GLA_CHUNK, GLA_DK, GLA_DV), F32)],
        compiler_params=_params("parallel", "parallel", "arbitrary"),
        name="gla",
    )(u, u, u, u, la, g_out.reshape(1, -1))


def _moba_kernel(qt_ref, kb_ref, v3_ref, kmean_ref, o_ref, kaug_ref, snext_ref):
    ti = pl.program_id(2)
    last_tile = pl.num_programs(2) - 1
    bs = MOBA_BLOCK
    tq = MOBA_TQ
    nb = kb_ref.shape[0] // bs
    dh = MOBA_DH
    vr = MOBA_VROWS
    scale = dh ** -0.5

    @pl.when(ti == 0)
    def _():
        seq = kb_ref.shape[0]
        row_blk = lax.broadcasted_iota(jnp.int32, (seq, MXU_DIM - dh), 0) // bs
        lane = lax.broadcasted_iota(jnp.int32, (seq, MXU_DIM - dh), 1)
        onehot = (row_blk == lane).astype(BF16)
        for g in range(MOBA_G):
            kaug_ref[g, :, :dh] = kb_ref[:, g * dh:(g + 1) * dh]
            kaug_ref[g, :, dh:] = onehot

    blk = lax.broadcasted_iota(jnp.int32, (nb, tq), 0)
    q_blk = 2 * ti + (lax.broadcasted_iota(jnp.int32, (nb, tq), 1) // bs)
    past = blk < q_blk
    key_row = lax.broadcasted_iota(jnp.int32, (tq, tq), 0)
    qry_col = lax.broadcasted_iota(jnp.int32, (tq, tq), 1)
    causal = key_row <= qry_col
    same_blk = (key_row // bs) == (qry_col // bs)
    diag = pl.multiple_of(ti * tq, tq)

    def pv_dot(g, t, p):
        return (_dot(v3_ref[2 * t, g * vr:(g + 1) * vr, :], p[:bs])
                + _dot(v3_ref[2 * t + 1, g * vr:(g + 1) * vr, :], p[bs:]))

    q_aug, state = [], []
    for g in range(MOBA_G):
        qt = qt_ref[g * dh:(g + 1) * dh, :]
        gate = jnp.dot(kmean_ref[:, g * dh:(g + 1) * dh], qt,
                       precision=lax.Precision.HIGHEST, preferred_element_type=F32)
        gate = jnp.where(past, gate, NEG_INF)
        rank = jnp.zeros((nb, tq), jnp.int32)
        for j in range(nb):
            gj = gate[j:j + 1, :]
            ahead = jnp.logical_or(gj > gate, jnp.logical_and(gj == gate, blk > j))
            rank = rank + ahead.astype(jnp.int32)
        chosen = jnp.logical_and(past, rank < MOBA_TOPK)
        bias = jnp.where(chosen, 0.0, NEG_INF)
        qa = jnp.concatenate([(qt * scale).astype(BF16), bias.astype(BF16),
                              jnp.zeros((MXU_DIM - dh - nb, tq), BF16)], axis=0)
        q_aug.append(qa)

        first_chosen = jnp.sum(jnp.where(blk == 2 * ti, chosen.astype(F32), 0.0),
                               axis=0, keepdims=True) > 0.5
        allowed = jnp.logical_and(causal, jnp.logical_or(same_blk, first_chosen))
        s = _dot(kaug_ref[g, pl.ds(diag, tq), :dh], qa[:dh])
        s = jnp.where(allowed, s, NEG_INF)
        m = jnp.max(s, axis=0, keepdims=True)
        p = jnp.exp(s - m).astype(BF16)
        state.append((m, pv_dot(g, ti, p)))

    def score(g, t):
        off = pl.multiple_of(t * tq, tq)
        s = _dot(kaug_ref[g, pl.ds(off, tq), :], q_aug[g])
        return s, jnp.max(s, axis=0, keepdims=True)

    def consume(g, t, s, tile_max, m, acc, valid=None):
        if valid is not None:
            tile_max = jnp.where(valid, tile_max, NEG_INF)
        m_new = jnp.maximum(m, tile_max)
        alpha = jnp.exp(m - m_new)
        shift = m_new if valid is None else jnp.where(valid, m_new, -NEG_INF)
        p = jnp.exp(s - shift).astype(BF16)
        return m_new, alpha * acc + pv_dot(g, t, p)

    carry0 = []
    for g in range(MOBA_G):
        s, tile_max = score(g, 0)
        snext_ref[g] = s
        carry0.append(state[g] + (tile_max,))

    def body(u, carry):
        out = []
        for g in range(MOBA_G):
            m, acc, max_a = carry[g]
            s_b, max_b = score(g, 2 * u + 1)
            m, acc = consume(g, 2 * u, snext_ref[g], max_a, m, acc)
            s_c, max_c = score(g, jnp.minimum(2 * u + 2, last_tile))
            snext_ref[g] = s_c
            m, acc = consume(g, 2 * u + 1, s_b, max_b, m, acc, valid=2 * u + 1 < ti)
            out.append((m, acc, max_c))
        return tuple(out)

    carry = lax.fori_loop(0, (ti + 1) // 2, body, tuple(carry0))
    for g in range(MOBA_G):
        m, acc, _ = carry[g]
        o_ref[:, g * dh:(g + 1) * dh] = (acc[:dh] * (1.0 / acc[dh:dh + 1])).T


def _moba(q_t, kb, kmean, v3, batch, seq):
    n = kb.shape[0]
    nb = seq // MOBA_BLOCK
    nt = seq // MOBA_TQ
    gw = MOBA_G * MOBA_DH
    return pl.pallas_call(
        _moba_kernel,
        grid=(batch, MOBA_HEADS // MOBA_G, nt),
        in_specs=[
            pl.BlockSpec((gw, MOBA_TQ), lambda b, h, i: (h, b * nt + i)),
            pl.BlockSpec((seq, gw), lambda b, h, i: (b, h)),
            pl.BlockSpec((nb, MOBA_G * MOBA_VROWS, MOBA_BLOCK), lambda b, h, i: (b, h, 0)),
            pl.BlockSpec((nb, gw), lambda b, h, i: (b, h)),
        ],
        out_specs=pl.BlockSpec((MOBA_TQ, gw), lambda b, h, i: (b * nt + i, h)),
        out_shape=jax.ShapeDtypeStruct((n, MOBA_WIDTH), F32),
        scratch_shapes=[pltpu.VMEM((MOBA_G, seq, MXU_DIM), BF16),
                        pltpu.VMEM((MOBA_G, MOBA_TQ, MOBA_TQ), F32)],
        compiler_params=_params("parallel", "parallel", "arbitrary"),
        name="moba",
    )(q_t, kb, v3, kmean)


def _memkv_kernel(mem_ref, g_ref, wkv_ref, gk_ref, k_ref, v_ref):
    h = _rms(mem_ref[...], g_ref[...]).astype(BF16)
    kv = _dot(h, wkv_ref[...])
    gk = gk_ref[...]
    ks = [_rms(kv[:, i * XATTN_DH:(i + 1) * XATTN_DH], gk) for i in range(XATTN_HEADS)]
    k_ref[...] = jnp.concatenate(ks, axis=1)
    v_ref[...] = kv[:, XATTN_WIDTH:]


def _memkv(mem, g, wkv_b, gk):
    m, d = mem.shape
    return pl.pallas_call(
        _memkv_kernel,
        out_shape=[jax.ShapeDtypeStruct((m, XATTN_WIDTH), F32),
                   jax.ShapeDtypeStruct((m, XATTN_WIDTH), F32)],
        compiler_params=pltpu.CompilerParams(vmem_limit_bytes=VMEM_LIMIT),
        name="memkv",
    )(mem, g.reshape(1, d), wkv_b, gk.reshape(1, -1))


def _outx_kernel(x1_ref, og_ref, om_ref, wout_ref, gx_ref, wq_ref, gq_ref, kx_ref, vx_ref, wo_ref,
                 o_ref):
    x2 = (x1_ref[...]
          + _dot(og_ref[...].astype(BF16), wout_ref[:GLA_WIDTH, :])
          + _dot(om_ref[...].astype(BF16), wout_ref[GLA_WIDTH:, :]))
    h = _rms(x2, gx_ref[...]).astype(BF16)
    q = _dot(h, wq_ref[...])
    gq = gq_ref[...]
    scale = XATTN_DH ** -0.5
    heads = []
    for i in range(XATTN_HEADS):
        cols = slice(i * XATTN_DH, (i + 1) * XATTN_DH)
        qh = _rms(q[:, cols], gq).astype(BF16)
        s = _dot_nt(qh, kx_ref[:, cols].astype(BF16)) * scale
        p = jnp.exp(s - jnp.max(s, axis=-1, keepdims=True))
        p = p * (1.0 / jnp.sum(p, axis=-1, keepdims=True))
        heads.append(_dot(p.astype(BF16), vx_ref[:, cols].astype(BF16)))
    o = jnp.concatenate(heads, axis=1).astype(BF16)
    o_ref[...] = x2 + _dot(o, wo_ref[...])


def _outx(x1, og, om, wout_b, gx, wq_b, gq, kx, vx, wo_b, seq, mem_len):
    n, d = x1.shape
    tiles_per_seq = seq // OUTX_TM
    const = lambda i: (0, 0)
    return pl.pallas_call(
        _outx_kernel,
        grid=(n // OUTX_TM,),
        in_specs=[
            pl.BlockSpec((OUTX_TM, d), lambda i: (i, 0)),
            pl.BlockSpec((OUTX_TM, GLA_WIDTH), lambda i: (i, 0)),
            pl.BlockSpec((OUTX_TM, MOBA_WIDTH), lambda i: (i, 0)),
            pl.BlockSpec((GLA_WIDTH + MOBA_WIDTH, d), const),
            pl.BlockSpec((1, d), const),
            pl.BlockSpec((d, XATTN_WIDTH), const),
            pl.BlockSpec((1, XATTN_DH), const),
            pl.BlockSpec((mem_len, XATTN_WIDTH), lambda i: (i // tiles_per_seq, 0)),
            pl.BlockSpec((mem_len, XATTN_WIDTH), lambda i: (i // tiles_per_seq, 0)),
            pl.BlockSpec((XATTN_WIDTH, d), const),
        ],
        out_specs=pl.BlockSpec((OUTX_TM, d), lambda i: (i, 0)),
        out_shape=jax.ShapeDtypeStruct((n, d), F32),
        compiler_params=_params("parallel"),
        name="outx",
    )(x1, og, om, wout_b, gx.reshape(1, d), wq_b, gq.reshape(1, -1), kx, vx, wo_b)


def kernel(x, mem, ffn1_norm, ffn1_w_gate, ffn1_w_up, ffn1_w_down, mix_norm, w_in, gla_w_gate2, gla_b_gate2, gla_out_norm, moba_q_norm, moba_k_norm, w_out, xattn_norm, mem_norm, xattn_w_q, xattn_w_kv, xattn_w_o, xattn_q_norm, xattn_k_norm, ffn2_norm, ffn2_w_gate, ffn2_w_up, ffn2_w_down):
    batch, seq, d = x.shape
    mem_len = mem.shape[1]
    depth = ffn1_norm.shape[0]
    n = batch * seq
    assert seq % GLA_TC == 0 and seq % (2 * MOBA_BLOCK) == 0 and seq % OUTX_TM == 0
    assert seq // MOBA_BLOCK + MOBA_DH <= MXU_DIM and seq // MOBA_BLOCK == 16
    assert n % FFN_TM == 0 and n % NORM_TM == 0 and n % PROJ_TM == 0 and PROJ_TM % MOBA_BLOCK == 0

    xf = x.reshape(n, d)
    memf = mem.reshape(batch * mem_len, d)
    lr0 = GLA_COLS
    mq0 = lr0 + GLA_GATE_RANK
    w_in_t = jnp.swapaxes(w_in, 1, 2)
    for l in range(depth):
        w_lr_t = w_in_t[l, lr0:mq0]
        w_mq_t = w_in_t[l, mq0:mq0 + MOBA_WIDTH]
        w_mk_t = w_in_t[l, mq0 + MOBA_WIDTH:mq0 + 2 * MOBA_WIDTH]
        w_mv_t = w_in_t[l, mq0 + 2 * MOBA_WIDTH:]

        x1 = _ffn(xf, ffn1_norm[l], ffn1_w_gate[l], ffn1_w_up[l], ffn1_w_down[l])
        hn, la = _mixnorm(x1, mix_norm[l], w_lr_t, gla_w_gate2[l], gla_b_gate2[l])
        u = _projg(hn, w_in_t, l)
        q_t, kb, kmean, v3 = _proj_moba(hn, w_mq_t, w_mk_t, w_mv_t, moba_q_norm[l], moba_k_norm[l])
        o_gla = _gla(u, la, gla_out_norm[l], batch, seq)
        o_moba = _moba(q_t, kb, kmean, v3, batch, seq)
        kx, vx = _memkv(memf, mem_norm[l], xattn_w_kv[l].astype(BF16), xattn_k_norm[l])
        x3 = _outx(x1, o_gla, o_moba, w_out[l].astype(BF16), xattn_norm[l],
                   xattn_w_q[l].astype(BF16), xattn_q_norm[l], kx, vx,
                   xattn_w_o[l].astype(BF16), seq, mem_len)
        xf = _ffn(x3, ffn2_norm[l], ffn2_w_gate[l], ffn2_w_up[l], ffn2_w_down[l])
    return xf.reshape(batch, seq, d)
```

```python
import jax
import jax.numpy as jnp
from jax import lax
from jax.experimental import pallas as pl
from jax.experimental.pallas import tpu as pltpu

F32 = jnp.float32
BF16 = jnp.bfloat16

EPS = 1e-6
NEG_INF = -1e30
D_FF = 5632
GLA_HEADS = 4
GLA_DV = 256
GLA_DK = 128
GLA_QK = GLA_HEADS * GLA_DK
GLA_WIDTH = GLA_HEADS * GLA_DV
GLA_COLS = 2 * GLA_QK + 2 * GLA_WIDTH
GLA_GATE_RANK = 16
GLA_GATE_TAU = 16.0
GLA_CHUNK = 64
MOBA_DH = 128
MOBA_HEADS = 8
MOBA_WIDTH = MOBA_HEADS * MOBA_DH
MOBA_BLOCK = 256
MOBA_TOPK = 3
XATTN_HEADS = 4
XATTN_DH = 128
XATTN_WIDTH = XATTN_HEADS * XATTN_DH

V7X_VMEM_BYTES = 64 * 1024 * 1024
VMEM_LIMIT = V7X_VMEM_BYTES - 6 * 1024 * 1024
FFN_VMEM_LIMIT = V7X_VMEM_BYTES - 2 * 1024 * 1024
MXU_DIM = 256

FFN_TM = 2048
FFN_TF = 256
NORM_TM = 1024
PROJ_TM = 2048
PROJ_TN = 512
GLA_TC = 512
GLA_SUB = 256
GLA_G = 4
MOBA_G = 4
MOBA_TQ = 2 * MOBA_BLOCK
MOBA_VPAD = 16
MOBA_VROWS = MOBA_DH + MOBA_VPAD
OUTX_TM = 512


def _params(*semantics):
    return pltpu.CompilerParams(dimension_semantics=semantics, vmem_limit_bytes=VMEM_LIMIT)


def _rms(x, g):
    return x * lax.rsqrt(jnp.mean(x * x, axis=-1, keepdims=True) + EPS) * g


def _dot(a, b):
    return jnp.dot(a, b, preferred_element_type=F32)


def _dot_nt(a, b):
    return lax.dot_general(a, b, (((1,), (1,)), ((), ())), preferred_element_type=F32)


def _silu(x):
    return x * (1.0 / (1.0 + jnp.exp(-x)))


def _ffn_kernel(x_hbm, g_ref, wg_ref, wu_ref, wd_ref, o_hbm, h_ref, xbuf_ref, acc_ref, xsem, osem):
    i = pl.program_id(0)
    f = pl.program_id(1)
    last_i = pl.num_programs(0) - 1
    last_f = pl.num_programs(1) - 1

    def x_copy(tile):
        return pltpu.make_async_copy(x_hbm.at[pl.ds(tile * FFN_TM, FFN_TM)], xbuf_ref, xsem)

    def o_copy(tile):
        return pltpu.make_async_copy(acc_ref, o_hbm.at[pl.ds(tile * FFN_TM, FFN_TM)], osem)

    @pl.when(jnp.logical_and(i == 0, f == 0))
    def _():
        x_copy(0).start()

    @pl.when(f == 0)
    def _():
        x_copy(i).wait()
        x = xbuf_ref[...]
        h_ref[...] = _rms(x, g_ref[...]).astype(BF16)

        @pl.when(i > 0)
        def _():
            o_copy(i - 1).wait()

        acc_ref[...] = 2.0 * xbuf_ref[...]

    @pl.when(jnp.logical_and(f == 1, i < last_i))
    def _():
        x_copy(i + 1).start()

    h = h_ref[...]
    gate = _dot(h, wg_ref[...].astype(BF16))
    up = _dot(h, wu_ref[...].astype(BF16))
    act = (_silu(gate) * up).astype(BF16)
    acc_ref[...] += _dot(act, wd_ref[...].astype(BF16))

    @pl.when(f == last_f)
    def _():
        acc_ref[...] = 0.5 * acc_ref[...]
        o_copy(i).start()

        @pl.when(i == last_i)
        def _():
            o_copy(i).wait()


def _ffn(x, g, wg, wu, wd):
    n, d = x.shape
    grid = (n // FFN_TM, D_FF // FFN_TF)
    assert grid[1] >= 2
    return pl.pallas_call(
        _ffn_kernel,
        grid=grid,
        in_specs=[
            pl.BlockSpec(memory_space=pl.ANY),
            pl.BlockSpec((1, d), lambda i, f: (0, 0)),
            pl.BlockSpec((d, FFN_TF), lambda i, f: (0, f)),
            pl.BlockSpec((d, FFN_TF), lambda i, f: (0, f)),
            pl.BlockSpec((FFN_TF, d), lambda i, f: (f, 0)),
        ],
        out_specs=pl.BlockSpec(memory_space=pl.ANY),
        out_shape=jax.ShapeDtypeStruct((n, d), F32),
        scratch_shapes=[pltpu.VMEM((FFN_TM, d), BF16), pltpu.VMEM((FFN_TM, d), F32),
                        pltpu.VMEM((FFN_TM, d), F32),
                        pltpu.SemaphoreType.DMA, pltpu.SemaphoreType.DMA],
        compiler_params=pltpu.CompilerParams(dimension_semantics=("arbitrary", "arbitrary"),
                                             vmem_limit_bytes=FFN_VMEM_LIMIT),
        name="ffn",
    )(x, g.reshape(1, d), wg, wu, wd)


def _mixnorm_kernel(x_ref, g_ref, wlr_ref, wg2_ref, bg2_ref, hn_ref, la_ref):
    h = _rms(x_ref[...], g_ref[...]).astype(BF16)
    hn_ref[...] = h
    lr = _dot_nt(h, wlr_ref[...].astype(BF16))
    pre = _dot(lr.astype(BF16), wg2_ref[...].astype(BF16)) + bg2_ref[...]
    log_sig = -(jnp.maximum(-pre, 0.0) + jnp.log(1.0 + jnp.exp(-jnp.abs(pre))))
    la_ref[...] = log_sig * (1.0 / GLA_GATE_TAU)


def _mixnorm(x1, g, w_lr_t, wg2, bg2):
    n, d = x1.shape
    const = lambda i: (0, 0)
    return pl.pallas_call(
        _mixnorm_kernel,
        grid=(n // NORM_TM,),
        in_specs=[
            pl.BlockSpec((NORM_TM, d), lambda i: (i, 0)),
            pl.BlockSpec((1, d), const),
            pl.BlockSpec((GLA_GATE_RANK, d), const),
            pl.BlockSpec((GLA_GATE_RANK, GLA_QK), const),
            pl.BlockSpec((1, GLA_QK), const),
        ],
        out_specs=[pl.BlockSpec((NORM_TM, d), lambda i: (i, 0)),
                   pl.BlockSpec((NORM_TM, GLA_QK), lambda i: (i, 0))],
        out_shape=[jax.ShapeDtypeStruct((n, d), BF16),
                   jax.ShapeDtypeStruct((n, GLA_QK), F32)],
        compiler_params=_params("parallel"),
        name="mixnorm",
    )(x1, g.reshape(1, d), w_lr_t, wg2, bg2.reshape(1, -1))


def _projg_kernel(hn_ref, wt_ref, u_ref):
    u_ref[...] = _dot_nt(hn_ref[...], wt_ref[...].astype(BF16))


def _projg(hn, w_in_t, layer):
    n, d = hn.shape
    return pl.pallas_call(
        _projg_kernel,
        grid=(n // PROJ_TM, GLA_COLS // PROJ_TN),
        in_specs=[
            pl.BlockSpec((PROJ_TM, d), lambda i, j: (i, 0)),
            pl.BlockSpec((None, PROJ_TN, d), lambda i, j: (layer, j, 0)),
        ],
        out_specs=pl.BlockSpec((PROJ_TM, PROJ_TN), lambda i, j: (i, j)),
        out_shape=jax.ShapeDtypeStruct((n, GLA_COLS), F32),
        compiler_params=_params("parallel", "arbitrary"),
        name="projg",
    )(hn, w_in_t)


def _projq_kernel(hn_ref, wt_ref, g_ref, qt_ref):
    yt = _dot_nt(wt_ref[0].astype(BF16), hn_ref[...])
    g = g_ref[...]
    parts = []
    for k in range(PROJ_TN // MOBA_DH):
        p = yt[k * MOBA_DH:(k + 1) * MOBA_DH, :]
        inv = lax.rsqrt(jnp.mean(p * p, axis=0, keepdims=True) + EPS)
        parts.append(p * inv * g)
    qt_ref[...] = jnp.concatenate(parts, axis=0)


def _projk_kernel(hn_ref, wt_ref, g_ref, kb_ref, kmean_ref):
    y = _dot_nt(hn_ref[...], wt_ref[0].astype(BF16))
    g = g_ref[...]
    parts = [_rms(y[:, k * MOBA_DH:(k + 1) * MOBA_DH], g) for k in range(PROJ_TN // MOBA_DH)]
    kn = jnp.concatenate(parts, axis=1)
    kb_ref[...] = kn.astype(BF16)
    blocks = PROJ_TM // MOBA_BLOCK
    kmean_ref[0] = jnp.sum(kn.reshape(blocks, MOBA_BLOCK, PROJ_TN), axis=1) * (1.0 / MOBA_BLOCK)


def _projv_kernel(hn_ref, wt_ref, v3_ref):
    yt = _dot_nt(wt_ref[0].astype(BF16), hn_ref[...]).astype(BF16)
    ones = jnp.ones((MOBA_VPAD, PROJ_TM), BF16)
    parts = []
    for k in range(PROJ_TN // MOBA_DH):
        parts += [yt[k * MOBA_DH:(k + 1) * MOBA_DH], ones]
    ya = jnp.concatenate(parts, axis=0)
    for t in range(PROJ_TM // MOBA_BLOCK):
        v3_ref[t] = ya[:, t * MOBA_BLOCK:(t + 1) * MOBA_BLOCK]


def _proj_moba(hn, w_in_t, layer, row0, gq, gk):
    n, d = hn.shape
    grid = (n // PROJ_TM, MOBA_WIDTH // PROJ_TN)
    hn_spec = pl.BlockSpec((PROJ_TM, d), lambda i, j: (i, 0))

    def wt_spec(first_row):
        return pl.BlockSpec((pl.Element(1), pl.Element(PROJ_TN), pl.Element(d)),
                            lambda i, j: (layer, pl.multiple_of(first_row + j * PROJ_TN, 8), 0))

    params = _params("parallel", "arbitrary")
    blocks = PROJ_TM // MOBA_BLOCK

    q_t = pl.pallas_call(
        _projq_kernel, grid=grid,
        in_specs=[hn_spec, wt_spec(row0), pl.BlockSpec((MOBA_DH, 1), lambda i, j: (0, 0))],
        out_specs=pl.BlockSpec((PROJ_TN, PROJ_TM), lambda i, j: (j, i)),
        out_shape=jax.ShapeDtypeStruct((MOBA_WIDTH, n), F32),
        compiler_params=params, name="projq",
    )(hn, w_in_t, gq.reshape(-1, 1))

    kb, kmean = pl.pallas_call(
        _projk_kernel, grid=grid,
        in_specs=[hn_spec, wt_spec(row0 + MOBA_WIDTH), pl.BlockSpec((1, MOBA_DH), lambda i, j: (0, 0))],
        out_specs=[pl.BlockSpec((PROJ_TM, PROJ_TN), lambda i, j: (i, j)),
                   pl.BlockSpec((1, blocks, PROJ_TN), lambda i, j: (i, 0, j))],
        out_shape=[jax.ShapeDtypeStruct((n, MOBA_WIDTH), BF16),
                   jax.ShapeDtypeStruct((n // PROJ_TM, blocks, MOBA_WIDTH), F32)],
        compiler_params=params, name="projk",
    )(hn, w_in_t, gk.reshape(1, -1))

    v3 = pl.pallas_call(
        _projv_kernel, grid=grid,
        in_specs=[hn_spec, wt_spec(row0 + 2 * MOBA_WIDTH)],
        out_specs=pl.BlockSpec((blocks, PROJ_TN // MOBA_DH * MOBA_VROWS, MOBA_BLOCK),
                               lambda i, j: (i, j, 0)),
        out_shape=jax.ShapeDtypeStruct((n // MOBA_BLOCK, MOBA_HEADS * MOBA_VROWS, MOBA_BLOCK), BF16),
        compiler_params=params, name="projv",
    )(hn, w_in_t)
    return q_t, kb, kmean.reshape(n // MOBA_BLOCK, MOBA_WIDTH), v3


def _dot_exact_rhs(m, a):
    a1 = a.astype(BF16)
    r1 = a - a1.astype(F32)
    a2 = r1.astype(BF16)
    a3 = (r1 - a2.astype(F32)).astype(BF16)
    return _dot(m, a1) + _dot(m, a2) + _dot(m, a3)


def _gla_head(q, k, v, r, la, g_out, state, upd_ref):
    tc = GLA_TC
    nchunk = tc // GLA_CHUNK
    sub = GLA_SUB

    crow = lax.broadcasted_iota(jnp.int32, (GLA_CHUNK, GLA_CHUNK), 0)
    ccol = lax.broadcasted_iota(jnp.int32, (GLA_CHUNK, GLA_CHUNK), 1)
    tri = (ccol <= crow).astype(BF16)
    la_wide = jnp.concatenate([la[c * GLA_CHUNK:(c + 1) * GLA_CHUNK] for c in range(nchunk)], axis=1)
    bc_wide = _dot_exact_rhs(tri, la_wide)
    bcum = jnp.concatenate([bc_wide[:, c * GLA_DK:(c + 1) * GLA_DK] for c in range(nchunk)], axis=0)
    btot = jnp.concatenate(
        [jnp.broadcast_to(bc_wide[GLA_CHUNK - 1:GLA_CHUNK, c * GLA_DK:(c + 1) * GLA_DK],
                          (GLA_CHUNK, GLA_DK)) for c in range(nchunk)], axis=0)

    q_dec_b = ((q * (GLA_DK ** -0.5)) * jnp.exp(bcum)).astype(BF16)
    k_inv_b = (k * jnp.exp(-bcum)).astype(BF16)
    k_tail_t = (k * jnp.exp(btot - bcum)).T.astype(BF16)
    decay_t = jnp.exp(btot).T
    vb = v.astype(BF16)

    row = lax.broadcasted_iota(jnp.int32, (sub, sub), 0)
    col = lax.broadcasted_iota(jnp.int32, (sub, sub), 1)
    causal = jnp.logical_and((row // GLA_CHUNK) == (col // GLA_CHUNK), col <= row)
    intra = []
    for p in range(tc // sub):
        rows = slice(p * sub, (p + 1) * sub)
        att = jnp.where(causal, _dot_nt(q_dec_b[rows], k_inv_b[rows]), 0.0)
        intra.append(_dot(att.astype(BF16), vb[rows]))

    for c in range(nchunk):
        rows = slice(c * GLA_CHUNK, (c + 1) * GLA_CHUNK)
        upd_ref[c] = _dot(k_tail_t[:, rows], vb[rows])

    inter = []
    for c in range(nchunk):
        rows = slice(c * GLA_CHUNK, (c + 1) * GLA_CHUNK)
        inter.append(_dot(q_dec_b[rows], state.astype(BF16)))
        dec_c = decay_t[:, c * GLA_CHUNK:c * GLA_CHUNK + 1]
        state = dec_c * state + upd_ref[c]
    o = jnp.concatenate(intra, axis=0) + jnp.concatenate(inter, axis=0)
    return _rms(o, g_out) * _silu(r), state


def _gla_kernel(q_ref, k_ref, v_ref, r_ref, la_ref, go_ref, o_ref, s_ref, upd_ref):
    @pl.when(pl.program_id(2) == 0)
    def _():
        s_ref[...] = jnp.zeros_like(s_ref)

    for g in range(GLA_G):
        qk = slice(g * GLA_DK, (g + 1) * GLA_DK)
        vr = slice(g * GLA_DV, (g + 1) * GLA_DV)
        o, state = _gla_head(q_ref[:, qk], k_ref[:, qk], v_ref[:, vr], r_ref[:, vr], la_ref[:, qk],
                             go_ref[...], s_ref[g], upd_ref.at[g])
        s_ref[g] = state
        o_ref[:, vr] = o


def _gla(u, la, g_out, batch, seq):
    n = u.shape[0]
    nt = seq // GLA_TC
    tok = lambda b, h, t: b * nt + t
    qk_w, v_w = GLA_G * GLA_DK, GLA_G * GLA_DV
    k_blk0 = GLA_QK // qk_w
    v_blk0 = 2 * GLA_QK // v_w
    r_blk0 = v_blk0 + GLA_WIDTH // v_w
    return pl.pallas_call(
        _gla_kernel,
        grid=(batch, GLA_HEADS // GLA_G, nt),
        in_specs=[
            pl.BlockSpec((GLA_TC, qk_w), lambda b, h, t: (tok(b, h, t), h)),
            pl.BlockSpec((GLA_TC, qk_w), lambda b, h, t: (tok(b, h, t), k_blk0 + h)),
            pl.BlockSpec((GLA_TC, v_w), lambda b, h, t: (tok(b, h, t), v_blk0 + h)),
            pl.BlockSpec((GLA_TC, v_w), lambda b, h, t: (tok(b, h, t), r_blk0 + h)),
            pl.BlockSpec((GLA_TC, qk_w), lambda b, h, t: (tok(b, h, t), h)),
            pl.BlockSpec((1, GLA_DV), lambda b, h, t: (0, 0)),
        ],
        out_specs=pl.BlockSpec((GLA_TC, v_w), lambda b, h, t: (tok(b, h, t), h)),
        out_shape=jax.ShapeDtypeStruct((n, GLA_WIDTH), F32),
        scratch_shapes=[pltpu.VMEM((GLA_G, GLA_DK, GLA_DV), F32),
                        pltpu.VMEM((GLA_G, GLA_TC // GLA_CHUNK, GLA_DK, GLA_DV), F32)],
        compiler_params=_params("parallel", "parallel", "arbitrary"),
        name="gla",
    )(u, u, u, u, la, g_out.reshape(1, -1))


def _moba_kernel(qt_ref, kb_ref, v3_ref, kmean_ref, o_ref, kaug_ref, snext_ref):
    ti = pl.program_id(2)
    last_tile = pl.num_programs(2) - 1
    bs = MOBA_BLOCK
    tq = MOBA_TQ
    nb = kb_ref.shape[0] // bs
    dh = MOBA_DH
    vr = MOBA_VROWS
    scale = dh ** -0.5

    @pl.when(ti == 0)
    def _():
        seq = kb_ref.shape[0]
        row_blk = lax.broadcasted_iota(jnp.int32, (seq, MXU_DIM - dh), 0) // bs
        lane = lax.broadcasted_iota(jnp.int32, (seq, MXU_DIM - dh), 1)
        onehot = (row_blk == lane).astype(BF16)
        for g in range(MOBA_G):
            kaug_ref[g, :, :dh] = kb_ref[:, g * dh:(g + 1) * dh]
            kaug_ref[g, :, dh:] = onehot

    blk = lax.broadcasted_iota(jnp.int32, (nb, tq), 0)
    q_blk = 2 * ti + (lax.broadcasted_iota(jnp.int32, (nb, tq), 1) // bs)
    past = blk < q_blk
    key_row = lax.broadcasted_iota(jnp.int32, (tq, tq), 0)
    qry_col = lax.broadcasted_iota(jnp.int32, (tq, tq), 1)
    causal = key_row <= qry_col
    same_blk = (key_row // bs) == (qry_col // bs)
    diag = pl.multiple_of(ti * tq, tq)

    def pv_dot(g, t, p):
        return (_dot(v3_ref[2 * t, g * vr:(g + 1) * vr, :], p[:bs])
                + _dot(v3_ref[2 * t + 1, g * vr:(g + 1) * vr, :], p[bs:]))

    q_aug, state = [], []
    for g in range(MOBA_G):
        qt = qt_ref[g * dh:(g + 1) * dh, :]
        gate = jnp.dot(kmean_ref[:, g * dh:(g + 1) * dh], qt,
                       precision=lax.Precision.HIGHEST, preferred_element_type=F32)
        gate = jnp.where(past, gate, NEG_INF)
        rank = jnp.zeros((nb, tq), jnp.int32)
        for j in range(nb):
            gj = gate[j:j + 1, :]
            ahead = jnp.logical_or(gj > gate, jnp.logical_and(gj == gate, blk > j))
            rank = rank + ahead.astype(jnp.int32)
        chosen = jnp.logical_and(past, rank < MOBA_TOPK)
        bias = jnp.where(chosen, 0.0, NEG_INF)
        qa = jnp.concatenate([(qt * scale).astype(BF16), bias.astype(BF16),
                              jnp.zeros((MXU_DIM - dh - nb, tq), BF16)], axis=0)
        q_aug.append(qa)

        first_chosen = jnp.sum(jnp.where(blk == 2 * ti, chosen.astype(F32), 0.0),
                               axis=0, keepdims=True) > 0.5
        allowed = jnp.logical_and(causal, jnp.logical_or(same_blk, first_chosen))
        s = _dot(kaug_ref[g, pl.ds(diag, tq), :dh], qa[:dh])
        s = jnp.where(allowed, s, NEG_INF)
        m = jnp.max(s, axis=0, keepdims=True)
        p = jnp.exp(s - m).astype(BF16)
        state.append((m, pv_dot(g, ti, p)))

    def score(g, t):
        off = pl.multiple_of(t * tq, tq)
        s = _dot(kaug_ref[g, pl.ds(off, tq), :], q_aug[g])
        return s, jnp.max(s, axis=0, keepdims=True)

    def consume(g, t, s, tile_max, m, acc, valid=None):
        if valid is not None:
            tile_max = jnp.where(valid, tile_max, NEG_INF)
        m_new = jnp.maximum(m, tile_max)
        alpha = jnp.exp(m - m_new)
        shift = m_new if valid is None else jnp.where(valid, m_new, -NEG_INF)
        p = jnp.exp(s - shift).astype(BF16)
        return m_new, alpha * acc + pv_dot(g, t, p)

    carry0 = []
    for g in range(MOBA_G):
        s, tile_max = score(g, 0)
        snext_ref[g] = s
        carry0.append(state[g] + (tile_max,))

    def body(u, carry):
        out = []
        for g in range(MOBA_G):
            m, acc, max_a = carry[g]
            s_b, max_b = score(g, 2 * u + 1)
            m, acc = consume(g, 2 * u, snext_ref[g], max_a, m, acc)
            s_c, max_c = score(g, jnp.minimum(2 * u + 2, last_tile))
            snext_ref[g] = s_c
            m, acc = consume(g, 2 * u + 1, s_b, max_b, m, acc, valid=2 * u + 1 < ti)
            out.append((m, acc, max_c))
        return tuple(out)

    carry = lax.fori_loop(0, (ti + 1) // 2, body, tuple(carry0))
    for g in range(MOBA_G):
        m, acc, _ = carry[g]
        o_ref[:, g * dh:(g + 1) * dh] = (acc[:dh] * (1.0 / acc[dh:dh + 1])).T


def _moba(q_t, kb, kmean, v3, batch, seq):
    n = kb.shape[0]
    nb = seq // MOBA_BLOCK
    nt = seq // MOBA_TQ
    gw = MOBA_G * MOBA_DH
    return pl.pallas_call(
        _moba_kernel,
        grid=(batch, MOBA_HEADS // MOBA_G, nt),
        in_specs=[
            pl.BlockSpec((gw, MOBA_TQ), lambda b, h, i: (h, b * nt + i)),
            pl.BlockSpec((seq, gw), lambda b, h, i: (b, h)),
            pl.BlockSpec((nb, MOBA_G * MOBA_VROWS, MOBA_BLOCK), lambda b, h, i: (b, h, 0)),
            pl.BlockSpec((nb, gw), lambda b, h, i: (b, h)),
        ],
        out_specs=pl.BlockSpec((MOBA_TQ, gw), lambda b, h, i: (b * nt + i, h)),
        out_shape=jax.ShapeDtypeStruct((n, MOBA_WIDTH), F32),
        scratch_shapes=[pltpu.VMEM((MOBA_G, seq, MXU_DIM), BF16),
                        pltpu.VMEM((MOBA_G, MOBA_TQ, MOBA_TQ), F32)],
        compiler_params=_params("parallel", "parallel", "arbitrary"),
        name="moba",
    )(q_t, kb, v3, kmean)


def _memkv_kernel(mem_ref, g_ref, wkv_ref, gk_ref, k_ref, v_ref):
    h = _rms(mem_ref[...], g_ref[...]).astype(BF16)
    kv = _dot(h, wkv_ref[...].astype(BF16))
    gk = gk_ref[...]
    ks = [_rms(kv[:, i * XATTN_DH:(i + 1) * XATTN_DH], gk) for i in range(XATTN_HEADS)]
    k_ref[...] = jnp.concatenate(ks, axis=1)
    v_ref[...] = kv[:, XATTN_WIDTH:]


def _memkv(mem, g, wkv_b, gk):
    m, d = mem.shape
    return pl.pallas_call(
        _memkv_kernel,
        out_shape=[jax.ShapeDtypeStruct((m, XATTN_WIDTH), F32),
                   jax.ShapeDtypeStruct((m, XATTN_WIDTH), F32)],
        compiler_params=pltpu.CompilerParams(vmem_limit_bytes=VMEM_LIMIT),
        name="memkv",
    )(mem, g.reshape(1, d), wkv_b, gk.reshape(1, -1))


def _outx_kernel(x1_ref, og_ref, om_ref, wout_ref, gx_ref, wq_ref, gq_ref, kx_ref, vx_ref, wo_ref,
                 o_ref):
    x2 = (x1_ref[...]
          + _dot(og_ref[...].astype(BF16), wout_ref[:GLA_WIDTH, :])
          + _dot(om_ref[...].astype(BF16), wout_ref[GLA_WIDTH:, :]))
    h = _rms(x2, gx_ref[...]).astype(BF16)
    q = _dot(h, wq_ref[...])
    gq = gq_ref[...]
    scale = XATTN_DH ** -0.5
    heads = []
    for i in range(XATTN_HEADS):
        cols = slice(i * XATTN_DH, (i + 1) * XATTN_DH)
        qh = _rms(q[:, cols], gq).astype(BF16)
        s = _dot_nt(qh, kx_ref[:, cols].astype(BF16)) * scale
        p = jnp.exp(s - jnp.max(s, axis=-1, keepdims=True))
        p = p * (1.0 / jnp.sum(p, axis=-1, keepdims=True))
        heads.append(_dot(p.astype(BF16), vx_ref[:, cols].astype(BF16)))
    o = jnp.concatenate(heads, axis=1).astype(BF16)
    o_ref[...] = x2 + _dot(o, wo_ref[...])


def _outx(x1, og, om, wout_b, gx, wq_b, gq, kx, vx, wo_b, seq, mem_len):
    n, d = x1.shape
    tiles_per_seq = seq // OUTX_TM
    const = lambda i: (0, 0)
    return pl.pallas_call(
        _outx_kernel,
        grid=(n // OUTX_TM,),
        in_specs=[
            pl.BlockSpec((OUTX_TM, d), lambda i: (i, 0)),
            pl.BlockSpec((OUTX_TM, GLA_WIDTH), lambda i: (i, 0)),
            pl.BlockSpec((OUTX_TM, MOBA_WIDTH), lambda i: (i, 0)),
            pl.BlockSpec((GLA_WIDTH + MOBA_WIDTH, d), const),
            pl.BlockSpec((1, d), const),
            pl.BlockSpec((d, XATTN_WIDTH), const),
            pl.BlockSpec((1, XATTN_DH), const),
            pl.BlockSpec((mem_len, XATTN_WIDTH), lambda i: (i // tiles_per_seq, 0)),
            pl.BlockSpec((mem_len, XATTN_WIDTH), lambda i: (i // tiles_per_seq, 0)),
            pl.BlockSpec((XATTN_WIDTH, d), const),
        ],
        out_specs=pl.BlockSpec((OUTX_TM, d), lambda i: (i, 0)),
        out_shape=jax.ShapeDtypeStruct((n, d), F32),
        compiler_params=_params("parallel"),
        name="outx",
    )(x1, og, om, wout_b, gx.reshape(1, d), wq_b, gq.reshape(1, -1), kx, vx, wo_b)


def kernel(x, mem, ffn1_norm, ffn1_w_gate, ffn1_w_up, ffn1_w_down, mix_norm, w_in, gla_w_gate2, gla_b_gate2, gla_out_norm, moba_q_norm, moba_k_norm, w_out, xattn_norm, mem_norm, xattn_w_q, xattn_w_kv, xattn_w_o, xattn_q_norm, xattn_k_norm, ffn2_norm, ffn2_w_gate, ffn2_w_up, ffn2_w_down):
    batch, seq, d = x.shape
    mem_len = mem.shape[1]
    depth = ffn1_norm.shape[0]
    n = batch * seq
    assert seq % GLA_TC == 0 and seq % (2 * MOBA_BLOCK) == 0 and seq % OUTX_TM == 0
    assert seq // MOBA_BLOCK + MOBA_DH <= MXU_DIM and seq // MOBA_BLOCK == 16
    assert n % FFN_TM == 0 and n % NORM_TM == 0 and n % PROJ_TM == 0 and PROJ_TM % MOBA_BLOCK == 0

    xf = x.reshape(n, d)
    memf = mem.reshape(batch * mem_len, d)
    lr0 = GLA_COLS
    mq0 = lr0 + GLA_GATE_RANK
    w_in_t = jnp.swapaxes(w_in, 1, 2)
    for l in range(depth):
        w_lr_t = w_in_t[l, lr0:mq0]

        x1 = _ffn(xf, ffn1_norm[l], ffn1_w_gate[l], ffn1_w_up[l], ffn1_w_down[l])
        hn, la = _mixnorm(x1, mix_norm[l], w_lr_t, gla_w_gate2[l], gla_b_gate2[l])
        u = _projg(hn, w_in_t, l)
        q_t, kb, kmean, v3 = _proj_moba(hn, w_in_t, l, mq0, moba_q_norm[l], moba_k_norm[l])
        o_gla = _gla(u, la, gla_out_norm[l], batch, seq)
        o_moba = _moba(q_t, kb, kmean, v3, batch, seq)
        kx, vx = _memkv(memf, mem_norm[l], xattn_w_kv[l], xattn_k_norm[l])
        x3 = _outx(x1, o_gla, o_moba, w_out[l].astype(BF16), xattn_norm[l],
                   xattn_w_q[l].astype(BF16), xattn_q_norm[l], kx, vx,
                   xattn_w_o[l].astype(BF16), seq, mem_len)
        xf = _ffn(x3, ffn2_norm[l], ffn2_w_gate[l], ffn2_w_up[l], ffn2_w_down[l])
    return xf.reshape(batch, seq, d)
```

```python
import jax
import jax.numpy as jnp
from jax import lax
from jax.experimental import pallas as pl
from jax.experimental.pallas import tpu as pltpu

F32 = jnp.float32
BF16 = jnp.bfloat16

EPS = 1e-6
NEG_INF = -1e30
D_FF = 5632
GLA_HEADS = 4
GLA_DV = 256
GLA_DK = 128
GLA_QK = GLA_HEADS * GLA_DK
GLA_WIDTH = GLA_HEADS * GLA_DV
GLA_COLS = 2 * GLA_QK + 2 * GLA_WIDTH
GLA_GATE_RANK = 16
GLA_GATE_TAU = 16.0
GLA_CHUNK = 64
MOBA_DH = 128
MOBA_HEADS = 8
MOBA_WIDTH = MOBA_HEADS * MOBA_DH
MOBA_BLOCK = 256
MOBA_TOPK = 3
XATTN_HEADS = 4
XATTN_DH = 128
XATTN_WIDTH = XATTN_HEADS * XATTN_DH

V7X_VMEM_BYTES = 64 * 1024 * 1024
VMEM_LIMIT = V7X_VMEM_BYTES - 6 * 1024 * 1024
FFN_VMEM_LIMIT = V7X_VMEM_BYTES - 2 * 1024 * 1024
MXU_DIM = 256

FFN_TM = 1024
FFN_TF = 512
NORM_TM = 1024
PROJ_TM = 2048
PROJ_TN = 512
GLA_TC = 512
GLA_SUB = 256
GLA_G = 4
MOBA_G = 4
MOBA_TQ = 2 * MOBA_BLOCK
MOBA_VPAD = 16
MOBA_VROWS = MOBA_DH + MOBA_VPAD
OUTX_TM = 512


def _params(*semantics):
    return pltpu.CompilerParams(dimension_semantics=semantics, vmem_limit_bytes=VMEM_LIMIT)


def _rms(x, g):
    return x * lax.rsqrt(jnp.mean(x * x, axis=-1, keepdims=True) + EPS) * g


def _dot(a, b):
    return jnp.dot(a, b, preferred_element_type=F32)


def _dot_nt(a, b):
    return lax.dot_general(a, b, (((1,), (1,)), ((), ())), preferred_element_type=F32)


def _silu(x):
    return x * (1.0 / (1.0 + jnp.exp(-x)))


def _ffn_kernel(x_hbm, g_ref, wg_ref, wu_ref, wd_ref, o_hbm, h_ref, xbuf_ref, acc_ref, xsem, osem):
    i = pl.program_id(0)
    f = pl.program_id(1)
    last_i = pl.num_programs(0) - 1
    last_f = pl.num_programs(1) - 1

    def x_copy(tile):
        return pltpu.make_async_copy(x_hbm.at[pl.ds(tile * FFN_TM, FFN_TM)], xbuf_ref, xsem)

    def o_copy(tile):
        return pltpu.make_async_copy(acc_ref, o_hbm.at[pl.ds(tile * FFN_TM, FFN_TM)], osem)

    @pl.when(jnp.logical_and(i == 0, f == 0))
    def _():
        x_copy(0).start()

    @pl.when(f == 0)
    def _():
        x_copy(i).wait()
        x = xbuf_ref[...]
        h_ref[...] = _rms(x, g_ref[...]).astype(BF16)

        @pl.when(i > 0)
        def _():
            o_copy(i - 1).wait()

        acc_ref[...] = 2.0 * xbuf_ref[...]

    @pl.when(jnp.logical_and(f == 1, i < last_i))
    def _():
        x_copy(i + 1).start()

    h = h_ref[...]
    gate = _dot(h, wg_ref[...].astype(BF16))
    up = _dot(h, wu_ref[...].astype(BF16))
    act = (_silu(gate) * up).astype(BF16)
    acc_ref[...] += _dot(act, wd_ref[...].astype(BF16))

    @pl.when(f == last_f)
    def _():
        acc_ref[...] = 0.5 * acc_ref[...]
        o_copy(i).start()

        @pl.when(i == last_i)
        def _():
            o_copy(i).wait()


def _ffn(x, g, wg, wu, wd):
    n, d = x.shape
    grid = (n // FFN_TM, D_FF // FFN_TF)
    assert grid[1] >= 2
    return pl.pallas_call(
        _ffn_kernel,
        grid=grid,
        in_specs=[
            pl.BlockSpec(memory_space=pl.ANY),
            pl.BlockSpec((1, d), lambda i, f: (0, 0)),
            pl.BlockSpec((d, FFN_TF), lambda i, f: (0, f)),
            pl.BlockSpec((d, FFN_TF), lambda i, f: (0, f)),
            pl.BlockSpec((FFN_TF, d), lambda i, f: (f, 0)),
        ],
        out_specs=pl.BlockSpec(memory_space=pl.ANY),
        out_shape=jax.ShapeDtypeStruct((n, d), F32),
        scratch_shapes=[pltpu.VMEM((FFN_TM, d), BF16), pltpu.VMEM((FFN_TM, d), F32),
                        pltpu.VMEM((FFN_TM, d), F32),
                        pltpu.SemaphoreType.DMA, pltpu.SemaphoreType.DMA],
        compiler_params=pltpu.CompilerParams(dimension_semantics=("arbitrary", "arbitrary"),
                                             vmem_limit_bytes=FFN_VMEM_LIMIT),
        name="ffn",
    )(x, g.reshape(1, d), wg, wu, wd)


def _mixnorm_kernel(x_ref, g_ref, wlr_ref, wg2_ref, bg2_ref, hn_ref, la_ref):
    h = _rms(x_ref[...], g_ref[...]).astype(BF16)
    hn_ref[...] = h
    lr = _dot_nt(h, wlr_ref[...].astype(BF16))
    pre = _dot(lr.astype(BF16), wg2_ref[...].astype(BF16)) + bg2_ref[...]
    log_sig = -(jnp.maximum(-pre, 0.0) + jnp.log(1.0 + jnp.exp(-jnp.abs(pre))))
    la_ref[...] = log_sig * (1.0 / GLA_GATE_TAU)


def _mixnorm(x1, g, w_lr_t, wg2, bg2):
    n, d = x1.shape
    const = lambda i: (0, 0)
    return pl.pallas_call(
        _mixnorm_kernel,
        grid=(n // NORM_TM,),
        in_specs=[
            pl.BlockSpec((NORM_TM, d), lambda i: (i, 0)),
            pl.BlockSpec((1, d), const),
            pl.BlockSpec((GLA_GATE_RANK, d), const),
            pl.BlockSpec((GLA_GATE_RANK, GLA_QK), const),
            pl.BlockSpec((1, GLA_QK), const),
        ],
        out_specs=[pl.BlockSpec((NORM_TM, d), lambda i: (i, 0)),
                   pl.BlockSpec((NORM_TM, GLA_QK), lambda i: (i, 0))],
        out_shape=[jax.ShapeDtypeStruct((n, d), BF16),
                   jax.ShapeDtypeStruct((n, GLA_QK), F32)],
        compiler_params=_params("parallel"),
        name="mixnorm",
    )(x1, g.reshape(1, d), w_lr_t, wg2, bg2.reshape(1, -1))


def _projg_kernel(hn_ref, wt_ref, u_ref):
    u_ref[...] = _dot_nt(hn_ref[...], wt_ref[...].astype(BF16))


def _projg(hn, w_in_t, layer):
    n, d = hn.shape
    return pl.pallas_call(
        _projg_kernel,
        grid=(n // PROJ_TM, GLA_COLS // PROJ_TN),
        in_specs=[
            pl.BlockSpec((PROJ_TM, d), lambda i, j: (i, 0)),
            pl.BlockSpec((None, PROJ_TN, d), lambda i, j: (layer, j, 0)),
        ],
        out_specs=pl.BlockSpec((PROJ_TM, PROJ_TN), lambda i, j: (i, j)),
        out_shape=jax.ShapeDtypeStruct((n, GLA_COLS), F32),
        compiler_params=_params("parallel", "arbitrary"),
        name="projg",
    )(hn, w_in_t)


def _projq_kernel(hn_ref, wt_ref, g_ref, qt_ref):
    yt = _dot_nt(wt_ref[0].astype(BF16), hn_ref[...])
    g = g_ref[...]
    parts = []
    for k in range(PROJ_TN // MOBA_DH):
        p = yt[k * MOBA_DH:(k + 1) * MOBA_DH, :]
        inv = lax.rsqrt(jnp.mean(p * p, axis=0, keepdims=True) + EPS)
        parts.append(p * inv * g)
    qt_ref[...] = jnp.concatenate(parts, axis=0)


def _projk_kernel(hn_ref, wt_ref, g_ref, kb_ref, kmean_ref):
    y = _dot_nt(hn_ref[...], wt_ref[0].astype(BF16))
    g = g_ref[...]
    parts = [_rms(y[:, k * MOBA_DH:(k + 1) * MOBA_DH], g) for k in range(PROJ_TN // MOBA_DH)]
    kn = jnp.concatenate(parts, axis=1)
    kb_ref[...] = kn.astype(BF16)
    blocks = PROJ_TM // MOBA_BLOCK
    kmean_ref[0] = jnp.sum(kn.reshape(blocks, MOBA_BLOCK, PROJ_TN), axis=1) * (1.0 / MOBA_BLOCK)


def _projv_kernel(hn_ref, wt_ref, v3_ref):
    yt = _dot_nt(wt_ref[0].astype(BF16), hn_ref[...]).astype(BF16)
    ones = jnp.ones((MOBA_VPAD, PROJ_TM), BF16)
    parts = []
    for k in range(PROJ_TN // MOBA_DH):
        parts += [yt[k * MOBA_DH:(k + 1) * MOBA_DH], ones]
    ya = jnp.concatenate(parts, axis=0)
    for t in range(PROJ_TM // MOBA_BLOCK):
        v3_ref[t] = ya[:, t * MOBA_BLOCK:(t + 1) * MOBA_BLOCK]


def _projm_kernel(hn_ref, wt_ref, gq_ref, gk_ref, qt_ref, kb_ref, kmean_ref, v3_ref):
    j = pl.program_id(1)
    steps = MOBA_WIDTH // PROJ_TN

    @pl.when(j < steps)
    def _():
        _projq_kernel(hn_ref, wt_ref, gq_ref, qt_ref)

    @pl.when(jnp.logical_and(j >= steps, j < 2 * steps))
    def _():
        _projk_kernel(hn_ref, wt_ref, gk_ref, kb_ref, kmean_ref)

    @pl.when(j >= 2 * steps)
    def _():
        _projv_kernel(hn_ref, wt_ref, v3_ref)


def _proj_moba(hn, w_in_t, layer, row0, gq, gk):
    n, d = hn.shape
    steps = MOBA_WIDTH // PROJ_TN
    blocks = PROJ_TM // MOBA_BLOCK
    own = lambda j, first: jnp.clip(j - first * steps, 0, steps - 1)
    const = lambda i, j: (0, 0)
    q_t, kb, kmean, v3 = pl.pallas_call(
        _projm_kernel,
        grid=(n // PROJ_TM, 3 * steps),
        in_specs=[
            pl.BlockSpec((PROJ_TM, d), lambda i, j: (i, 0)),
            pl.BlockSpec((pl.Element(1), pl.Element(PROJ_TN), pl.Element(d)),
                         lambda i, j: (layer, pl.multiple_of(row0 + j * PROJ_TN, 8), 0)),
            pl.BlockSpec((MOBA_DH, 1), const),
            pl.BlockSpec((1, MOBA_DH), const),
        ],
        out_specs=[
            pl.BlockSpec((PROJ_TN, PROJ_TM), lambda i, j: (own(j, 0), i)),
            pl.BlockSpec((PROJ_TM, PROJ_TN), lambda i, j: (i, own(j, 1))),
            pl.BlockSpec((1, blocks, PROJ_TN), lambda i, j: (i, 0, own(j, 1))),
            pl.BlockSpec((blocks, PROJ_TN // MOBA_DH * MOBA_VROWS, MOBA_BLOCK),
                         lambda i, j: (i, own(j, 2), 0)),
        ],
        out_shape=[jax.ShapeDtypeStruct((MOBA_WIDTH, n), F32),
                   jax.ShapeDtypeStruct((n, MOBA_WIDTH), BF16),
                   jax.ShapeDtypeStruct((n // PROJ_TM, blocks, MOBA_WIDTH), F32),
                   jax.ShapeDtypeStruct((n // MOBA_BLOCK, MOBA_HEADS * MOBA_VROWS, MOBA_BLOCK), BF16)],
        compiler_params=_params("arbitrary", "arbitrary"),
        name="projm",
    )(hn, w_in_t, gq.reshape(-1, 1), gk.reshape(1, -1))
    return q_t, kb, kmean.reshape(n // MOBA_BLOCK, MOBA_WIDTH), v3


def _dot_exact_rhs(m, a):
    a1 = a.astype(BF16)
    r1 = a - a1.astype(F32)
    a2 = r1.astype(BF16)
    a3 = (r1 - a2.astype(F32)).astype(BF16)
    return _dot(m, a1) + _dot(m, a2) + _dot(m, a3)


def _gla_head(q, k, v, r, la, g_out, state, upd_ref):
    tc = GLA_TC
    nchunk = tc // GLA_CHUNK
    sub = GLA_SUB

    crow = lax.broadcasted_iota(jnp.int32, (GLA_CHUNK, GLA_CHUNK), 0)
    ccol = lax.broadcasted_iota(jnp.int32, (GLA_CHUNK, GLA_CHUNK), 1)
    tri = (ccol <= crow).astype(BF16)
    la_wide = jnp.concatenate([la[c * GLA_CHUNK:(c + 1) * GLA_CHUNK] for c in range(nchunk)], axis=1)
    bc_wide = _dot_exact_rhs(tri, la_wide)
    bcum = jnp.concatenate([bc_wide[:, c * GLA_DK:(c + 1) * GLA_DK] for c in range(nchunk)], axis=0)
    btot = jnp.concatenate(
        [jnp.broadcast_to(bc_wide[GLA_CHUNK - 1:GLA_CHUNK, c * GLA_DK:(c + 1) * GLA_DK],
                          (GLA_CHUNK, GLA_DK)) for c in range(nchunk)], axis=0)

    q_dec_b = ((q * (GLA_DK ** -0.5)) * jnp.exp(bcum)).astype(BF16)
    k_inv_b = (k * jnp.exp(-bcum)).astype(BF16)
    k_tail_t = (k * jnp.exp(btot - bcum)).T.astype(BF16)
    decay_t = jnp.exp(btot).T
    vb = v.astype(BF16)

    row = lax.broadcasted_iota(jnp.int32, (sub, sub), 0)
    col = lax.broadcasted_iota(jnp.int32, (sub, sub), 1)
    causal = jnp.logical_and((row // GLA_CHUNK) == (col // GLA_CHUNK), col <= row)
    intra = []
    for p in range(tc // sub):
        rows = slice(p * sub, (p + 1) * sub)
        att = jnp.where(causal, _dot_nt(q_dec_b[rows], k_inv_b[rows]), 0.0)
        intra.append(_dot(att.astype(BF16), vb[rows]))

    for c in range(nchunk):
        rows = slice(c * GLA_CHUNK, (c + 1) * GLA_CHUNK)
        upd_ref[c] = _dot(k_tail_t[:, rows], vb[rows])

    inter = []
    for c in range(nchunk):
        rows = slice(c * GLA_CHUNK, (c + 1) * GLA_CHUNK)
        inter.append(_dot(q_dec_b[rows], state.astype(BF16)))
        dec_c = decay_t[:, c * GLA_CHUNK:c * GLA_CHUNK + 1]
        state = dec_c * state + upd_ref[c]
    o = jnp.concatenate(intra, axis=0) + jnp.concatenate(inter, axis=0)
    return _rms(o, g_out) * _silu(r), state


def _gla_kernel(q_ref, k_ref, v_ref, r_ref, la_ref, go_ref, o_ref, s_ref, upd_ref):
    @pl.when(pl.program_id(2) == 0)
    def _():
        s_ref[...] = jnp.zeros_like(s_ref)

    for g in range(GLA_G):
        qk = slice(g * GLA_DK, (g + 1) * GLA_DK)
        vr = slice(g * GLA_DV, (g + 1) * GLA_DV)
        o, state = _gla_head(q_ref[:, qk], k_ref[:, qk], v_ref[:, vr], r_ref[:, vr], la_ref[:, qk],
                             go_ref[...], s_ref[g], upd_ref.at[g])
        s_ref[g] = state
        o_ref[:, vr] = o


def _gla(u, la, g_out, batch, seq):
    n = u.shape[0]
    nt = seq // GLA_TC
    tok = lambda b, h, t: b * nt + t
    qk_w, v_w = GLA_G * GLA_DK, GLA_G * GLA_DV
    k_blk0 = GLA_QK // qk_w
    v_blk0 = 2 * GLA_QK // v_w
    r_blk0 = v_blk0 + GLA_WIDTH // v_w
    return pl.pallas_call(
        _gla_kernel,
        grid=(batch, GLA_HEADS // GLA_G, nt),
        in_specs=[
            pl.BlockSpec((GLA_TC, qk_w), lambda b, h, t: (tok(b, h, t), h)),
            pl.BlockSpec((GLA_TC, qk_w), lambda b, h, t: (tok(b, h, t), k_blk0 + h)),
            pl.BlockSpec((GLA_TC, v_w), lambda b, h, t: (tok(b, h, t), v_blk0 + h)),
            pl.BlockSpec((GLA_TC, v_w), lambda b, h, t: (tok(b, h, t), r_blk0 + h)),
            pl.BlockSpec((GLA_TC, qk_w), lambda b, h, t: (tok(b, h, t), h)),
            pl.BlockSpec((1, GLA_DV), lambda b, h, t: (0, 0)),
        ],
        out_specs=pl.BlockSpec((GLA_TC, v_w), lambda b, h, t: (tok(b, h, t), h)),
        out_shape=jax.ShapeDtypeStruct((n, GLA_WIDTH), F32),
        scratch_shapes=[pltpu.VMEM((GLA_G, GLA_DK, GLA_DV), F32),
                        pltpu.VMEM((GLA_G, GLA_TC // GLA_CHUNK, GLA_DK, GLA_DV), F32)],
        compiler_params=_params("parallel", "parallel", "arbitrary"),
        name="gla",
    )(u, u, u, u, la, g_out.reshape(1, -1))


def _moba_kernel(qt_ref, kb_ref, v3_ref, kmean_ref, o_ref, kaug_ref, snext_ref):
    ti = pl.program_id(2)
    last_tile = pl.num_programs(2) - 1
    bs = MOBA_BLOCK
    tq = MOBA_TQ
    nb = kb_ref.shape[0] // bs
    dh = MOBA_DH
    vr = MOBA_VROWS
    scale = dh ** -0.5

    @pl.when(ti == 0)
    def _():
        seq = kb_ref.shape[0]
        row_blk = lax.broadcasted_iota(jnp.int32, (seq, MXU_DIM - dh), 0) // bs
        lane = lax.broadcasted_iota(jnp.int32, (seq, MXU_DIM - dh), 1)
        onehot = (row_blk == lane).astype(BF16)
        for g in range(MOBA_G):
            kaug_ref[g, :, :dh] = kb_ref[:, g * dh:(g + 1) * dh]
            kaug_ref[g, :, dh:] = onehot

    blk = lax.broadcasted_iota(jnp.int32, (nb, tq), 0)
    q_blk = 2 * ti + (lax.broadcasted_iota(jnp.int32, (nb, tq), 1) // bs)
    past = blk < q_blk
    key_row = lax.broadcasted_iota(jnp.int32, (tq, tq), 0)
    qry_col = lax.broadcasted_iota(jnp.int32, (tq, tq), 1)
    causal = key_row <= qry_col
    same_blk = (key_row // bs) == (qry_col // bs)
    diag = pl.multiple_of(ti * tq, tq)

    def pv_dot(g, t, p):
        return (_dot(v3_ref[2 * t, g * vr:(g + 1) * vr, :], p[:bs])
                + _dot(v3_ref[2 * t + 1, g * vr:(g + 1) * vr, :], p[bs:]))

    q_aug, state = [], []
    for g in range(MOBA_G):
        qt = qt_ref[g * dh:(g + 1) * dh, :]
        gate = jnp.dot(kmean_ref[:, g * dh:(g + 1) * dh], qt,
                       precision=lax.Precision.HIGHEST, preferred_element_type=F32)
        gate = jnp.where(past, gate, NEG_INF)
        rank = jnp.zeros((nb, tq), jnp.int32)
        for j in range(nb):
            gj = gate[j:j + 1, :]
            ahead = jnp.logical_or(gj > gate, jnp.logical_and(gj == gate, blk > j))
            rank = rank + ahead.astype(jnp.int32)
        chosen = jnp.logical_and(past, rank < MOBA_TOPK)
        bias = jnp.where(chosen, 0.0, NEG_INF)
        qa = jnp.concatenate([(qt * scale).astype(BF16), bias.astype(BF16),
                              jnp.zeros((MXU_DIM - dh - nb, tq), BF16)], axis=0)
        q_aug.append(qa)

        first_chosen = jnp.sum(jnp.where(blk == 2 * ti, chosen.astype(F32), 0.0),
                               axis=0, keepdims=True) > 0.5
        allowed = jnp.logical_and(causal, jnp.logical_or(same_blk, first_chosen))
        s = _dot(kaug_ref[g, pl.ds(diag, tq), :dh], qa[:dh])
        s = jnp.where(allowed, s, NEG_INF)
        m = jnp.max(s, axis=0, keepdims=True)
        p = jnp.exp(s - m).astype(BF16)
        state.append((m, pv_dot(g, ti, p)))

    def score(g, t):
        off = pl.multiple_of(t * tq, tq)
        s = _dot(kaug_ref[g, pl.ds(off, tq), :], q_aug[g])
        return s, jnp.max(s, axis=0, keepdims=True)

    def consume(g, t, s, tile_max, m, acc, valid=None):
        if valid is not None:
            tile_max = jnp.where(valid, tile_max, NEG_INF)
        m_new = jnp.maximum(m, tile_max)
        alpha = jnp.exp(m - m_new)
        shift = m_new if valid is None else jnp.where(valid, m_new, -NEG_INF)
        p = jnp.exp(s - shift).astype(BF16)
        return m_new, alpha * acc + pv_dot(g, t, p)

    carry0 = []
    for g in range(MOBA_G):
        s, tile_max = score(g, 0)
        snext_ref[g] = s
        carry0.append(state[g] + (tile_max,))

    def body(u, carry):
        out = []
        for g in range(MOBA_G):
            m, acc, max_a = carry[g]
            s_b, max_b = score(g, 2 * u + 1)
            m, acc = consume(g, 2 * u, snext_ref[g], max_a, m, acc)
            s_c, max_c = score(g, jnp.minimum(2 * u + 2, last_tile))
            snext_ref[g] = s_c
            m, acc = consume(g, 2 * u + 1, s_b, max_b, m, acc, valid=2 * u + 1 < ti)
            out.append((m, acc, max_c))
        return tuple(out)

    carry = lax.fori_loop(0, (ti + 1) // 2, body, tuple(carry0))
    for g in range(MOBA_G):
        m, acc, _ = carry[g]
        o_ref[:, g * dh:(g + 1) * dh] = (acc[:dh] * (1.0 / acc[dh:dh + 1])).T


def _moba(q_t, kb, kmean, v3, batch, seq):
    n = kb.shape[0]
    nb = seq // MOBA_BLOCK
    nt = seq // MOBA_TQ
    gw = MOBA_G * MOBA_DH
    return pl.pallas_call(
        _moba_kernel,
        grid=(batch, MOBA_HEADS // MOBA_G, nt),
        in_specs=[
            pl.BlockSpec((gw, MOBA_TQ), lambda b, h, i: (h, b * nt + i)),
            pl.BlockSpec((seq, gw), lambda b, h, i: (b, h)),
            pl.BlockSpec((nb, MOBA_G * MOBA_VROWS, MOBA_BLOCK), lambda b, h, i: (b, h, 0)),
            pl.BlockSpec((nb, gw), lambda b, h, i: (b, h)),
        ],
        out_specs=pl.BlockSpec((MOBA_TQ, gw), lambda b, h, i: (b * nt + i, h)),
        out_shape=jax.ShapeDtypeStruct((n, MOBA_WIDTH), F32),
        scratch_shapes=[pltpu.VMEM((MOBA_G, seq, MXU_DIM), BF16),
                        pltpu.VMEM((MOBA_G, MOBA_TQ, MOBA_TQ), F32)],
        compiler_params=_params("parallel", "parallel", "arbitrary"),
        name="moba",
    )(q_t, kb, v3, kmean)


def _memkv_kernel(mem_ref, g_ref, wkv_ref, gk_ref, k_ref, v_ref):
    h = _rms(mem_ref[...], g_ref[...]).astype(BF16)
    kv = _dot(h, wkv_ref[...].astype(BF16))
    gk = gk_ref[...]
    ks = [_rms(kv[:, i * XATTN_DH:(i + 1) * XATTN_DH], gk) for i in range(XATTN_HEADS)]
    k_ref[...] = jnp.concatenate(ks, axis=1)
    v_ref[...] = kv[:, XATTN_WIDTH:]


def _memkv(mem, g, wkv_b, gk):
    m, d = mem.shape
    return pl.pallas_call(
        _memkv_kernel,
        out_shape=[jax.ShapeDtypeStruct((m, XATTN_WIDTH), F32),
                   jax.ShapeDtypeStruct((m, XATTN_WIDTH), F32)],
        compiler_params=pltpu.CompilerParams(vmem_limit_bytes=VMEM_LIMIT),
        name="memkv",
    )(mem, g.reshape(1, d), wkv_b, gk.reshape(1, -1))


def _outx_kernel(x1_ref, og_ref, om_ref, wout_ref, gx_ref, wq_ref, gq_ref, kx_ref, vx_ref, wo_ref,
                 o_ref):
    x2 = (x1_ref[...]
          + _dot(og_ref[...].astype(BF16), wout_ref[:GLA_WIDTH, :])
          + _dot(om_ref[...].astype(BF16), wout_ref[GLA_WIDTH:, :]))
    h = _rms(x2, gx_ref[...]).astype(BF16)
    q = _dot(h, wq_ref[...])
    gq = gq_ref[...]
    scale = XATTN_DH ** -0.5
    heads = []
    for i in range(XATTN_HEADS):
        cols = slice(i * XATTN_DH, (i + 1) * XATTN_DH)
        qh = _rms(q[:, cols], gq).astype(BF16)
        s = _dot_nt(qh, kx_ref[:, cols].astype(BF16)) * scale
        p = jnp.exp(s - jnp.max(s, axis=-1, keepdims=True))
        p = p * (1.0 / jnp.sum(p, axis=-1, keepdims=True))
        heads.append(_dot(p.astype(BF16), vx_ref[:, cols].astype(BF16)))
    o = jnp.concatenate(heads, axis=1).astype(BF16)
    o_ref[...] = x2 + _dot(o, wo_ref[...])


def _outx(x1, og, om, wout_b, gx, wq_b, gq, kx, vx, wo_b, seq, mem_len):
    n, d = x1.shape
    tiles_per_seq = seq // OUTX_TM
    const = lambda i: (0, 0)
    return pl.pallas_call(
        _outx_kernel,
        grid=(n // OUTX_TM,),
        in_specs=[
            pl.BlockSpec((OUTX_TM, d), lambda i: (i, 0)),
            pl.BlockSpec((OUTX_TM, GLA_WIDTH), lambda i: (i, 0)),
            pl.BlockSpec((OUTX_TM, MOBA_WIDTH), lambda i: (i, 0)),
            pl.BlockSpec((GLA_WIDTH + MOBA_WIDTH, d), const),
            pl.BlockSpec((1, d), const),
            pl.BlockSpec((d, XATTN_WIDTH), const),
            pl.BlockSpec((1, XATTN_DH), const),
            pl.BlockSpec((mem_len, XATTN_WIDTH), lambda i: (i // tiles_per_seq, 0)),
            pl.BlockSpec((mem_len, XATTN_WIDTH), lambda i: (i // tiles_per_seq, 0)),
            pl.BlockSpec((XATTN_WIDTH, d), const),
        ],
        out_specs=pl.BlockSpec((OUTX_TM, d), lambda i: (i, 0)),
        out_shape=jax.ShapeDtypeStruct((n, d), F32),
        compiler_params=_params("parallel"),
        name="outx",
    )(x1, og, om, wout_b, gx.reshape(1, d), wq_b, gq.reshape(1, -1), kx, vx, wo_b)


def kernel(x, mem, ffn1_norm, ffn1_w_gate, ffn1_w_up, ffn1_w_down, mix_norm, w_in, gla_w_gate2, gla_b_gate2, gla_out_norm, moba_q_norm, moba_k_norm, w_out, xattn_norm, mem_norm, xattn_w_q, xattn_w_kv, xattn_w_o, xattn_q_norm, xattn_k_norm, ffn2_norm, ffn2_w_gate, ffn2_w_up, ffn2_w_down):
    batch, seq, d = x.shape
    mem_len = mem.shape[1]
    depth = ffn1_norm.shape[0]
    n = batch * seq
    assert seq % GLA_TC == 0 and seq % (2 * MOBA_BLOCK) == 0 and seq % OUTX_TM == 0
    assert seq // MOBA_BLOCK + MOBA_DH <= MXU_DIM and seq // MOBA_BLOCK == 16
    assert n % FFN_TM == 0 and n % NORM_TM == 0 and n % PROJ_TM == 0 and PROJ_TM % MOBA_BLOCK == 0

    xf = x.reshape(n, d)
    memf = mem.reshape(batch * mem_len, d)
    lr0 = GLA_COLS
    mq0 = lr0 + GLA_GATE_RANK
    w_in_t = jnp.swapaxes(w_in, 1, 2)
    for l in range(depth):
        w_lr_t = w_in_t[l, lr0:mq0]

        x1 = _ffn(xf, ffn1_norm[l], ffn1_w_gate[l], ffn1_w_up[l], ffn1_w_down[l])
        hn, la = _mixnorm(x1, mix_norm[l], w_lr_t, gla_w_gate2[l], gla_b_gate2[l])
        u = _projg(hn, w_in_t, l)
        q_t, kb, kmean, v3 = _proj_moba(hn, w_in_t, l, mq0, moba_q_norm[l], moba_k_norm[l])
        o_gla = _gla(u, la, gla_out_norm[l], batch, seq)
        o_moba = _moba(q_t, kb, kmean, v3, batch, seq)
        kx, vx = _memkv(memf, mem_norm[l], xattn_w_kv[l], xattn_k_norm[l])
        x3 = _outx(x1, o_gla, o_moba, w_out[l].astype(BF16), xattn_norm[l],
                   xattn_w_q[l].astype(BF16), xattn_q_norm[l], kx, vx,
                   xattn_w_o[l].astype(BF16), seq, mem_len)
        xf = _ffn(x3, ffn2_norm[l], ffn2_w_gate[l], ffn2_w_up[l], ffn2_w_down[l])
    return xf.reshape(batch, seq, d)
```

```python
import jax
import jax.numpy as jnp
from jax import lax
from jax.experimental import pallas as pl
from jax.experimental.pallas import tpu as pltpu

F32 = jnp.float32
BF16 = jnp.bfloat16

EPS = 1e-6
NEG_INF = -1e30
D_FF = 5632
GLA_HEADS = 4
GLA_DV = 256
GLA_DK = 128
GLA_QK = GLA_HEADS * GLA_DK
GLA_WIDTH = GLA_HEADS * GLA_DV
GLA_COLS = 2 * GLA_QK + 2 * GLA_WIDTH
GLA_GATE_RANK = 16
GLA_GATE_TAU = 16.0
GLA_CHUNK = 64
MOBA_DH = 128
MOBA_HEADS = 8
MOBA_WIDTH = MOBA_HEADS * MOBA_DH
MOBA_BLOCK = 256
MOBA_TOPK = 3
XATTN_HEADS = 4
XATTN_DH = 128
XATTN_WIDTH = XATTN_HEADS * XATTN_DH

V7X_VMEM_BYTES = 64 * 1024 * 1024
VMEM_LIMIT = V7X_VMEM_BYTES - 6 * 1024 * 1024
FFN_VMEM_LIMIT = V7X_VMEM_BYTES - 2 * 1024 * 1024
MXU_DIM = 256

FFN_TM = 1024
FFN_TF = 512
FFN_NORM_STEPS = 8
FFN_NORM_FIRST = 3
NORM_TM = 1024
PROJ_TM = 2048
PROJ_TN = 512
GLA_TC = 512
GLA_SUB = 256
GLA_G = 4
MOBA_G = 4
MOBA_TQ = 2 * MOBA_BLOCK
MOBA_VPAD = 16
MOBA_VROWS = MOBA_DH + MOBA_VPAD
OUTX_TM = 512


def _params(*semantics):
    return pltpu.CompilerParams(dimension_semantics=semantics, vmem_limit_bytes=VMEM_LIMIT)


def _rms(x, g):
    return x * lax.rsqrt(jnp.mean(x * x, axis=-1, keepdims=True) + EPS) * g


def _dot(a, b):
    return jnp.dot(a, b, preferred_element_type=F32)


def _dot_nt(a, b):
    return lax.dot_general(a, b, (((1,), (1,)), ((), ())), preferred_element_type=F32)


def _silu(x):
    return x * (1.0 / (1.0 + jnp.exp(-x)))


def _ffn_kernel(x_hbm, g_ref, wg_ref, wu_ref, wd_ref, o_hbm, ha_ref, hb_ref, xbuf_ref, acc_ref, xsem, osem):
    i = pl.program_id(0)
    f = pl.program_id(1)
    last_i = pl.num_programs(0) - 1
    last_f = pl.num_programs(1) - 1
    chunk = FFN_TM // FFN_NORM_STEPS

    def x_copy(tile):
        return pltpu.make_async_copy(x_hbm.at[pl.ds(tile * FFN_TM, FFN_TM)], xbuf_ref, xsem)

    def o_copy(tile):
        return pltpu.make_async_copy(acc_ref, o_hbm.at[pl.ds(tile * FFN_TM, FFN_TM)], osem)

    @pl.when(jnp.logical_and(i == 0, f == 0))
    def _():
        x_copy(0).start()
        x_copy(0).wait()
        ha_ref[...] = _rms(xbuf_ref[...], g_ref[...]).astype(BF16)

    @pl.when(f == 0)
    def _():
        @pl.when(i > 0)
        def _():
            o_copy(i - 1).wait()

        acc_ref[...] = xbuf_ref[...]

    @pl.when(jnp.logical_and(f == 1, i < last_i))
    def _():
        x_copy(i + 1).start()

    @pl.when(jnp.logical_and(f == FFN_NORM_FIRST, i < last_i))
    def _():
        x_copy(i + 1).wait()

    def step(h_ref, h_next_ref, with_norm):
        h = h_ref[...]
        gate = _dot(h, wg_ref[...].astype(BF16))
        up = _dot(h, wu_ref[...].astype(BF16))
        act = (0.5 * _silu(gate) * up).astype(BF16)
        acc_ref[...] += _dot(act, wd_ref[...].astype(BF16))
        if with_norm:
            rows = pl.ds(pl.multiple_of((f - FFN_NORM_FIRST) * chunk, chunk), chunk)
            h_next_ref[rows, :] = _rms(xbuf_ref[rows, :], g_ref[...]).astype(BF16)

    even = i % 2 == 0
    norm_step = f >= FFN_NORM_FIRST
    for h_ref, h_next_ref, parity in ((ha_ref, hb_ref, even), (hb_ref, ha_ref, jnp.logical_not(even))):
        @pl.when(jnp.logical_and(parity, norm_step))
        def _():
            step(h_ref, h_next_ref, True)

        @pl.when(jnp.logical_and(parity, jnp.logical_not(norm_step)))
        def _():
            step(h_ref, h_next_ref, False)

    @pl.when(f == last_f)
    def _():
        o_copy(i).start()

        @pl.when(i == last_i)
        def _():
            o_copy(i).wait()


def _ffn(x, g, wg, wu, wd):
    n, d = x.shape
    grid = (n // FFN_TM, D_FF // FFN_TF)
    assert grid[1] == FFN_NORM_FIRST + FFN_NORM_STEPS
    return pl.pallas_call(
        _ffn_kernel,
        grid=grid,
        in_specs=[
            pl.BlockSpec(memory_space=pl.ANY),
            pl.BlockSpec((1, d), lambda i, f: (0, 0)),
            pl.BlockSpec((d, FFN_TF), lambda i, f: (0, f)),
            pl.BlockSpec((d, FFN_TF), lambda i, f: (0, f)),
            pl.BlockSpec((FFN_TF, d), lambda i, f: (f, 0)),
        ],
        out_specs=pl.BlockSpec(memory_space=pl.ANY),
        out_shape=jax.ShapeDtypeStruct((n, d), F32),
        scratch_shapes=[pltpu.VMEM((FFN_TM, d), BF16), pltpu.VMEM((FFN_TM, d), BF16),
                        pltpu.VMEM((FFN_TM, d), F32), pltpu.VMEM((FFN_TM, d), F32),
                        pltpu.SemaphoreType.DMA, pltpu.SemaphoreType.DMA],
        compiler_params=pltpu.CompilerParams(dimension_semantics=("arbitrary", "arbitrary"),
                                             vmem_limit_bytes=FFN_VMEM_LIMIT),
        name="ffn",
    )(x, g.reshape(1, d), wg, wu, wd)


def _mixnorm_kernel(x_ref, g_ref, wlr_ref, wg2_ref, bg2_ref, hn_ref, la_ref):
    h = _rms(x_ref[...], g_ref[...]).astype(BF16)
    hn_ref[...] = h
    lr = _dot_nt(h, wlr_ref[...].astype(BF16))
    pre = _dot(lr.astype(BF16), wg2_ref[...].astype(BF16)) + bg2_ref[...]
    log_sig = -(jnp.maximum(-pre, 0.0) + jnp.log(1.0 + jnp.exp(-jnp.abs(pre))))
    la_ref[...] = log_sig * (1.0 / GLA_GATE_TAU)


def _mixnorm(x1, g, w_lr_t, wg2, bg2):
    n, d = x1.shape
    const = lambda i: (0, 0)
    return pl.pallas_call(
        _mixnorm_kernel,
        grid=(n // NORM_TM,),
        in_specs=[
            pl.BlockSpec((NORM_TM, d), lambda i: (i, 0)),
            pl.BlockSpec((1, d), const),
            pl.BlockSpec((GLA_GATE_RANK, d), const),
            pl.BlockSpec((GLA_GATE_RANK, GLA_QK), const),
            pl.BlockSpec((1, GLA_QK), const),
        ],
        out_specs=[pl.BlockSpec((NORM_TM, d), lambda i: (i, 0)),
                   pl.BlockSpec((NORM_TM, GLA_QK), lambda i: (i, 0))],
        out_shape=[jax.ShapeDtypeStruct((n, d), BF16),
                   jax.ShapeDtypeStruct((n, GLA_QK), F32)],
        compiler_params=_params("parallel"),
        name="mixnorm",
    )(x1, g.reshape(1, d), w_lr_t, wg2, bg2.reshape(1, -1))


def _projg_kernel(hn_ref, wt_ref, u_ref):
    u_ref[...] = _dot_nt(hn_ref[...], wt_ref[0].astype(BF16))


def _projq_kernel(hn_ref, wt_ref, g_ref, qt_ref):
    yt = _dot_nt(wt_ref[0].astype(BF16), hn_ref[...])
    g = g_ref[...]
    parts = []
    for k in range(PROJ_TN // MOBA_DH):
        p = yt[k * MOBA_DH:(k + 1) * MOBA_DH, :]
        inv = lax.rsqrt(jnp.mean(p * p, axis=0, keepdims=True) + EPS)
        parts.append(p * inv * g)
    qt_ref[...] = jnp.concatenate(parts, axis=0)


def _projk_kernel(hn_ref, wt_ref, g_ref, kb_ref, kmean_ref):
    y = _dot_nt(hn_ref[...], wt_ref[0].astype(BF16))
    g = g_ref[...]
    parts = [_rms(y[:, k * MOBA_DH:(k + 1) * MOBA_DH], g) for k in range(PROJ_TN // MOBA_DH)]
    kn = jnp.concatenate(parts, axis=1)
    kb_ref[...] = kn.astype(BF16)
    blocks = PROJ_TM // MOBA_BLOCK
    kmean_ref[0] = jnp.sum(kn.reshape(blocks, MOBA_BLOCK, PROJ_TN), axis=1) * (1.0 / MOBA_BLOCK)


def _projv_kernel(hn_ref, wt_ref, v3_ref):
    yt = _dot_nt(wt_ref[0].astype(BF16), hn_ref[...]).astype(BF16)
    ones = jnp.ones((MOBA_VPAD, PROJ_TM), BF16)
    parts = []
    for k in range(PROJ_TN // MOBA_DH):
        parts += [yt[k * MOBA_DH:(k + 1) * MOBA_DH], ones]
    ya = jnp.concatenate(parts, axis=0)
    for t in range(PROJ_TM // MOBA_BLOCK):
        v3_ref[t] = ya[:, t * MOBA_BLOCK:(t + 1) * MOBA_BLOCK]


def _proj_kernel(hn_ref, wt_ref, gq_ref, gk_ref, u_ref, qt_ref, kb_ref, kmean_ref, v3_ref):
    j = pl.program_id(1)
    gla = GLA_COLS // PROJ_TN
    steps = MOBA_WIDTH // PROJ_TN

    @pl.when(j < gla)
    def _():
        _projg_kernel(hn_ref, wt_ref, u_ref)

    @pl.when(jnp.logical_and(j >= gla, j < gla + steps))
    def _():
        _projq_kernel(hn_ref, wt_ref, gq_ref, qt_ref)

    @pl.when(jnp.logical_and(j >= gla + steps, j < gla + 2 * steps))
    def _():
        _projk_kernel(hn_ref, wt_ref, gk_ref, kb_ref, kmean_ref)

    @pl.when(j >= gla + 2 * steps)
    def _():
        _projv_kernel(hn_ref, wt_ref, v3_ref)


def _proj(hn, w_in_t, layer, gq, gk):
    n, d = hn.shape
    gla = GLA_COLS // PROJ_TN
    steps = MOBA_WIDTH // PROJ_TN
    blocks = PROJ_TM // MOBA_BLOCK
    moba_gap = GLA_GATE_RANK
    own = lambda j, first: jnp.clip(j - gla - first * steps, 0, steps - 1)
    const = lambda i, j: (0, 0)
    u, q_t, kb, kmean, v3 = pl.pallas_call(
        _proj_kernel,
        grid=(n // PROJ_TM, gla + 3 * steps),
        in_specs=[
            pl.BlockSpec((PROJ_TM, d), lambda i, j: (i, 0)),
            pl.BlockSpec((pl.Element(1), pl.Element(PROJ_TN), pl.Element(d)),
                         lambda i, j: (layer,
                                       pl.multiple_of(j * PROJ_TN + jnp.where(j >= gla, moba_gap, 0), 8), 0)),
            pl.BlockSpec((MOBA_DH, 1), const),
            pl.BlockSpec((1, MOBA_DH), const),
        ],
        out_specs=[
            pl.BlockSpec((PROJ_TM, PROJ_TN), lambda i, j: (i, jnp.minimum(j, gla - 1))),
            pl.BlockSpec((PROJ_TN, PROJ_TM), lambda i, j: (own(j, 0), i)),
            pl.BlockSpec((PROJ_TM, PROJ_TN), lambda i, j: (i, own(j, 1))),
            pl.BlockSpec((1, blocks, PROJ_TN), lambda i, j: (i, 0, own(j, 1))),
            pl.BlockSpec((blocks, PROJ_TN // MOBA_DH * MOBA_VROWS, MOBA_BLOCK),
                         lambda i, j: (i, own(j, 2), 0)),
        ],
        out_shape=[jax.ShapeDtypeStruct((n, GLA_COLS), F32),
                   jax.ShapeDtypeStruct((MOBA_WIDTH, n), F32),
                   jax.ShapeDtypeStruct((n, MOBA_WIDTH), BF16),
                   jax.ShapeDtypeStruct((n // PROJ_TM, blocks, MOBA_WIDTH), F32),
                   jax.ShapeDtypeStruct((n // MOBA_BLOCK, MOBA_HEADS * MOBA_VROWS, MOBA_BLOCK), BF16)],
        compiler_params=_params("arbitrary", "arbitrary"),
        name="proj",
    )(hn, w_in_t, gq.reshape(-1, 1), gk.reshape(1, -1))
    return u, q_t, kb, kmean.reshape(n // MOBA_BLOCK, MOBA_WIDTH), v3


def _dot_exact_rhs(m, a):
    a1 = a.astype(BF16)
    r1 = a - a1.astype(F32)
    a2 = r1.astype(BF16)
    a3 = (r1 - a2.astype(F32)).astype(BF16)
    return _dot(m, a1) + _dot(m, a2) + _dot(m, a3)


def _gla_head(q, k, v, r, la, g_out, state, upd_ref):
    tc = GLA_TC
    nchunk = tc // GLA_CHUNK
    sub = GLA_SUB

    crow = lax.broadcasted_iota(jnp.int32, (GLA_CHUNK, GLA_CHUNK), 0)
    ccol = lax.broadcasted_iota(jnp.int32, (GLA_CHUNK, GLA_CHUNK), 1)
    tri = (ccol <= crow).astype(BF16)
    la_wide = jnp.concatenate([la[c * GLA_CHUNK:(c + 1) * GLA_CHUNK] for c in range(nchunk)], axis=1)
    bc_wide = _dot_exact_rhs(tri, la_wide)
    bcum = jnp.concatenate([bc_wide[:, c * GLA_DK:(c + 1) * GLA_DK] for c in range(nchunk)], axis=0)
    btot = jnp.concatenate(
        [jnp.broadcast_to(bc_wide[GLA_CHUNK - 1:GLA_CHUNK, c * GLA_DK:(c + 1) * GLA_DK],
                          (GLA_CHUNK, GLA_DK)) for c in range(nchunk)], axis=0)

    q_dec_b = ((q * (GLA_DK ** -0.5)) * jnp.exp(bcum)).astype(BF16)
    k_inv_b = (k * jnp.exp(-bcum)).astype(BF16)
    k_tail_t = (k * jnp.exp(btot - bcum)).T.astype(BF16)
    decay_t = jnp.exp(btot).T
    vb = v.astype(BF16)

    row = lax.broadcasted_iota(jnp.int32, (sub, sub), 0)
    col = lax.broadcasted_iota(jnp.int32, (sub, sub), 1)
    causal = jnp.logical_and((row // GLA_CHUNK) == (col // GLA_CHUNK), col <= row)
    intra = []
    for p in range(tc // sub):
        rows = slice(p * sub, (p + 1) * sub)
        att = jnp.where(causal, _dot_nt(q_dec_b[rows], k_inv_b[rows]), 0.0)
        intra.append(_dot(att.astype(BF16), vb[rows]))

    for c in range(nchunk):
        rows = slice(c * GLA_CHUNK, (c + 1) * GLA_CHUNK)
        upd_ref[c] = _dot(k_tail_t[:, rows], vb[rows])

    inter = []
    for c in range(nchunk):
        rows = slice(c * GLA_CHUNK, (c + 1) * GLA_CHUNK)
        inter.append(_dot(q_dec_b[rows], state.astype(BF16)))
        dec_c = decay_t[:, c * GLA_CHUNK:c * GLA_CHUNK + 1]
        state = dec_c * state + upd_ref[c]
    o = jnp.concatenate(intra, axis=0) + jnp.concatenate(inter, axis=0)
    return _rms(o, g_out) * _silu(r), state


def _gla_kernel(q_ref, k_ref, v_ref, r_ref, la_ref, go_ref, o_ref, s_ref, upd_ref):
    @pl.when(pl.program_id(2) == 0)
    def _():
        s_ref[...] = jnp.zeros_like(s_ref)

    for g in range(GLA_G):
        qk = slice(g * GLA_DK, (g + 1) * GLA_DK)
        vr = slice(g * GLA_DV, (g + 1) * GLA_DV)
        o, state = _gla_head(q_ref[:, qk], k_ref[:, qk], v_ref[:, vr], r_ref[:, vr], la_ref[:, qk],
                             go_ref[...], s_ref[g], upd_ref.at[g])
        s_ref[g] = state
        o_ref[:, vr] = o


def _gla(u, la, g_out, batch, seq):
    n = u.shape[0]
    nt = seq // GLA_TC
    tok = lambda b, h, t: b * nt + t
    qk_w, v_w = GLA_G * GLA_DK, GLA_G * GLA_DV
    k_blk0 = GLA_QK // qk_w
    v_blk0 = 2 * GLA_QK // v_w
    r_blk0 = v_blk0 + GLA_WIDTH // v_w
    return pl.pallas_call(
        _gla_kernel,
        grid=(batch, GLA_HEADS // GLA_G, nt),
        in_specs=[
            pl.BlockSpec((GLA_TC, qk_w), lambda b, h, t: (tok(b, h, t), h)),
            pl.BlockSpec((GLA_TC, qk_w), lambda b, h, t: (tok(b, h, t), k_blk0 + h)),
            pl.BlockSpec((GLA_TC, v_w), lambda b, h, t: (tok(b, h, t), v_blk0 + h)),
            pl.BlockSpec((GLA_TC, v_w), lambda b, h, t: (tok(b, h, t), r_blk0 + h)),
            pl.BlockSpec((GLA_TC, qk_w), lambda b, h, t: (tok(b, h, t), h)),
            pl.BlockSpec((1, GLA_DV), lambda b, h, t: (0, 0)),
        ],
        out_specs=pl.BlockSpec((GLA_TC, v_w), lambda b, h, t: (tok(b, h, t), h)),
        out_shape=jax.ShapeDtypeStruct((n, GLA_WIDTH), F32),
        scratch_shapes=[pltpu.VMEM((GLA_G, GLA_DK, GLA_DV), F32),
                        pltpu.VMEM((GLA_G, GLA_TC // GLA_CHUNK, GLA_DK, GLA_DV), F32)],
        compiler_params=_params("parallel", "parallel", "arbitrary"),
        name="gla",
    )(u, u, u, u, la, g_out.reshape(1, -1))


def _moba_kernel(qt_ref, kb_ref, v3_ref, kmean_ref, o_ref, kaug_ref, snext_ref):
    ti = pl.program_id(2)
    last_tile = pl.num_programs(2) - 1
    bs = MOBA_BLOCK
    tq = MOBA_TQ
    nb = kb_ref.shape[0] // bs
    dh = MOBA_DH
    vr = MOBA_VROWS
    scale = dh ** -0.5

    @pl.when(ti == 0)
    def _():
        seq = kb_ref.shape[0]
        row_blk = lax.broadcasted_iota(jnp.int32, (seq, MXU_DIM - dh), 0) // bs
        lane = lax.broadcasted_iota(jnp.int32, (seq, MXU_DIM - dh), 1)
        onehot = (row_blk == lane).astype(BF16)
        for g in range(MOBA_G):
            kaug_ref[g, :, :dh] = kb_ref[:, g * dh:(g + 1) * dh]
            kaug_ref[g, :, dh:] = onehot

    blk = lax.broadcasted_iota(jnp.int32, (nb, tq), 0)
    q_blk = 2 * ti + (lax.broadcasted_iota(jnp.int32, (nb, tq), 1) // bs)
    past = blk < q_blk
    key_row = lax.broadcasted_iota(jnp.int32, (tq, tq), 0)
    qry_col = lax.broadcasted_iota(jnp.int32, (tq, tq), 1)
    causal = key_row <= qry_col
    same_blk = (key_row // bs) == (qry_col // bs)
    diag = pl.multiple_of(ti * tq, tq)

    def pv_dot(g, t, p):
        return (_dot(v3_ref[2 * t, g * vr:(g + 1) * vr, :], p[:bs])
                + _dot(v3_ref[2 * t + 1, g * vr:(g + 1) * vr, :], p[bs:]))

    q_aug, state = [], []
    for g in range(MOBA_G):
        qt = qt_ref[g * dh:(g + 1) * dh, :]
        gate = jnp.dot(kmean_ref[:, g * dh:(g + 1) * dh], qt,
                       precision=lax.Precision.HIGHEST, preferred_element_type=F32)
        gate = jnp.where(past, gate, NEG_INF)
        rank = jnp.zeros((nb, tq), jnp.int32)
        for j in range(nb):
            gj = gate[j:j + 1, :]
            ahead = jnp.logical_or(gj > gate, jnp.logical_and(gj == gate, blk > j))
            rank = rank + ahead.astype(jnp.int32)
        chosen = jnp.logical_and(past, rank < MOBA_TOPK)
        bias = jnp.where(chosen, 0.0, NEG_INF)
        qa = jnp.concatenate([(qt * scale).astype(BF16), bias.astype(BF16),
                              jnp.zeros((MXU_DIM - dh - nb, tq), BF16)], axis=0)
        q_aug.append(qa)

        first_chosen = jnp.sum(jnp.where(blk == 2 * ti, chosen.astype(F32), 0.0),
                               axis=0, keepdims=True) > 0.5
        allowed = jnp.logical_and(causal, jnp.logical_or(same_blk, first_chosen))
        s = _dot(kaug_ref[g, pl.ds(diag, tq), :dh], qa[:dh])
        s = jnp.where(allowed, s, NEG_INF)
        m = jnp.max(s, axis=0, keepdims=True)
        p = jnp.exp(s - m).astype(BF16)
        state.append((m, pv_dot(g, ti, p)))

    def score(g, t):
        off = pl.multiple_of(t * tq, tq)
        s = _dot(kaug_ref[g, pl.ds(off, tq), :], q_aug[g])
        return s, jnp.max(s, axis=0, keepdims=True)

    def consume(g, t, s, tile_max, m, acc, valid=None):
        if valid is not None:
            tile_max = jnp.where(valid, tile_max, NEG_INF)
        m_new = jnp.maximum(m, tile_max)
        alpha = jnp.exp(m - m_new)
        shift = m_new if valid is None else jnp.where(valid, m_new, -NEG_INF)
        p = jnp.exp(s - shift).astype(BF16)
        return m_new, alpha * acc + pv_dot(g, t, p)

    carry0 = []
    for g in range(MOBA_G):
        s, tile_max = score(g, 0)
        snext_ref[g] = s
        carry0.append(state[g] + (tile_max,))

    def body(u, carry):
        out = []
        for g in range(MOBA_G):
            m, acc, max_a = carry[g]
            s_b, max_b = score(g, 2 * u + 1)
            m, acc = consume(g, 2 * u, snext_ref[g], max_a, m, acc)
            s_c, max_c = score(g, jnp.minimum(2 * u + 2, last_tile))
            snext_ref[g] = s_c
            m, acc = consume(g, 2 * u + 1, s_b, max_b, m, acc, valid=2 * u + 1 < ti)
            out.append((m, acc, max_c))
        return tuple(out)

    carry = lax.fori_loop(0, (ti + 1) // 2, body, tuple(carry0))
    for g in range(MOBA_G):
        m, acc, _ = carry[g]
        o_ref[:, g * dh:(g + 1) * dh] = (acc[:dh] * (1.0 / acc[dh:dh + 1])).T


def _moba(q_t, kb, kmean, v3, batch, seq):
    n = kb.shape[0]
    nb = seq // MOBA_BLOCK
    nt = seq // MOBA_TQ
    gw = MOBA_G * MOBA_DH
    return pl.pallas_call(
        _moba_kernel,
        grid=(batch, MOBA_HEADS // MOBA_G, nt),
        in_specs=[
            pl.BlockSpec((gw, MOBA_TQ), lambda b, h, i: (h, b * nt + i)),
            pl.BlockSpec((seq, gw), lambda b, h, i: (b, h)),
            pl.BlockSpec((nb, MOBA_G * MOBA_VROWS, MOBA_BLOCK), lambda b, h, i: (b, h, 0)),
            pl.BlockSpec((nb, gw), lambda b, h, i: (b, h)),
        ],
        out_specs=pl.BlockSpec((MOBA_TQ, gw), lambda b, h, i: (b * nt + i, h)),
        out_shape=jax.ShapeDtypeStruct((n, MOBA_WIDTH), F32),
        scratch_shapes=[pltpu.VMEM((MOBA_G, seq, MXU_DIM), BF16),
                        pltpu.VMEM((MOBA_G, MOBA_TQ, MOBA_TQ), F32)],
        compiler_params=_params("parallel", "parallel", "arbitrary"),
        name="moba",
    )(q_t, kb, v3, kmean)


def _memkv_kernel(mem_ref, g_ref, wkv_ref, gk_ref, k_ref, v_ref):
    h = _rms(mem_ref[...], g_ref[...]).astype(BF16)
    kv = _dot(h, wkv_ref[...].astype(BF16))
    gk = gk_ref[...]
    ks = [_rms(kv[:, i * XATTN_DH:(i + 1) * XATTN_DH], gk) for i in range(XATTN_HEADS)]
    k_ref[...] = jnp.concatenate(ks, axis=1)
    v_ref[...] = kv[:, XATTN_WIDTH:]


def _memkv(mem, g, wkv_b, gk):
    m, d = mem.shape
    return pl.pallas_call(
        _memkv_kernel,
        out_shape=[jax.ShapeDtypeStruct((m, XATTN_WIDTH), F32),
                   jax.ShapeDtypeStruct((m, XATTN_WIDTH), F32)],
        compiler_params=pltpu.CompilerParams(vmem_limit_bytes=VMEM_LIMIT),
        name="memkv",
    )(mem, g.reshape(1, d), wkv_b, gk.reshape(1, -1))


def _outx_kernel(x1_ref, og_ref, om_ref, wout_ref, gx_ref, wq_ref, gq_ref, kx_ref, vx_ref, wo_ref,
                 o_ref):
    x2 = (x1_ref[...]
          + _dot(og_ref[...].astype(BF16), wout_ref[:GLA_WIDTH, :])
          + _dot(om_ref[...].astype(BF16), wout_ref[GLA_WIDTH:, :]))
    h = _rms(x2, gx_ref[...]).astype(BF16)
    q = _dot(h, wq_ref[...])
    gq = gq_ref[...]
    scale = XATTN_DH ** -0.5
    heads = []
    for i in range(XATTN_HEADS):
        cols = slice(i * XATTN_DH, (i + 1) * XATTN_DH)
        qh = _rms(q[:, cols], gq).astype(BF16)
        s = _dot_nt(qh, kx_ref[:, cols].astype(BF16)) * scale
        p = jnp.exp(s - jnp.max(s, axis=-1, keepdims=True))
        p = p * (1.0 / jnp.sum(p, axis=-1, keepdims=True))
        heads.append(_dot(p.astype(BF16), vx_ref[:, cols].astype(BF16)))
    o = jnp.concatenate(heads, axis=1).astype(BF16)
    o_ref[...] = x2 + _dot(o, wo_ref[...])


def _outx(x1, og, om, wout_b, gx, wq_b, gq, kx, vx, wo_b, seq, mem_len):
    n, d = x1.shape
    tiles_per_seq = seq // OUTX_TM
    const = lambda i: (0, 0)
    return pl.pallas_call(
        _outx_kernel,
        grid=(n // OUTX_TM,),
        in_specs=[
            pl.BlockSpec((OUTX_TM, d), lambda i: (i, 0)),
            pl.BlockSpec((OUTX_TM, GLA_WIDTH), lambda i: (i, 0)),
            pl.BlockSpec((OUTX_TM, MOBA_WIDTH), lambda i: (i, 0)),
            pl.BlockSpec((GLA_WIDTH + MOBA_WIDTH, d), const),
            pl.BlockSpec((1, d), const),
            pl.BlockSpec((d, XATTN_WIDTH), const),
            pl.BlockSpec((1, XATTN_DH), const),
            pl.BlockSpec((mem_len, XATTN_WIDTH), lambda i: (i // tiles_per_seq, 0)),
            pl.BlockSpec((mem_len, XATTN_WIDTH), lambda i: (i // tiles_per_seq, 0)),
            pl.BlockSpec((XATTN_WIDTH, d), const),
        ],
        out_specs=pl.BlockSpec((OUTX_TM, d), lambda i: (i, 0)),
        out_shape=jax.ShapeDtypeStruct((n, d), F32),
        compiler_params=_params("parallel"),
        name="outx",
    )(x1, og, om, wout_b, gx.reshape(1, d), wq_b, gq.reshape(1, -1), kx, vx, wo_b)


def kernel(x, mem, ffn1_norm, ffn1_w_gate, ffn1_w_up, ffn1_w_down, mix_norm, w_in, gla_w_gate2, gla_b_gate2, gla_out_norm, moba_q_norm, moba_k_norm, w_out, xattn_norm, mem_norm, xattn_w_q, xattn_w_kv, xattn_w_o, xattn_q_norm, xattn_k_norm, ffn2_norm, ffn2_w_gate, ffn2_w_up, ffn2_w_down):
    batch, seq, d = x.shape
    mem_len = mem.shape[1]
    depth = ffn1_norm.shape[0]
    n = batch * seq
    assert seq % GLA_TC == 0 and seq % (2 * MOBA_BLOCK) == 0 and seq % OUTX_TM == 0
    assert seq // MOBA_BLOCK + MOBA_DH <= MXU_DIM and seq // MOBA_BLOCK == 16
    assert n % FFN_TM == 0 and n % NORM_TM == 0 and n % PROJ_TM == 0 and PROJ_TM % MOBA_BLOCK == 0

    xf = x.reshape(n, d)
    memf = mem.reshape(batch * mem_len, d)
    lr0 = GLA_COLS
    mq0 = lr0 + GLA_GATE_RANK
    w_in_t = jnp.swapaxes(w_in, 1, 2)
    for l in range(depth):
        w_lr_t = w_in_t[l, lr0:mq0]

        x1 = _ffn(xf, ffn1_norm[l], ffn1_w_gate[l], ffn1_w_up[l], ffn1_w_down[l])
        hn, la = _mixnorm(x1, mix_norm[l], w_lr_t, gla_w_gate2[l], gla_b_gate2[l])
        u, q_t, kb, kmean, v3 = _proj(hn, w_in_t, l, moba_q_norm[l], moba_k_norm[l])
        o_gla = _gla(u, la, gla_out_norm[l], batch, seq)
        o_moba = _moba(q_t, kb, kmean, v3, batch, seq)
        kx, vx = _memkv(memf, mem_norm[l], xattn_w_kv[l], xattn_k_norm[l])
        x3 = _outx(x1, o_gla, o_moba, w_out[l].astype(BF16), xattn_norm[l],
                   xattn_w_q[l].astype(BF16), xattn_q_norm[l], kx, vx,
                   xattn_w_o[l].astype(BF16), seq, mem_len)
        xf = _ffn(x3, ffn2_norm[l], ffn2_w_gate[l], ffn2_w_up[l], ffn2_w_down[l])
    return xf.reshape(batch, seq, d)
```

```python
import jax
import jax.numpy as jnp
from jax import lax
from jax.experimental import pallas as pl
from jax.experimental.pallas import tpu as pltpu

F32 = jnp.float32
BF16 = jnp.bfloat16

EPS = 1e-6
NEG_INF = -1e30
D_FF = 5632
GLA_HEADS = 4
GLA_DV = 256
GLA_DK = 128
GLA_QK = GLA_HEADS * GLA_DK
GLA_WIDTH = GLA_HEADS * GLA_DV
GLA_COLS = 2 * GLA_QK + 2 * GLA_WIDTH
GLA_GATE_RANK = 16
GLA_GATE_TAU = 16.0
GLA_CHUNK = 64
MOBA_DH = 128
MOBA_HEADS = 8
MOBA_WIDTH = MOBA_HEADS * MOBA_DH
MOBA_BLOCK = 256
MOBA_TOPK = 3
XATTN_HEADS = 4
XATTN_DH = 128
XATTN_WIDTH = XATTN_HEADS * XATTN_DH

V7X_VMEM_BYTES = 64 * 1024 * 1024
VMEM_LIMIT = V7X_VMEM_BYTES - 6 * 1024 * 1024
FFN_VMEM_LIMIT = V7X_VMEM_BYTES - 2 * 1024 * 1024
MXU_DIM = 256

FFN_TM = 1024
FFN_TF = 512
NORM_TM = 1024
PROJ_TM = 2048
PROJ_TN = 512
GLA_TC = 512
GLA_SUB = 256
GLA_G = 4
MOBA_G = 4
MOBA_TQ = 2 * MOBA_BLOCK
MOBA_VPAD = 16
MOBA_VROWS = MOBA_DH + MOBA_VPAD
OUTX_TM = 512


def _params(*semantics):
    return pltpu.CompilerParams(dimension_semantics=semantics, vmem_limit_bytes=VMEM_LIMIT)


def _rms(x, g):
    return x * lax.rsqrt(jnp.mean(x * x, axis=-1, keepdims=True) + EPS) * g


def _dot(a, b):
    return jnp.dot(a, b, preferred_element_type=F32)


def _dot_nt(a, b):
    return lax.dot_general(a, b, (((1,), (1,)), ((), ())), preferred_element_type=F32)


def _silu(x):
    return x * (1.0 / (1.0 + jnp.exp(-x)))


def _ffn_kernel(x_hbm, g_ref, wg_ref, wu_ref, wd_ref, o_hbm, h_ref, xbuf_ref, acc_ref, xsem, osem):
    i = pl.program_id(0)
    f = pl.program_id(1)
    last_i = pl.num_programs(0) - 1
    last_f = pl.num_programs(1) - 1

    def x_copy(tile):
        return pltpu.make_async_copy(x_hbm.at[pl.ds(tile * FFN_TM, FFN_TM)], xbuf_ref, xsem)

    def o_copy(tile):
        return pltpu.make_async_copy(acc_ref, o_hbm.at[pl.ds(tile * FFN_TM, FFN_TM)], osem)

    @pl.when(jnp.logical_and(i == 0, f == 0))
    def _():
        x_copy(0).start()

    @pl.when(f == 0)
    def _():
        x_copy(i).wait()
        x = xbuf_ref[...]
        h_ref[...] = _rms(x, g_ref[...]).astype(BF16)

        @pl.when(i > 0)
        def _():
            o_copy(i - 1).wait()

        acc_ref[...] = xbuf_ref[...]

    @pl.when(jnp.logical_and(f == 1, i < last_i))
    def _():
        x_copy(i + 1).start()

    h = h_ref[...]
    gate = _dot(h, wg_ref[...].astype(BF16))
    up = _dot(h, wu_ref[...].astype(BF16))
    act = (0.5 * _silu(gate) * up).astype(BF16)
    acc_ref[...] += _dot(act, wd_ref[...].astype(BF16))

    @pl.when(f == last_f)
    def _():
        o_copy(i).start()

        @pl.when(i == last_i)
        def _():
            o_copy(i).wait()


def _ffn(x, g, wg, wu, wd):
    n, d = x.shape
    grid = (n // FFN_TM, D_FF // FFN_TF)
    assert grid[1] >= 2
    return pl.pallas_call(
        _ffn_kernel,
        grid=grid,
        in_specs=[
            pl.BlockSpec(memory_space=pl.ANY),
            pl.BlockSpec((1, d), lambda i, f: (0, 0)),
            pl.BlockSpec((d, FFN_TF), lambda i, f: (0, f)),
            pl.BlockSpec((d, FFN_TF), lambda i, f: (0, f)),
            pl.BlockSpec((FFN_TF, d), lambda i, f: (f, 0)),
        ],
        out_specs=pl.BlockSpec(memory_space=pl.ANY),
        out_shape=jax.ShapeDtypeStruct((n, d), F32),
        scratch_shapes=[pltpu.VMEM((FFN_TM, d), BF16), pltpu.VMEM((FFN_TM, d), F32),
                        pltpu.VMEM((FFN_TM, d), F32),
                        pltpu.SemaphoreType.DMA, pltpu.SemaphoreType.DMA],
        compiler_params=pltpu.CompilerParams(dimension_semantics=("arbitrary", "arbitrary"),
                                             vmem_limit_bytes=FFN_VMEM_LIMIT),
        name="ffn",
    )(x, g.reshape(1, d), wg, wu, wd)


def _mixnorm_kernel(x_ref, g_ref, wlr_ref, wg2_ref, bg2_ref, hn_ref, la_ref):
    h = _rms(x_ref[...], g_ref[...]).astype(BF16)
    hn_ref[...] = h
    lr = _dot_nt(h, wlr_ref[...].astype(BF16))
    pre = _dot(lr.astype(BF16), wg2_ref[...].astype(BF16)) + bg2_ref[...]
    log_sig = -(jnp.maximum(-pre, 0.0) + jnp.log(1.0 + jnp.exp(-jnp.abs(pre))))
    la_ref[...] = log_sig * (1.0 / GLA_GATE_TAU)


def _mixnorm(x1, g, w_lr_t, wg2, bg2):
    n, d = x1.shape
    const = lambda i: (0, 0)
    return pl.pallas_call(
        _mixnorm_kernel,
        grid=(n // NORM_TM,),
        in_specs=[
            pl.BlockSpec((NORM_TM, d), lambda i: (i, 0)),
            pl.BlockSpec((1, d), const),
            pl.BlockSpec((GLA_GATE_RANK, d), const),
            pl.BlockSpec((GLA_GATE_RANK, GLA_QK), const),
            pl.BlockSpec((1, GLA_QK), const),
        ],
        out_specs=[pl.BlockSpec((NORM_TM, d), lambda i: (i, 0)),
                   pl.BlockSpec((NORM_TM, GLA_QK), lambda i: (i, 0))],
        out_shape=[jax.ShapeDtypeStruct((n, d), BF16),
                   jax.ShapeDtypeStruct((n, GLA_QK), F32)],
        compiler_params=_params("parallel"),
        name="mixnorm",
    )(x1, g.reshape(1, d), w_lr_t, wg2, bg2.reshape(1, -1))


def _projg_kernel(hn_ref, wt_ref, u_ref):
    u_ref[...] = _dot_nt(hn_ref[...], wt_ref[0].astype(BF16))


def _projq_kernel(hn_ref, wt_ref, g_ref, qt_ref):
    yt = _dot_nt(wt_ref[0].astype(BF16), hn_ref[...])
    g = g_ref[...]
    parts = []
    for k in range(PROJ_TN // MOBA_DH):
        p = yt[k * MOBA_DH:(k + 1) * MOBA_DH, :]
        inv = lax.rsqrt(jnp.mean(p * p, axis=0, keepdims=True) + EPS)
        parts.append(p * inv * g)
    qt_ref[...] = jnp.concatenate(parts, axis=0)


def _projk_kernel(hn_ref, wt_ref, g_ref, kb_ref, kmean_ref):
    y = _dot_nt(hn_ref[...], wt_ref[0].astype(BF16))
    g = g_ref[...]
    parts = [_rms(y[:, k * MOBA_DH:(k + 1) * MOBA_DH], g) for k in range(PROJ_TN // MOBA_DH)]
    kn = jnp.concatenate(parts, axis=1)
    kb_ref[...] = kn.astype(BF16)
    blocks = PROJ_TM // MOBA_BLOCK
    kmean_ref[0] = jnp.sum(kn.reshape(blocks, MOBA_BLOCK, PROJ_TN), axis=1) * (1.0 / MOBA_BLOCK)


def _projv_kernel(hn_ref, wt_ref, v3_ref):
    yt = _dot_nt(wt_ref[0].astype(BF16), hn_ref[...]).astype(BF16)
    ones = jnp.ones((MOBA_VPAD, PROJ_TM), BF16)
    parts = []
    for k in range(PROJ_TN // MOBA_DH):
        parts += [yt[k * MOBA_DH:(k + 1) * MOBA_DH], ones]
    ya = jnp.concatenate(parts, axis=0)
    for t in range(PROJ_TM // MOBA_BLOCK):
        v3_ref[t] = ya[:, t * MOBA_BLOCK:(t + 1) * MOBA_BLOCK]


def _proj_kernel(hn_ref, wt_ref, gq_ref, gk_ref, u_ref, qt_ref, kb_ref, kmean_ref, v3_ref):
    j = pl.program_id(1)
    gla = GLA_COLS // PROJ_TN
    steps = MOBA_WIDTH // PROJ_TN

    @pl.when(j < gla)
    def _():
        _projg_kernel(hn_ref, wt_ref, u_ref)

    @pl.when(jnp.logical_and(j >= gla, j < gla + steps))
    def _():
        _projq_kernel(hn_ref, wt_ref, gq_ref, qt_ref)

    @pl.when(jnp.logical_and(j >= gla + steps, j < gla + 2 * steps))
    def _():
        _projk_kernel(hn_ref, wt_ref, gk_ref, kb_ref, kmean_ref)

    @pl.when(j >= gla + 2 * steps)
    def _():
        _projv_kernel(hn_ref, wt_ref, v3_ref)


def _proj(hn, w_in_t, layer, gq, gk):
    n, d = hn.shape
    gla = GLA_COLS // PROJ_TN
    steps = MOBA_WIDTH // PROJ_TN
    blocks = PROJ_TM // MOBA_BLOCK
    moba_gap = GLA_GATE_RANK
    own = lambda j, first: jnp.clip(j - gla - first * steps, 0, steps - 1)
    const = lambda i, j: (0, 0)
    u, q_t, kb, kmean, v3 = pl.pallas_call(
        _proj_kernel,
        grid=(n // PROJ_TM, gla + 3 * steps),
        in_specs=[
            pl.BlockSpec((PROJ_TM, d), lambda i, j: (i, 0)),
            pl.BlockSpec((pl.Element(1), pl.Element(PROJ_TN), pl.Element(d)),
                         lambda i, j: (layer,
                                       pl.multiple_of(j * PROJ_TN + jnp.where(j >= gla, moba_gap, 0), 8), 0)),
            pl.BlockSpec((MOBA_DH, 1), const),
            pl.BlockSpec((1, MOBA_DH), const),
        ],
        out_specs=[
            pl.BlockSpec((PROJ_TM, PROJ_TN), lambda i, j: (i, jnp.minimum(j, gla - 1))),
            pl.BlockSpec((PROJ_TN, PROJ_TM), lambda i, j: (own(j, 0), i)),
            pl.BlockSpec((PROJ_TM, PROJ_TN), lambda i, j: (i, own(j, 1))),
            pl.BlockSpec((1, blocks, PROJ_TN), lambda i, j: (i, 0, own(j, 1))),
            pl.BlockSpec((blocks, PROJ_TN // MOBA_DH * MOBA_VROWS, MOBA_BLOCK),
                         lambda i, j: (i, own(j, 2), 0)),
        ],
        out_shape=[jax.ShapeDtypeStruct((n, GLA_COLS), F32),
                   jax.ShapeDtypeStruct((MOBA_WIDTH, n), F32),
                   jax.ShapeDtypeStruct((n, MOBA_WIDTH), BF16),
                   jax.ShapeDtypeStruct((n // PROJ_TM, blocks, MOBA_WIDTH), F32),
                   jax.ShapeDtypeStruct((n // MOBA_BLOCK, MOBA_HEADS * MOBA_VROWS, MOBA_BLOCK), BF16)],
        compiler_params=_params("arbitrary", "arbitrary"),
        name="proj",
    )(hn, w_in_t, gq.reshape(-1, 1), gk.reshape(1, -1))
    return u, q_t, kb, kmean.reshape(n // MOBA_BLOCK, MOBA_WIDTH), v3


def _dot_exact_rhs(m, a):
    a1 = a.astype(BF16)
    r1 = a - a1.astype(F32)
    a2 = r1.astype(BF16)
    a3 = (r1 - a2.astype(F32)).astype(BF16)
    return _dot(m, a1) + _dot(m, a2) + _dot(m, a3)


def _gla_head(q, k, v, r, la, g_out, state, upd_ref):
    tc = GLA_TC
    nchunk = tc // GLA_CHUNK
    sub = GLA_SUB

    crow = lax.broadcasted_iota(jnp.int32, (GLA_CHUNK, GLA_CHUNK), 0)
    ccol = lax.broadcasted_iota(jnp.int32, (GLA_CHUNK, GLA_CHUNK), 1)
    tri = (ccol <= crow).astype(BF16)
    la_wide = jnp.concatenate([la[c * GLA_CHUNK:(c + 1) * GLA_CHUNK] for c in range(nchunk)], axis=1)
    bc_wide = _dot_exact_rhs(tri, la_wide)
    bcum = jnp.concatenate([bc_wide[:, c * GLA_DK:(c + 1) * GLA_DK] for c in range(nchunk)], axis=0)
    btot = jnp.concatenate(
        [jnp.broadcast_to(bc_wide[GLA_CHUNK - 1:GLA_CHUNK, c * GLA_DK:(c + 1) * GLA_DK],
                          (GLA_CHUNK, GLA_DK)) for c in range(nchunk)], axis=0)

    q_dec_b = ((q * (GLA_DK ** -0.5)) * jnp.exp(bcum)).astype(BF16)
    k_inv_b = (k * jnp.exp(-bcum)).astype(BF16)
    k_tail_t = (k * jnp.exp(btot - bcum)).T.astype(BF16)
    decay_t = jnp.exp(btot).T
    vb = v.astype(BF16)

    row = lax.broadcasted_iota(jnp.int32, (sub, sub), 0)
    col = lax.broadcasted_iota(jnp.int32, (sub, sub), 1)
    causal = jnp.logical_and((row // GLA_CHUNK) == (col // GLA_CHUNK), col <= row)
    intra = []
    for p in range(tc // sub):
        rows = slice(p * sub, (p + 1) * sub)
        att = jnp.where(causal, _dot_nt(q_dec_b[rows], k_inv_b[rows]), 0.0)
        intra.append(_dot(att.astype(BF16), vb[rows]))

    for c in range(nchunk):
        rows = slice(c * GLA_CHUNK, (c + 1) * GLA_CHUNK)
        upd_ref[c] = _dot(k_tail_t[:, rows], vb[rows])

    inter = []
    for c in range(nchunk):
        rows = slice(c * GLA_CHUNK, (c + 1) * GLA_CHUNK)
        inter.append(_dot(q_dec_b[rows], state.astype(BF16)))
        dec_c = decay_t[:, c * GLA_CHUNK:c * GLA_CHUNK + 1]
        state = dec_c * state + upd_ref[c]
    o = jnp.concatenate(intra, axis=0) + jnp.concatenate(inter, axis=0)
    return _rms(o, g_out) * _silu(r), state


def _gla_kernel(q_ref, k_ref, v_ref, r_ref, la_ref, go_ref, o_ref, s_ref, upd_ref):
    @pl.when(pl.program_id(2) == 0)
    def _():
        s_ref[...] = jnp.zeros_like(s_ref)

    for g in range(GLA_G):
        qk = slice(g * GLA_DK, (g + 1) * GLA_DK)
        vr = slice(g * GLA_DV, (g + 1) * GLA_DV)
        o, state = _gla_head(q_ref[:, qk], k_ref[:, qk], v_ref[:, vr], r_ref[:, vr], la_ref[:, qk],
                             go_ref[...], s_ref[g], upd_ref.at[g])
        s_ref[g] = state
        o_ref[:, vr] = o


def _gla(u, la, g_out, batch, seq):
    n = u.shape[0]
    nt = seq // GLA_TC
    tok = lambda b, h, t: b * nt + t
    qk_w, v_w = GLA_G * GLA_DK, GLA_G * GLA_DV
    k_blk0 = GLA_QK // qk_w
    v_blk0 = 2 * GLA_QK // v_w
    r_blk0 = v_blk0 + GLA_WIDTH // v_w
    return pl.pallas_call(
        _gla_kernel,
        grid=(batch, GLA_HEADS // GLA_G, nt),
        in_specs=[
            pl.BlockSpec((GLA_TC, qk_w), lambda b, h, t: (tok(b, h, t), h)),
            pl.BlockSpec((GLA_TC, qk_w), lambda b, h, t: (tok(b, h, t), k_blk0 + h)),
            pl.BlockSpec((GLA_TC, v_w), lambda b, h, t: (tok(b, h, t), v_blk0 + h)),
            pl.BlockSpec((GLA_TC, v_w), lambda b, h, t: (tok(b, h, t), r_blk0 + h)),
            pl.BlockSpec((GLA_TC, qk_w), lambda b, h, t: (tok(b, h, t), h)),
            pl.BlockSpec((1, GLA_DV), lambda b, h, t: (0, 0)),
        ],
        out_specs=pl.BlockSpec((GLA_TC, v_w), lambda b, h, t: (tok(b, h, t), h)),
        out_shape=jax.ShapeDtypeStruct((n, GLA_WIDTH), F32),
        scratch_shapes=[pltpu.VMEM((GLA_G, GLA_DK, GLA_DV), F32),
                        pltpu.VMEM((GLA_G, GLA_TC // GLA_CHUNK, GLA_DK, GLA_DV), F32)],
        compiler_params=_params("parallel", "parallel", "arbitrary"),
        name="gla",
    )(u, u, u, u, la, g_out.reshape(1, -1))


def _moba_kernel(qt_ref, kb_ref, v3_ref, kmean_ref, o_ref, kaug_ref, snext_ref):
    ti = pl.program_id(2)
    last_tile = pl.num_programs(2) - 1
    bs = MOBA_BLOCK
    tq = MOBA_TQ
    nb = kb_ref.shape[0] // bs
    dh = MOBA_DH
    vr = MOBA_VROWS
    scale = dh ** -0.5

    @pl.when(ti == 0)
    def _():
        seq = kb_ref.shape[0]
        row_blk = lax.broadcasted_iota(jnp.int32, (seq, MXU_DIM - dh), 0) // bs
        lane = lax.broadcasted_iota(jnp.int32, (seq, MXU_DIM - dh), 1)
        onehot = (row_blk == lane).astype(BF16)
        for g in range(MOBA_G):
            kaug_ref[g, :, :dh] = kb_ref[:, g * dh:(g + 1) * dh]
            kaug_ref[g, :, dh:] = onehot

    blk = lax.broadcasted_iota(jnp.int32, (nb, tq), 0)
    q_blk = 2 * ti + (lax.broadcasted_iota(jnp.int32, (nb, tq), 1) // bs)
    past = blk < q_blk
    causal = (lax.broadcasted_iota(jnp.int32, (bs, bs), 0)
              <= lax.broadcasted_iota(jnp.int32, (bs, bs), 1))
    diag = pl.multiple_of(ti * tq, tq)

    def pv_dot(g, t, p):
        return (_dot(v3_ref[2 * t, g * vr:(g + 1) * vr, :], p[:bs])
                + _dot(v3_ref[2 * t + 1, g * vr:(g + 1) * vr, :], p[bs:]))

    q_aug, biases, state = [], [], []

    def score(g, t):
        off = pl.multiple_of(t * tq, tq)
        s = _dot(kaug_ref[g, pl.ds(off, tq), :], q_aug[g])
        return s, jnp.max(s, axis=0, keepdims=True)

    for g in range(MOBA_G):
        qt = qt_ref[g * dh:(g + 1) * dh, :]
        gate = jnp.dot(kmean_ref[:, g * dh:(g + 1) * dh], qt,
                       precision=lax.Precision.HIGHEST, preferred_element_type=F32)
        gate = jnp.where(past, gate, NEG_INF)
        rank = jnp.zeros((nb, tq), jnp.int32)
        for j in range(nb):
            gj = gate[j:j + 1, :]
            ahead = jnp.logical_or(gj > gate, jnp.logical_and(gj == gate, blk > j))
            rank = rank + ahead.astype(jnp.int32)
        chosen = jnp.logical_and(past, rank < MOBA_TOPK)
        bias = jnp.where(chosen, 0.0, NEG_INF)
        qa = jnp.concatenate([(qt * scale).astype(BF16), bias.astype(BF16),
                              jnp.zeros((MXU_DIM - dh - nb, tq), BF16)], axis=0)
        q_aug.append(qa)
        biases.append(bias)

    first_maxes = []
    for g in range(MOBA_G):
        s_first, first_max = score(g, 0)
        snext_ref[g] = s_first
        first_maxes.append(first_max)

    for g in range(MOBA_G):
        qa, bias = q_aug[g], biases[g]
        first_bias = jnp.sum(jnp.where(blk == 2 * ti, bias, 0.0), axis=0, keepdims=True)[:, bs:]
        s0 = _dot(kaug_ref[g, pl.ds(diag, bs), :dh], qa[:dh])
        s00 = jnp.where(causal, s0[:, :bs], NEG_INF)
        s01 = s0[:, bs:] + first_bias
        s11 = jnp.where(causal, _dot(kaug_ref[g, pl.ds(pl.multiple_of(diag + bs, bs), bs), :dh],
                                     qa[:dh, bs:]), NEG_INF)
        m0 = jnp.max(s00, axis=0, keepdims=True)
        m1 = jnp.maximum(jnp.max(s01, axis=0, keepdims=True), jnp.max(s11, axis=0, keepdims=True))
        p0 = jnp.concatenate([jnp.exp(s00 - m0), jnp.exp(s01 - m1)], axis=1).astype(BF16)
        p11 = jnp.exp(s11 - m1).astype(BF16)
        acc0 = _dot(v3_ref[2 * ti, g * vr:(g + 1) * vr, :], p0)
        acc1 = _dot(v3_ref[2 * ti + 1, g * vr:(g + 1) * vr, :], p11)
        acc = jnp.concatenate([acc0[:, :bs], acc0[:, bs:] + acc1], axis=1)
        state.append((jnp.concatenate([m0, m1], axis=1), acc, first_maxes[g]))

    def consume(g, t, s, tile_max, m, acc, valid=None):
        if valid is not None:
            tile_max = jnp.where(valid, tile_max, NEG_INF)
        m_new = jnp.maximum(m, tile_max)
        alpha = jnp.exp(m - m_new)
        shift = m_new if valid is None else jnp.where(valid, m_new, -NEG_INF)
        p = jnp.exp(s - shift).astype(BF16)
        return m_new, alpha * acc + pv_dot(g, t, p)

    def body(u, carry):
        out = []
        for g in range(MOBA_G):
            m, acc, max_a = carry[g]
            s_b, max_b = score(g, 2 * u + 1)
            m, acc = consume(g, 2 * u, snext_ref[g], max_a, m, acc)
            s_c, max_c = score(g, jnp.minimum(2 * u + 2, last_tile))
            snext_ref[g] = s_c
            m, acc = consume(g, 2 * u + 1, s_b, max_b, m, acc, valid=2 * u + 1 < ti)
            out.append((m, acc, max_c))
        return tuple(out)

    carry = lax.fori_loop(0, (ti + 1) // 2, body, tuple(state))
    for g in range(MOBA_G):
        m, acc, _ = carry[g]
        o_ref[:, g * dh:(g + 1) * dh] = (acc[:dh] * (1.0 / acc[dh:dh + 1])).T


def _moba(q_t, kb, kmean, v3, batch, seq):
    n = kb.shape[0]
    nb = seq // MOBA_BLOCK
    nt = seq // MOBA_TQ
    gw = MOBA_G * MOBA_DH
    return pl.pallas_call(
        _moba_kernel,
        grid=(batch, MOBA_HEADS // MOBA_G, nt),
        in_specs=[
            pl.BlockSpec((gw, MOBA_TQ), lambda b, h, i: (h, b * nt + i)),
            pl.BlockSpec((seq, gw), lambda b, h, i: (b, h)),
            pl.BlockSpec((nb, MOBA_G * MOBA_VROWS, MOBA_BLOCK), lambda b, h, i: (b, h, 0)),
            pl.BlockSpec((nb, gw), lambda b, h, i: (b, h)),
        ],
        out_specs=pl.BlockSpec((MOBA_TQ, gw), lambda b, h, i: (b * nt + i, h)),
        out_shape=jax.ShapeDtypeStruct((n, MOBA_WIDTH), F32),
        scratch_shapes=[pltpu.VMEM((MOBA_G, seq, MXU_DIM), BF16),
                        pltpu.VMEM((MOBA_G, MOBA_TQ, MOBA_TQ), F32)],
        compiler_params=_params("parallel", "parallel", "arbitrary"),
        name="moba",
    )(q_t, kb, v3, kmean)


def _memkv_kernel(mem_ref, g_ref, wkv_ref, gk_ref, k_ref, v_ref):
    h = _rms(mem_ref[...], g_ref[...]).astype(BF16)
    kv = _dot(h, wkv_ref[...].astype(BF16))
    gk = gk_ref[...]
    ks = [_rms(kv[:, i * XATTN_DH:(i + 1) * XATTN_DH], gk) for i in range(XATTN_HEADS)]
    k_ref[...] = jnp.concatenate(ks, axis=1)
    v_ref[...] = kv[:, XATTN_WIDTH:]


def _memkv(mem, g, wkv_b, gk):
    m, d = mem.shape
    return pl.pallas_call(
        _memkv_kernel,
        out_shape=[jax.ShapeDtypeStruct((m, XATTN_WIDTH), F32),
                   jax.ShapeDtypeStruct((m, XATTN_WIDTH), F32)],
        compiler_params=pltpu.CompilerParams(vmem_limit_bytes=VMEM_LIMIT),
        name="memkv",
    )(mem, g.reshape(1, d), wkv_b, gk.reshape(1, -1))


def _outx_kernel(x1_ref, og_ref, om_ref, wout_ref, gx_ref, wq_ref, gq_ref, kx_ref, vx_ref, wo_ref,
                 o_ref):
    x2 = (x1_ref[...]
          + _dot(og_ref[...].astype(BF16), wout_ref[:GLA_WIDTH, :])
          + _dot(om_ref[...].astype(BF16), wout_ref[GLA_WIDTH:, :]))
    h = _rms(x2, gx_ref[...]).astype(BF16)
    q = _dot(h, wq_ref[...])
    gq = gq_ref[...]
    scale = XATTN_DH ** -0.5
    heads = []
    for i in range(XATTN_HEADS):
        cols = slice(i * XATTN_DH, (i + 1) * XATTN_DH)
        qh = _rms(q[:, cols], gq).astype(BF16)
        s = _dot_nt(qh, kx_ref[:, cols].astype(BF16)) * scale
        p = jnp.exp(s - jnp.max(s, axis=-1, keepdims=True))
        p = p * (1.0 / jnp.sum(p, axis=-1, keepdims=True))
        heads.append(_dot(p.astype(BF16), vx_ref[:, cols].astype(BF16)))
    o = jnp.concatenate(heads, axis=1).astype(BF16)
    o_ref[...] = x2 + _dot(o, wo_ref[...])


def _outx(x1, og, om, wout_b, gx, wq_b, gq, kx, vx, wo_b, seq, mem_len):
    n, d = x1.shape
    tiles_per_seq = seq // OUTX_TM
    const = lambda i: (0, 0)
    return pl.pallas_call(
        _outx_kernel,
        grid=(n // OUTX_TM,),
        in_specs=[
            pl.BlockSpec((OUTX_TM, d), lambda i: (i, 0)),
            pl.BlockSpec((OUTX_TM, GLA_WIDTH), lambda i: (i, 0)),
            pl.BlockSpec((OUTX_TM, MOBA_WIDTH), lambda i: (i, 0)),
            pl.BlockSpec((GLA_WIDTH + MOBA_WIDTH, d), const),
            pl.BlockSpec((1, d), const),
            pl.BlockSpec((d, XATTN_WIDTH), const),
            pl.BlockSpec((1, XATTN_DH), const),
            pl.BlockSpec((mem_len, XATTN_WIDTH), lambda i: (i // tiles_per_seq, 0)),
            pl.BlockSpec((mem_len, XATTN_WIDTH), lambda i: (i // tiles_per_seq, 0)),
            pl.BlockSpec((XATTN_WIDTH, d), const),
        ],
        out_specs=pl.BlockSpec((OUTX_TM, d), lambda i: (i, 0)),
        out_shape=jax.ShapeDtypeStruct((n, d), F32),
        compiler_params=_params("parallel"),
        name="outx",
    )(x1, og, om, wout_b, gx.reshape(1, d), wq_b, gq.reshape(1, -1), kx, vx, wo_b)


def kernel(x, mem, ffn1_norm, ffn1_w_gate, ffn1_w_up, ffn1_w_down, mix_norm, w_in, gla_w_gate2, gla_b_gate2, gla_out_norm, moba_q_norm, moba_k_norm, w_out, xattn_norm, mem_norm, xattn_w_q, xattn_w_kv, xattn_w_o, xattn_q_norm, xattn_k_norm, ffn2_norm, ffn2_w_gate, ffn2_w_up, ffn2_w_down):
    batch, seq, d = x.shape
    mem_len = mem.shape[1]
    depth = ffn1_norm.shape[0]
    n = batch * seq
    assert seq % GLA_TC == 0 and seq % (2 * MOBA_BLOCK) == 0 and seq % OUTX_TM == 0
    assert seq // MOBA_BLOCK + MOBA_DH <= MXU_DIM and seq // MOBA_BLOCK == 16
    assert n % FFN_TM == 0 and n % NORM_TM == 0 and n % PROJ_TM == 0 and PROJ_TM % MOBA_BLOCK == 0

    xf = x.reshape(n, d)
    memf = mem.reshape(batch * mem_len, d)
    lr0 = GLA_COLS
    mq0 = lr0 + GLA_GATE_RANK
    w_in_t = jnp.swapaxes(w_in, 1, 2)
    for l in range(depth):
        w_lr_t = w_in_t[l, lr0:mq0]

        x1 = _ffn(xf, ffn1_norm[l], ffn1_w_gate[l], ffn1_w_up[l], ffn1_w_down[l])
        hn, la = _mixnorm(x1, mix_norm[l], w_lr_t, gla_w_gate2[l], gla_b_gate2[l])
        u, q_t, kb, kmean, v3 = _proj(hn, w_in_t, l, moba_q_norm[l], moba_k_norm[l])
        o_gla = _gla(u, la, gla_out_norm[l], batch, seq)
        o_moba = _moba(q_t, kb, kmean, v3, batch, seq)
        kx, vx = _memkv(memf, mem_norm[l], xattn_w_kv[l], xattn_k_norm[l])
        x3 = _outx(x1, o_gla, o_moba, w_out[l].astype(BF16), xattn_norm[l],
                   xattn_w_q[l].astype(BF16), xattn_q_norm[l], kx, vx,
                   xattn_w_o[l].astype(BF16), seq, mem_len)
        xf = _ffn(x3, ffn2_norm[l], ffn2_w_gate[l], ffn2_w_up[l], ffn2_w_down[l])
    return xf.reshape(batch, seq, d)
```

```python
import jax
import jax.numpy as jnp
from jax import lax
from jax.experimental import pallas as pl
from jax.experimental.pallas import tpu as pltpu

F32 = jnp.float32
BF16 = jnp.bfloat16

EPS = 1e-6
NEG_INF = -1e30
D_FF = 5632
GLA_HEADS = 4
GLA_DV = 256
GLA_DK = 128
GLA_QK = GLA_HEADS * GLA_DK
GLA_WIDTH = GLA_HEADS * GLA_DV
GLA_COLS = 2 * GLA_QK + 2 * GLA_WIDTH
GLA_GATE_RANK = 16
GLA_GATE_TAU = 16.0
GLA_CHUNK = 64
MOBA_DH = 128
MOBA_HEADS = 8
MOBA_WIDTH = MOBA_HEADS * MOBA_DH
MOBA_BLOCK = 256
MOBA_TOPK = 3
XATTN_HEADS = 4
XATTN_DH = 128
XATTN_WIDTH = XATTN_HEADS * XATTN_DH

V7X_VMEM_BYTES = 64 * 1024 * 1024
VMEM_LIMIT = V7X_VMEM_BYTES - 6 * 1024 * 1024
FFN_VMEM_LIMIT = V7X_VMEM_BYTES - 2 * 1024 * 1024
MXU_DIM = 256

FFN_TM = 1024
FFN_TF = 512
NORM_TM = 1024
PROJ_TM = 2048
PROJ_TN = 512
GLA_TC = 512
GLA_SUB = 256
GLA_G = 4
MOBA_G = 4
MOBA_TQ = 2 * MOBA_BLOCK
BF16_SUBLANES = 16
MOBA_VPAD = BF16_SUBLANES
MOBA_VROWS = MOBA_DH + MOBA_VPAD
OUTX_TM = 512


def _params(*semantics):
    return pltpu.CompilerParams(dimension_semantics=semantics, vmem_limit_bytes=VMEM_LIMIT)


def _rms(x, g):
    return x * lax.rsqrt(jnp.mean(x * x, axis=-1, keepdims=True) + EPS) * g


def _dot(a, b):
    return jnp.dot(a, b, preferred_element_type=F32)


def _dot_nt(a, b):
    return lax.dot_general(a, b, (((1,), (1,)), ((), ())), preferred_element_type=F32)


def _silu(x):
    return x * (1.0 / (1.0 + jnp.exp(-x)))


def _ffn_kernel(x_hbm, g_ref, wg_ref, wu_ref, wd_ref, o_hbm, h_ref, xbuf_ref, acc_ref, xsem, osem):
    i = pl.program_id(0)
    f = pl.program_id(1)
    last_i = pl.num_programs(0) - 1
    last_f = pl.num_programs(1) - 1

    def x_copy(tile):
        return pltpu.make_async_copy(x_hbm.at[pl.ds(tile * FFN_TM, FFN_TM)], xbuf_ref, xsem)

    def o_copy(tile):
        return pltpu.make_async_copy(acc_ref, o_hbm.at[pl.ds(tile * FFN_TM, FFN_TM)], osem)

    @pl.when(jnp.logical_and(i == 0, f == 0))
    def _():
        x_copy(0).start()

    @pl.when(f == 0)
    def _():
        x_copy(i).wait()
        x = xbuf_ref[...]
        h_ref[...] = _rms(x, g_ref[...]).astype(BF16)

        @pl.when(i > 0)
        def _():
            o_copy(i - 1).wait()

        acc_ref[...] = xbuf_ref[...]

    @pl.when(jnp.logical_and(f == 1, i < last_i))
    def _():
        x_copy(i + 1).start()

    h = h_ref[...]
    gate = _dot(h, wg_ref[...].astype(BF16))
    up = _dot(h, wu_ref[...].astype(BF16))
    act = (0.5 * _silu(gate) * up).astype(BF16)
    acc_ref[...] += _dot(act, wd_ref[...].astype(BF16))

    @pl.when(f == last_f)
    def _():
        o_copy(i).start()

        @pl.when(i == last_i)
        def _():
            o_copy(i).wait()


def _ffn(x, g, wg, wu, wd):
    n, d = x.shape
    grid = (n // FFN_TM, D_FF // FFN_TF)
    assert grid[1] >= 2
    return pl.pallas_call(
        _ffn_kernel,
        grid=grid,
        in_specs=[
            pl.BlockSpec(memory_space=pl.ANY),
            pl.BlockSpec((1, d), lambda i, f: (0, 0)),
            pl.BlockSpec((d, FFN_TF), lambda i, f: (0, f)),
            pl.BlockSpec((d, FFN_TF), lambda i, f: (0, f)),
            pl.BlockSpec((FFN_TF, d), lambda i, f: (f, 0)),
        ],
        out_specs=pl.BlockSpec(memory_space=pl.ANY),
        out_shape=jax.ShapeDtypeStruct((n, d), F32),
        scratch_shapes=[pltpu.VMEM((FFN_TM, d), BF16), pltpu.VMEM((FFN_TM, d), F32),
                        pltpu.VMEM((FFN_TM, d), F32),
                        pltpu.SemaphoreType.DMA, pltpu.SemaphoreType.DMA],
        compiler_params=pltpu.CompilerParams(dimension_semantics=("arbitrary", "arbitrary"),
                                             vmem_limit_bytes=FFN_VMEM_LIMIT),
        name="ffn",
    )(x, g.reshape(1, d), wg, wu, wd)


def _mixnorm_kernel(x_ref, g_ref, wlr_ref, wg2_ref, bg2_ref, hn_ref, la_ref):
    h = _rms(x_ref[...], g_ref[...]).astype(BF16)
    hn_ref[...] = h
    lr = _dot_nt(h, wlr_ref[...].astype(BF16))
    pre = _dot(lr.astype(BF16), wg2_ref[...].astype(BF16)) + bg2_ref[...]
    log_sig = -(jnp.maximum(-pre, 0.0) + jnp.log(1.0 + jnp.exp(-jnp.abs(pre))))
    la_ref[...] = log_sig * (1.0 / GLA_GATE_TAU)


def _mixnorm(x1, g, w_lr_t, wg2, bg2):
    n, d = x1.shape
    const = lambda i: (0, 0)
    return pl.pallas_call(
        _mixnorm_kernel,
        grid=(n // NORM_TM,),
        in_specs=[
            pl.BlockSpec((NORM_TM, d), lambda i: (i, 0)),
            pl.BlockSpec((1, d), const),
            pl.BlockSpec((GLA_GATE_RANK, d), const),
            pl.BlockSpec((GLA_GATE_RANK, GLA_QK), const),
            pl.BlockSpec((1, GLA_QK), const),
        ],
        out_specs=[pl.BlockSpec((NORM_TM, d), lambda i: (i, 0)),
                   pl.BlockSpec((NORM_TM, GLA_QK), lambda i: (i, 0))],
        out_shape=[jax.ShapeDtypeStruct((n, d), BF16),
                   jax.ShapeDtypeStruct((n, GLA_QK), F32)],
        compiler_params=_params("parallel"),
        name="mixnorm",
    )(x1, g.reshape(1, d), w_lr_t, wg2, bg2.reshape(1, -1))


def _projg_kernel(hn_ref, wt_ref, u_ref):
    u_ref[...] = _dot_nt(hn_ref[...], wt_ref[0].astype(BF16))


def _projq_kernel(hn_ref, wt_ref, g_ref, qt_ref):
    yt = _dot_nt(wt_ref[0].astype(BF16), hn_ref[...])
    g = g_ref[...]
    parts = []
    for k in range(PROJ_TN // MOBA_DH):
        p = yt[k * MOBA_DH:(k + 1) * MOBA_DH, :]
        inv = lax.rsqrt(jnp.mean(p * p, axis=0, keepdims=True) + EPS)
        parts.append(p * inv * g)
    qt_ref[...] = jnp.concatenate(parts, axis=0)


def _projk_kernel(hn_ref, wt_ref, g_ref, kb_ref, kmean_ref):
    y = _dot_nt(hn_ref[...], wt_ref[0].astype(BF16))
    g = g_ref[...]
    parts = [_rms(y[:, k * MOBA_DH:(k + 1) * MOBA_DH], g) for k in range(PROJ_TN // MOBA_DH)]
    kn = jnp.concatenate(parts, axis=1)
    kb_ref[...] = kn.astype(BF16)
    blocks = PROJ_TM // MOBA_BLOCK
    kmean_ref[0] = jnp.sum(kn.reshape(blocks, MOBA_BLOCK, PROJ_TN), axis=1) * (1.0 / MOBA_BLOCK)


def _projv_kernel(hn_ref, wt_ref, v3_ref):
    yt = _dot_nt(wt_ref[0].astype(BF16), hn_ref[...]).astype(BF16)
    ones = jnp.ones((MOBA_VPAD, PROJ_TM), BF16)
    parts = []
    for k in range(PROJ_TN // MOBA_DH):
        parts += [yt[k * MOBA_DH:(k + 1) * MOBA_DH], ones]
    ya = jnp.concatenate(parts, axis=0)
    for t in range(PROJ_TM // MOBA_BLOCK):
        v3_ref[t] = ya[:, t * MOBA_BLOCK:(t + 1) * MOBA_BLOCK]


def _proj_kernel(hn_ref, wt_ref, gq_ref, gk_ref, u_ref, qt_ref, kb_ref, kmean_ref, v3_ref):
    j = pl.program_id(1)
    gla = GLA_COLS // PROJ_TN
    steps = MOBA_WIDTH // PROJ_TN

    @pl.when(j < gla)
    def _():
        _projg_kernel(hn_ref, wt_ref, u_ref)

    @pl.when(jnp.logical_and(j >= gla, j < gla + steps))
    def _():
        _projq_kernel(hn_ref, wt_ref, gq_ref, qt_ref)

    @pl.when(jnp.logical_and(j >= gla + steps, j < gla + 2 * steps))
    def _():
        _projk_kernel(hn_ref, wt_ref, gk_ref, kb_ref, kmean_ref)

    @pl.when(j >= gla + 2 * steps)
    def _():
        _projv_kernel(hn_ref, wt_ref, v3_ref)


def _proj(hn, w_in_t, layer, gq, gk):
    n, d = hn.shape
    gla = GLA_COLS // PROJ_TN
    steps = MOBA_WIDTH // PROJ_TN
    blocks = PROJ_TM // MOBA_BLOCK
    moba_gap = GLA_GATE_RANK
    own = lambda j, first: jnp.clip(j - gla - first * steps, 0, steps - 1)
    const = lambda i, j: (0, 0)
    u, q_t, kb, kmean, v3 = pl.pallas_call(
        _proj_kernel,
        grid=(n // PROJ_TM, gla + 3 * steps),
        in_specs=[
            pl.BlockSpec((PROJ_TM, d), lambda i, j: (i, 0)),
            pl.BlockSpec((pl.Element(1), pl.Element(PROJ_TN), pl.Element(d)),
                         lambda i, j: (layer,
                                       pl.multiple_of(j * PROJ_TN + jnp.where(j >= gla, moba_gap, 0), 8), 0)),
            pl.BlockSpec((MOBA_DH, 1), const),
            pl.BlockSpec((1, MOBA_DH), const),
        ],
        out_specs=[
            pl.BlockSpec((PROJ_TM, PROJ_TN), lambda i, j: (i, jnp.minimum(j, gla - 1))),
            pl.BlockSpec((PROJ_TN, PROJ_TM), lambda i, j: (own(j, 0), i)),
            pl.BlockSpec((PROJ_TM, PROJ_TN), lambda i, j: (i, own(j, 1))),
            pl.BlockSpec((1, blocks, PROJ_TN), lambda i, j: (i, 0, own(j, 1))),
            pl.BlockSpec((blocks, PROJ_TN // MOBA_DH * MOBA_VROWS, MOBA_BLOCK),
                         lambda i, j: (i, own(j, 2), 0)),
        ],
        out_shape=[jax.ShapeDtypeStruct((n, GLA_COLS), F32),
                   jax.ShapeDtypeStruct((MOBA_WIDTH, n), F32),
                   jax.ShapeDtypeStruct((n, MOBA_WIDTH), BF16),
                   jax.ShapeDtypeStruct((n // PROJ_TM, blocks, MOBA_WIDTH), F32),
                   jax.ShapeDtypeStruct((n // MOBA_BLOCK, MOBA_HEADS * MOBA_VROWS, MOBA_BLOCK), BF16)],
        compiler_params=_params("arbitrary", "arbitrary"),
        name="proj",
    )(hn, w_in_t, gq.reshape(-1, 1), gk.reshape(1, -1))
    return u, q_t, kb, kmean.reshape(n // MOBA_BLOCK, MOBA_WIDTH), v3


def _dot_exact_rhs(m, a):
    a1 = a.astype(BF16)
    r1 = a - a1.astype(F32)
    a2 = r1.astype(BF16)
    a3 = (r1 - a2.astype(F32)).astype(BF16)
    return _dot(m, a1) + _dot(m, a2) + _dot(m, a3)


def _gla_head(q, k, v, r, la, g_out, state, upd_ref):
    tc = GLA_TC
    nchunk = tc // GLA_CHUNK
    sub = GLA_SUB

    crow = lax.broadcasted_iota(jnp.int32, (GLA_CHUNK, GLA_CHUNK), 0)
    ccol = lax.broadcasted_iota(jnp.int32, (GLA_CHUNK, GLA_CHUNK), 1)
    tri = (ccol <= crow).astype(BF16)
    la_wide = jnp.concatenate([la[c * GLA_CHUNK:(c + 1) * GLA_CHUNK] for c in range(nchunk)], axis=1)
    bc_wide = _dot_exact_rhs(tri, la_wide)
    bcum = jnp.concatenate([bc_wide[:, c * GLA_DK:(c + 1) * GLA_DK] for c in range(nchunk)], axis=0)
    btot = jnp.concatenate(
        [jnp.broadcast_to(bc_wide[GLA_CHUNK - 1:GLA_CHUNK, c * GLA_DK:(c + 1) * GLA_DK],
                          (GLA_CHUNK, GLA_DK)) for c in range(nchunk)], axis=0)

    q_dec_b = ((q * (GLA_DK ** -0.5)) * jnp.exp(bcum)).astype(BF16)
    k_inv_b = (k * jnp.exp(-bcum)).astype(BF16)
    k_tail_t = (k * jnp.exp(btot - bcum)).T.astype(BF16)
    decay_t = jnp.exp(btot).T
    vb = v.astype(BF16)

    row = lax.broadcasted_iota(jnp.int32, (sub, sub), 0)
    col = lax.broadcasted_iota(jnp.int32, (sub, sub), 1)
    causal = jnp.logical_and((row // GLA_CHUNK) == (col // GLA_CHUNK), col <= row)
    intra = []
    for p in range(tc // sub):
        rows = slice(p * sub, (p + 1) * sub)
        att = jnp.where(causal, _dot_nt(q_dec_b[rows], k_inv_b[rows]), 0.0)
        intra.append(_dot(att.astype(BF16), vb[rows]))

    for c in range(nchunk):
        rows = slice(c * GLA_CHUNK, (c + 1) * GLA_CHUNK)
        upd_ref[c] = _dot(k_tail_t[:, rows], vb[rows])

    inter = []
    for c in range(nchunk):
        rows = slice(c * GLA_CHUNK, (c + 1) * GLA_CHUNK)
        inter.append(_dot(q_dec_b[rows], state.astype(BF16)))
        dec_c = decay_t[:, c * GLA_CHUNK:c * GLA_CHUNK + 1]
        state = dec_c * state + upd_ref[c]
    o = jnp.concatenate(intra, axis=0) + jnp.concatenate(inter, axis=0)
    return _rms(o, g_out) * _silu(r), state


def _gla_kernel(q_ref, k_ref, v_ref, r_ref, la_ref, go_ref, o_ref, s_ref, upd_ref):
    @pl.when(pl.program_id(2) == 0)
    def _():
        s_ref[...] = jnp.zeros_like(s_ref)

    for g in range(GLA_G):
        qk = slice(g * GLA_DK, (g + 1) * GLA_DK)
        vr = slice(g * GLA_DV, (g + 1) * GLA_DV)
        o, state = _gla_head(q_ref[:, qk], k_ref[:, qk], v_ref[:, vr], r_ref[:, vr], la_ref[:, qk],
                             go_ref[...], s_ref[g], upd_ref.at[g])
        s_ref[g] = state
        o_ref[:, vr] = o


def _gla(u, la, g_out, batch, seq):
    n = u.shape[0]
    nt = seq // GLA_TC
    tok = lambda b, h, t: b * nt + t
    qk_w, v_w = GLA_G * GLA_DK, GLA_G * GLA_DV
    k_blk0 = GLA_QK // qk_w
    v_blk0 = 2 * GLA_QK // v_w
    r_blk0 = v_blk0 + GLA_WIDTH // v_w
    return pl.pallas_call(
        _gla_kernel,
        grid=(batch, GLA_HEADS // GLA_G, nt),
        in_specs=[
            pl.BlockSpec((GLA_TC, qk_w), lambda b, h, t: (tok(b, h, t), h)),
            pl.BlockSpec((GLA_TC, qk_w), lambda b, h, t: (tok(b, h, t), k_blk0 + h)),
            pl.BlockSpec((GLA_TC, v_w), lambda b, h, t: (tok(b, h, t), v_blk0 + h)),
            pl.BlockSpec((GLA_TC, v_w), lambda b, h, t: (tok(b, h, t), r_blk0 + h)),
            pl.BlockSpec((GLA_TC, qk_w), lambda b, h, t: (tok(b, h, t), h)),
            pl.BlockSpec((1, GLA_DV), lambda b, h, t: (0, 0)),
        ],
        out_specs=pl.BlockSpec((GLA_TC, v_w), lambda b, h, t: (tok(b, h, t), h)),
        out_shape=jax.ShapeDtypeStruct((n, GLA_WIDTH), F32),
        scratch_shapes=[pltpu.VMEM((GLA_G, GLA_DK, GLA_DV), F32),
                        pltpu.VMEM((GLA_G, GLA_TC // GLA_CHUNK, GLA_DK, GLA_DV), F32)],
        compiler_params=_params("parallel", "parallel", "arbitrary"),
        name="gla",
    )(u, u, u, u, la, g_out.reshape(1, -1))


def _moba_kernel(qt_ref, kb_ref, v3_ref, kmean_ref, o_ref, kaug_ref, snext_ref):
    ti = pl.program_id(2)
    last_tile = pl.num_programs(2) - 1
    bs = MOBA_BLOCK
    tq = MOBA_TQ
    nb = kb_ref.shape[0] // bs
    dh = MOBA_DH
    vr = MOBA_VROWS
    scale = dh ** -0.5

    @pl.when(ti == 0)
    def _():
        seq = kb_ref.shape[0]
        row_blk = lax.broadcasted_iota(jnp.int32, (seq, MXU_DIM - dh), 0) // bs
        lane = lax.broadcasted_iota(jnp.int32, (seq, MXU_DIM - dh), 1)
        onehot = (row_blk == lane).astype(BF16)
        for g in range(MOBA_G):
            kaug_ref[g, :, :dh] = kb_ref[:, g * dh:(g + 1) * dh]
            kaug_ref[g, :, dh:] = onehot

    blk = lax.broadcasted_iota(jnp.int32, (nb, tq), 0)
    q_blk = 2 * ti + (lax.broadcasted_iota(jnp.int32, (nb, tq), 1) // bs)
    past = blk < q_blk
    causal = (lax.broadcasted_iota(jnp.int32, (bs, bs), 0)
              <= lax.broadcasted_iota(jnp.int32, (bs, bs), 1))
    diag = pl.multiple_of(ti * tq, tq)

    def pv_dot(g, t, p):
        return (_dot(v3_ref[2 * t, g * vr:(g + 1) * vr, :], p[:bs])
                + _dot(v3_ref[2 * t + 1, g * vr:(g + 1) * vr, :], p[bs:]))

    q_aug, biases, state = [], [], []

    def score(g, t):
        off = pl.multiple_of(t * tq, tq)
        s = _dot(kaug_ref[g, pl.ds(off, tq), :], q_aug[g])
        return s, jnp.max(s, axis=0, keepdims=True)

    for g in range(MOBA_G):
        qt = qt_ref[g * dh:(g + 1) * dh, :]
        gate = jnp.dot(kmean_ref[:, g * dh:(g + 1) * dh], qt,
                       precision=lax.Precision.HIGHEST, preferred_element_type=F32)
        gate = jnp.where(past, gate, NEG_INF)
        rank = jnp.zeros((nb, tq), jnp.int32)
        for j in range(nb):
            gj = gate[j:j + 1, :]
            ahead = jnp.logical_or(gj > gate, jnp.logical_and(gj == gate, blk > j))
            rank = rank + ahead.astype(jnp.int32)
        chosen = jnp.logical_and(past, rank < MOBA_TOPK)
        bias = jnp.where(chosen, 0.0, NEG_INF)
        qa = jnp.concatenate([(qt * scale).astype(BF16), bias.astype(BF16),
                              jnp.zeros((MXU_DIM - dh - nb, tq), BF16)], axis=0)
        q_aug.append(qa)
        biases.append(bias)

    first_maxes = []
    for g in range(MOBA_G):
        s_first, first_max = score(g, 0)
        snext_ref[g] = s_first
        first_maxes.append(first_max)

    for g in range(MOBA_G):
        qa, bias = q_aug[g], biases[g]
        first_bias = jnp.sum(jnp.where(blk == 2 * ti, bias, 0.0), axis=0, keepdims=True)[:, bs:]
        s0 = _dot(kaug_ref[g, pl.ds(diag, bs), :dh], qa[:dh])
        s00 = jnp.where(causal, s0[:, :bs], NEG_INF)
        s01 = s0[:, bs:] + first_bias
        s11 = jnp.where(causal, _dot(kaug_ref[g, pl.ds(pl.multiple_of(diag + bs, bs), bs), :dh],
                                     qa[:dh, bs:]), NEG_INF)
        m0 = jnp.max(s00, axis=0, keepdims=True)
        m1 = jnp.maximum(jnp.max(s01, axis=0, keepdims=True), jnp.max(s11, axis=0, keepdims=True))
        p0 = jnp.concatenate([jnp.exp(s00 - m0), jnp.exp(s01 - m1)], axis=1).astype(BF16)
        p11 = jnp.exp(s11 - m1).astype(BF16)
        acc0 = _dot(v3_ref[2 * ti, g * vr:(g + 1) * vr, :], p0)
        acc1 = _dot(v3_ref[2 * ti + 1, g * vr:(g + 1) * vr, :], p11)
        acc = jnp.concatenate([acc0[:, :bs], acc0[:, bs:] + acc1], axis=1)
        state.append((jnp.concatenate([m0, m1], axis=1), acc, first_maxes[g]))

    def consume(g, t, s, tile_max, m, acc, valid=None):
        if valid is not None:
            tile_max = jnp.where(valid, tile_max, NEG_INF)
        m_new = jnp.maximum(m, tile_max)
        alpha = jnp.exp(m - m_new)
        shift = m_new if valid is None else jnp.where(valid, m_new, -NEG_INF)
        p = jnp.exp(s - shift).astype(BF16)
        return m_new, alpha * acc + pv_dot(g, t, p)

    def body(u, carry):
        out = []
        for g in range(MOBA_G):
            m, acc, max_a = carry[g]
            s_b, max_b = score(g, 2 * u + 1)
            m, acc = consume(g, 2 * u, snext_ref[g], max_a, m, acc)
            s_c, max_c = score(g, jnp.minimum(2 * u + 2, last_tile))
            snext_ref[g] = s_c
            m, acc = consume(g, 2 * u + 1, s_b, max_b, m, acc, valid=2 * u + 1 < ti)
            out.append((m, acc, max_c))
        return tuple(out)

    carry = lax.fori_loop(0, (ti + 1) // 2, body, tuple(state))
    for g in range(MOBA_G):
        m, acc, _ = carry[g]
        o_ref[:, g * dh:(g + 1) * dh] = (acc[:dh] * (1.0 / acc[dh:dh + 1])).T


def _moba(q_t, kb, kmean, v3, batch, seq):
    n = kb.shape[0]
    nb = seq // MOBA_BLOCK
    nt = seq // MOBA_TQ
    gw = MOBA_G * MOBA_DH
    return pl.pallas_call(
        _moba_kernel,
        grid=(batch, MOBA_HEADS // MOBA_G, nt),
        in_specs=[
            pl.BlockSpec((gw, MOBA_TQ), lambda b, h, i: (h, b * nt + i)),
            pl.BlockSpec((seq, gw), lambda b, h, i: (b, h)),
            pl.BlockSpec((nb, MOBA_G * MOBA_VROWS, MOBA_BLOCK), lambda b, h, i: (b, h, 0)),
            pl.BlockSpec((nb, gw), lambda b, h, i: (b, h)),
        ],
        out_specs=pl.BlockSpec((MOBA_TQ, gw), lambda b, h, i: (b * nt + i, h)),
        out_shape=jax.ShapeDtypeStruct((n, MOBA_WIDTH), F32),
        scratch_shapes=[pltpu.VMEM((MOBA_G, seq, MXU_DIM), BF16),
                        pltpu.VMEM((MOBA_G, MOBA_TQ, MOBA_TQ), F32)],
        compiler_params=_params("parallel", "parallel", "arbitrary"),
        name="moba",
    )(q_t, kb, v3, kmean)


def _memkv_kernel(mem_ref, g_ref, wkv_ref, gk_ref, k_ref, v_ref):
    h = _rms(mem_ref[...], g_ref[...]).astype(BF16)
    kv = _dot(h, wkv_ref[...].astype(BF16))
    gk = gk_ref[...]
    ks = [_rms(kv[:, i * XATTN_DH:(i + 1) * XATTN_DH], gk) for i in range(XATTN_HEADS)]
    k_ref[...] = jnp.concatenate(ks, axis=1)
    v_ref[...] = kv[:, XATTN_WIDTH:]


def _memkv(mem, g, wkv_b, gk):
    m, d = mem.shape
    return pl.pallas_call(
        _memkv_kernel,
        out_shape=[jax.ShapeDtypeStruct((m, XATTN_WIDTH), F32),
                   jax.ShapeDtypeStruct((m, XATTN_WIDTH), F32)],
        compiler_params=pltpu.CompilerParams(vmem_limit_bytes=VMEM_LIMIT),
        name="memkv",
    )(mem, g.reshape(1, d), wkv_b, gk.reshape(1, -1))


def _outx_kernel(x1_ref, og_ref, om_ref, wout_ref, gx_ref, wq_ref, gq_ref, kx_ref, vx_ref, wo_ref,
                 o_ref):
    x2 = (x1_ref[...]
          + _dot(og_ref[...].astype(BF16), wout_ref[:GLA_WIDTH, :])
          + _dot(om_ref[...].astype(BF16), wout_ref[GLA_WIDTH:, :]))
    h = _rms(x2, gx_ref[...]).astype(BF16)
    q = _dot(h, wq_ref[...])
    gq = gq_ref[...]
    scale = XATTN_DH ** -0.5
    heads = []
    for i in range(XATTN_HEADS):
        cols = slice(i * XATTN_DH, (i + 1) * XATTN_DH)
        qh = _rms(q[:, cols], gq).astype(BF16)
        s = _dot_nt(qh, kx_ref[:, cols].astype(BF16)) * scale
        p = jnp.exp(s - jnp.max(s, axis=-1, keepdims=True))
        p = p * (1.0 / jnp.sum(p, axis=-1, keepdims=True))
        heads.append(_dot(p.astype(BF16), vx_ref[:, cols].astype(BF16)))
    o = jnp.concatenate(heads, axis=1).astype(BF16)
    o_ref[...] = x2 + _dot(o, wo_ref[...])


def _outx(x1, og, om, wout_b, gx, wq_b, gq, kx, vx, wo_b, seq, mem_len):
    n, d = x1.shape
    tiles_per_seq = seq // OUTX_TM
    const = lambda i: (0, 0)
    return pl.pallas_call(
        _outx_kernel,
        grid=(n // OUTX_TM,),
        in_specs=[
            pl.BlockSpec((OUTX_TM, d), lambda i: (i, 0)),
            pl.BlockSpec((OUTX_TM, GLA_WIDTH), lambda i: (i, 0)),
            pl.BlockSpec((OUTX_TM, MOBA_WIDTH), lambda i: (i, 0)),
            pl.BlockSpec((GLA_WIDTH + MOBA_WIDTH, d), const),
            pl.BlockSpec((1, d), const),
            pl.BlockSpec((d, XATTN_WIDTH), const),
            pl.BlockSpec((1, XATTN_DH), const),
            pl.BlockSpec((mem_len, XATTN_WIDTH), lambda i: (i // tiles_per_seq, 0)),
            pl.BlockSpec((mem_len, XATTN_WIDTH), lambda i: (i // tiles_per_seq, 0)),
            pl.BlockSpec((XATTN_WIDTH, d), const),
        ],
        out_specs=pl.BlockSpec((OUTX_TM, d), lambda i: (i, 0)),
        out_shape=jax.ShapeDtypeStruct((n, d), F32),
        compiler_params=_params("parallel"),
        name="outx",
    )(x1, og, om, wout_b, gx.reshape(1, d), wq_b, gq.reshape(1, -1), kx, vx, wo_b)


def kernel(x, mem, ffn1_norm, ffn1_w_gate, ffn1_w_up, ffn1_w_down, mix_norm, w_in, gla_w_gate2, gla_b_gate2, gla_out_norm, moba_q_norm, moba_k_norm, w_out, xattn_norm, mem_norm, xattn_w_q, xattn_w_kv, xattn_w_o, xattn_q_norm, xattn_k_norm, ffn2_norm, ffn2_w_gate, ffn2_w_up, ffn2_w_down):
    batch, seq, d = x.shape
    mem_len = mem.shape[1]
    depth = ffn1_norm.shape[0]
    n = batch * seq
    assert seq % GLA_TC == 0 and seq % (2 * MOBA_BLOCK) == 0 and seq % OUTX_TM == 0
    assert seq // MOBA_BLOCK + MOBA_DH <= MXU_DIM and (seq // MOBA_BLOCK) % BF16_SUBLANES == 0
    assert n % FFN_TM == 0 and n % NORM_TM == 0 and n % PROJ_TM == 0 and PROJ_TM % MOBA_BLOCK == 0

    xf = x.reshape(n, d)
    memf = mem.reshape(batch * mem_len, d)
    lr0 = GLA_COLS
    mq0 = lr0 + GLA_GATE_RANK
    w_in_t = jnp.swapaxes(w_in, 1, 2)
    for l in range(depth):
        w_lr_t = w_in_t[l, lr0:mq0]

        x1 = _ffn(xf, ffn1_norm[l], ffn1_w_gate[l], ffn1_w_up[l], ffn1_w_down[l])
        hn, la = _mixnorm(x1, mix_norm[l], w_lr_t, gla_w_gate2[l], gla_b_gate2[l])
        u, q_t, kb, kmean, v3 = _proj(hn, w_in_t, l, moba_q_norm[l], moba_k_norm[l])
        o_gla = _gla(u, la, gla_out_norm[l], batch, seq)
        o_moba = _moba(q_t, kb, kmean, v3, batch, seq)
        kx, vx = _memkv(memf, mem_norm[l], xattn_w_kv[l], xattn_k_norm[l])
        x3 = _outx(x1, o_gla, o_moba, w_out[l].astype(BF16), xattn_norm[l],
                   xattn_w_q[l].astype(BF16), xattn_q_norm[l], kx, vx,
                   xattn_w_o[l].astype(BF16), seq, mem_len)
        xf = _ffn(x3, ffn2_norm[l], ffn2_w_gate[l], ffn2_w_up[l], ffn2_w_down[l])
    return xf.reshape(batch, seq, d)
```

```python
import jax
import jax.numpy as jnp
from jax import lax
from jax.experimental import pallas as pl
from jax.experimental.pallas import tpu as pltpu

F32 = jnp.float32
BF16 = jnp.bfloat16

EPS = 1e-6
NEG_INF = -1e30
LOG2_E = 1.4426950408889634
D_FF = 5632
GLA_HEADS = 4
GLA_DV = 256
GLA_DK = 128
GLA_QK = GLA_HEADS * GLA_DK
GLA_WIDTH = GLA_HEADS * GLA_DV
GLA_COLS = 2 * GLA_QK + 2 * GLA_WIDTH
GLA_GATE_RANK = 16
GLA_GATE_TAU = 16.0
GLA_CHUNK = 64
MOBA_DH = 128
MOBA_HEADS = 8
MOBA_WIDTH = MOBA_HEADS * MOBA_DH
MOBA_BLOCK = 256
MOBA_TOPK = 3
XATTN_HEADS = 4
XATTN_DH = 128
XATTN_WIDTH = XATTN_HEADS * XATTN_DH

V7X_VMEM_BYTES = 64 * 1024 * 1024
VMEM_LIMIT = V7X_VMEM_BYTES - 6 * 1024 * 1024
FFN_VMEM_LIMIT = V7X_VMEM_BYTES - 2 * 1024 * 1024
MXU_DIM = 256

FFN_TM = 1024
FFN_TF = 512
NORM_TM = 1024
PROJ_TM = 2048
PROJ_TN = 512
GLA_TC = 512
GLA_SUB = 256
GLA_G = 4
MOBA_G = 4
MOBA_TQ = 2 * MOBA_BLOCK
BF16_SUBLANES = 16
MOBA_VPAD = BF16_SUBLANES
MOBA_VROWS = MOBA_DH + MOBA_VPAD
OUTX_TM = 512


def _params(*semantics):
    return pltpu.CompilerParams(dimension_semantics=semantics, vmem_limit_bytes=VMEM_LIMIT)


def _rms(x, g):
    return x * lax.rsqrt(jnp.mean(x * x, axis=-1, keepdims=True) + EPS) * g


def _dot(a, b):
    return jnp.dot(a, b, preferred_element_type=F32)


def _dot_nt(a, b):
    return lax.dot_general(a, b, (((1,), (1,)), ((), ())), preferred_element_type=F32)


def _silu(x):
    return x * (1.0 / (1.0 + jnp.exp(-x)))


def _ffn_kernel(x_hbm, g_ref, wg_ref, wu_ref, wd_ref, o_hbm, h_ref, xbuf_ref, acc_ref, xsem, osem):
    i = pl.program_id(0)
    f = pl.program_id(1)
    last_i = pl.num_programs(0) - 1
    last_f = pl.num_programs(1) - 1

    def x_copy(tile):
        return pltpu.make_async_copy(x_hbm.at[pl.ds(tile * FFN_TM, FFN_TM)], xbuf_ref, xsem)

    def o_copy(tile):
        return pltpu.make_async_copy(acc_ref, o_hbm.at[pl.ds(tile * FFN_TM, FFN_TM)], osem)

    @pl.when(jnp.logical_and(i == 0, f == 0))
    def _():
        x_copy(0).start()

    @pl.when(f == 0)
    def _():
        x_copy(i).wait()
        x = xbuf_ref[...]
        h_ref[...] = _rms(x, g_ref[...]).astype(BF16)

        @pl.when(i > 0)
        def _():
            o_copy(i - 1).wait()

        acc_ref[...] = xbuf_ref[...]

    @pl.when(jnp.logical_and(f == 1, i < last_i))
    def _():
        x_copy(i + 1).start()

    h = h_ref[...]
    gate = _dot(h, wg_ref[...].astype(BF16))
    up = _dot(h, wu_ref[...].astype(BF16))
    act = (0.5 * _silu(gate) * up).astype(BF16)
    acc_ref[...] += _dot(act, wd_ref[...].astype(BF16))

    @pl.when(f == last_f)
    def _():
        o_copy(i).start()

        @pl.when(i == last_i)
        def _():
            o_copy(i).wait()


def _ffn(x, g, wg, wu, wd):
    n, d = x.shape
    grid = (n // FFN_TM, D_FF // FFN_TF)
    assert grid[1] >= 2
    return pl.pallas_call(
        _ffn_kernel,
        grid=grid,
        in_specs=[
            pl.BlockSpec(memory_space=pl.ANY),
            pl.BlockSpec((1, d), lambda i, f: (0, 0)),
            pl.BlockSpec((d, FFN_TF), lambda i, f: (0, f)),
            pl.BlockSpec((d, FFN_TF), lambda i, f: (0, f)),
            pl.BlockSpec((FFN_TF, d), lambda i, f: (f, 0)),
        ],
        out_specs=pl.BlockSpec(memory_space=pl.ANY),
        out_shape=jax.ShapeDtypeStruct((n, d), F32),
        scratch_shapes=[pltpu.VMEM((FFN_TM, d), BF16), pltpu.VMEM((FFN_TM, d), F32),
                        pltpu.VMEM((FFN_TM, d), F32),
                        pltpu.SemaphoreType.DMA, pltpu.SemaphoreType.DMA],
        compiler_params=pltpu.CompilerParams(dimension_semantics=("arbitrary", "arbitrary"),
                                             vmem_limit_bytes=FFN_VMEM_LIMIT),
        name="ffn",
    )(x, g.reshape(1, d), wg, wu, wd)


def _mixnorm_kernel(x_ref, g_ref, wlr_ref, wg2_ref, bg2_ref, hn_ref, la_ref):
    h = _rms(x_ref[...], g_ref[...]).astype(BF16)
    hn_ref[...] = h
    lr = _dot_nt(h, wlr_ref[...].astype(BF16))
    pre = _dot(lr.astype(BF16), wg2_ref[...].astype(BF16)) + bg2_ref[...]
    log_sig = -(jnp.maximum(-pre, 0.0) + jnp.log(1.0 + jnp.exp(-jnp.abs(pre))))
    la_ref[...] = log_sig * (1.0 / GLA_GATE_TAU)


def _mixnorm(x1, g, w_lr_t, wg2, bg2):
    n, d = x1.shape
    const = lambda i: (0, 0)
    return pl.pallas_call(
        _mixnorm_kernel,
        grid=(n // NORM_TM,),
        in_specs=[
            pl.BlockSpec((NORM_TM, d), lambda i: (i, 0)),
            pl.BlockSpec((1, d), const),
            pl.BlockSpec((GLA_GATE_RANK, d), const),
            pl.BlockSpec((GLA_GATE_RANK, GLA_QK), const),
            pl.BlockSpec((1, GLA_QK), const),
        ],
        out_specs=[pl.BlockSpec((NORM_TM, d), lambda i: (i, 0)),
                   pl.BlockSpec((NORM_TM, GLA_QK), lambda i: (i, 0))],
        out_shape=[jax.ShapeDtypeStruct((n, d), BF16),
                   jax.ShapeDtypeStruct((n, GLA_QK), F32)],
        compiler_params=_params("parallel"),
        name="mixnorm",
    )(x1, g.reshape(1, d), w_lr_t, wg2, bg2.reshape(1, -1))


def _projg_kernel(hn_ref, wt_ref, u_ref):
    u_ref[...] = _dot_nt(hn_ref[...], wt_ref[0].astype(BF16))


def _projq_kernel(hn_ref, wt_ref, g_ref, qt_ref):
    yt = _dot_nt(wt_ref[0].astype(BF16), hn_ref[...])
    g = g_ref[...]
    parts = []
    for k in range(PROJ_TN // MOBA_DH):
        p = yt[k * MOBA_DH:(k + 1) * MOBA_DH, :]
        inv = lax.rsqrt(jnp.mean(p * p, axis=0, keepdims=True) + EPS)
        parts.append(p * inv * g)
    qt_ref[...] = jnp.concatenate(parts, axis=0)


def _projk_kernel(hn_ref, wt_ref, g_ref, kb_ref, kmean_ref):
    y = _dot_nt(hn_ref[...], wt_ref[0].astype(BF16))
    g = g_ref[...]
    parts = [_rms(y[:, k * MOBA_DH:(k + 1) * MOBA_DH], g) for k in range(PROJ_TN // MOBA_DH)]
    kn = jnp.concatenate(parts, axis=1)
    kb_ref[...] = kn.astype(BF16)
    blocks = PROJ_TM // MOBA_BLOCK
    kmean_ref[0] = jnp.sum(kn.reshape(blocks, MOBA_BLOCK, PROJ_TN), axis=1) * (1.0 / MOBA_BLOCK)


def _projv_kernel(hn_ref, wt_ref, v3_ref):
    yt = _dot_nt(wt_ref[0].astype(BF16), hn_ref[...]).astype(BF16)
    ones = jnp.ones((MOBA_VPAD, PROJ_TM), BF16)
    parts = []
    for k in range(PROJ_TN // MOBA_DH):
        parts += [yt[k * MOBA_DH:(k + 1) * MOBA_DH], ones]
    ya = jnp.concatenate(parts, axis=0)
    for t in range(PROJ_TM // MOBA_BLOCK):
        v3_ref[t] = ya[:, t * MOBA_BLOCK:(t + 1) * MOBA_BLOCK]


def _proj_kernel(hn_ref, wt_ref, gq_ref, gk_ref, u_ref, qt_ref, kb_ref, kmean_ref, v3_ref):
    j = pl.program_id(1)
    gla = GLA_COLS // PROJ_TN
    steps = MOBA_WIDTH // PROJ_TN

    @pl.when(j < gla)
    def _():
        _projg_kernel(hn_ref, wt_ref, u_ref)

    @pl.when(jnp.logical_and(j >= gla, j < gla + steps))
    def _():
        _projq_kernel(hn_ref, wt_ref, gq_ref, qt_ref)

    @pl.when(jnp.logical_and(j >= gla + steps, j < gla + 2 * steps))
    def _():
        _projk_kernel(hn_ref, wt_ref, gk_ref, kb_ref, kmean_ref)

    @pl.when(j >= gla + 2 * steps)
    def _():
        _projv_kernel(hn_ref, wt_ref, v3_ref)


def _proj(hn, w_in_t, layer, gq, gk):
    n, d = hn.shape
    gla = GLA_COLS // PROJ_TN
    steps = MOBA_WIDTH // PROJ_TN
    blocks = PROJ_TM // MOBA_BLOCK
    moba_gap = GLA_GATE_RANK
    own = lambda j, first: jnp.clip(j - gla - first * steps, 0, steps - 1)
    const = lambda i, j: (0, 0)
    u, q_t, kb, kmean, v3 = pl.pallas_call(
        _proj_kernel,
        grid=(n // PROJ_TM, gla + 3 * steps),
        in_specs=[
            pl.BlockSpec((PROJ_TM, d), lambda i, j: (i, 0)),
            pl.BlockSpec((pl.Element(1), pl.Element(PROJ_TN), pl.Element(d)),
                         lambda i, j: (layer,
                                       pl.multiple_of(j * PROJ_TN + jnp.where(j >= gla, moba_gap, 0), 8), 0)),
            pl.BlockSpec((MOBA_DH, 1), const),
            pl.BlockSpec((1, MOBA_DH), const),
        ],
        out_specs=[
            pl.BlockSpec((PROJ_TM, PROJ_TN), lambda i, j: (i, jnp.minimum(j, gla - 1))),
            pl.BlockSpec((PROJ_TN, PROJ_TM), lambda i, j: (own(j, 0), i)),
            pl.BlockSpec((PROJ_TM, PROJ_TN), lambda i, j: (i, own(j, 1))),
            pl.BlockSpec((1, blocks, PROJ_TN), lambda i, j: (i, 0, own(j, 1))),
            pl.BlockSpec((blocks, PROJ_TN // MOBA_DH * MOBA_VROWS, MOBA_BLOCK),
                         lambda i, j: (i, own(j, 2), 0)),
        ],
        out_shape=[jax.ShapeDtypeStruct((n, GLA_COLS), F32),
                   jax.ShapeDtypeStruct((MOBA_WIDTH, n), F32),
                   jax.ShapeDtypeStruct((n, MOBA_WIDTH), BF16),
                   jax.ShapeDtypeStruct((n // PROJ_TM, blocks, MOBA_WIDTH), F32),
                   jax.ShapeDtypeStruct((n // MOBA_BLOCK, MOBA_HEADS * MOBA_VROWS, MOBA_BLOCK), BF16)],
        compiler_params=_params("arbitrary", "arbitrary"),
        name="proj",
    )(hn, w_in_t, gq.reshape(-1, 1), gk.reshape(1, -1))
    return u, q_t, kb, kmean.reshape(n // MOBA_BLOCK, MOBA_WIDTH), v3


def _dot_exact_rhs(m, a):
    a1 = a.astype(BF16)
    r1 = a - a1.astype(F32)
    a2 = r1.astype(BF16)
    a3 = (r1 - a2.astype(F32)).astype(BF16)
    return _dot(m, a1) + _dot(m, a2) + _dot(m, a3)


def _gla_head(q, k, v, r, la, g_out, state, upd_ref):
    tc = GLA_TC
    nchunk = tc // GLA_CHUNK
    sub = GLA_SUB

    crow = lax.broadcasted_iota(jnp.int32, (GLA_CHUNK, GLA_CHUNK), 0)
    ccol = lax.broadcasted_iota(jnp.int32, (GLA_CHUNK, GLA_CHUNK), 1)
    tri = (ccol <= crow).astype(BF16)
    la_wide = jnp.concatenate([la[c * GLA_CHUNK:(c + 1) * GLA_CHUNK] for c in range(nchunk)], axis=1)
    bc_wide = _dot_exact_rhs(tri, la_wide)
    bcum = jnp.concatenate([bc_wide[:, c * GLA_DK:(c + 1) * GLA_DK] for c in range(nchunk)], axis=0)
    btot = jnp.concatenate(
        [jnp.broadcast_to(bc_wide[GLA_CHUNK - 1:GLA_CHUNK, c * GLA_DK:(c + 1) * GLA_DK],
                          (GLA_CHUNK, GLA_DK)) for c in range(nchunk)], axis=0)

    q_dec_b = ((q * (GLA_DK ** -0.5)) * jnp.exp(bcum)).astype(BF16)
    k_inv_b = (k * jnp.exp(-bcum)).astype(BF16)
    k_tail_t = (k * jnp.exp(btot - bcum)).T.astype(BF16)
    decay_t = jnp.exp(btot).T
    vb = v.astype(BF16)

    row = lax.broadcasted_iota(jnp.int32, (sub, sub), 0)
    col = lax.broadcasted_iota(jnp.int32, (sub, sub), 1)
    causal = jnp.logical_and((row // GLA_CHUNK) == (col // GLA_CHUNK), col <= row)
    intra = []
    for p in range(tc // sub):
        rows = slice(p * sub, (p + 1) * sub)
        att = jnp.where(causal, _dot_nt(q_dec_b[rows], k_inv_b[rows]), 0.0)
        intra.append(_dot(att.astype(BF16), vb[rows]))

    for c in range(nchunk):
        rows = slice(c * GLA_CHUNK, (c + 1) * GLA_CHUNK)
        upd_ref[c] = _dot(k_tail_t[:, rows], vb[rows])

    inter = []
    for c in range(nchunk):
        rows = slice(c * GLA_CHUNK, (c + 1) * GLA_CHUNK)
        inter.append(_dot(q_dec_b[rows], state.astype(BF16)))
        dec_c = decay_t[:, c * GLA_CHUNK:c * GLA_CHUNK + 1]
        state = dec_c * state + upd_ref[c]
    o = jnp.concatenate(intra, axis=0) + jnp.concatenate(inter, axis=0)
    return _rms(o, g_out) * _silu(r), state


def _gla_kernel(q_ref, k_ref, v_ref, r_ref, la_ref, go_ref, o_ref, s_ref, upd_ref):
    @pl.when(pl.program_id(2) == 0)
    def _():
        s_ref[...] = jnp.zeros_like(s_ref)

    for g in range(GLA_G):
        qk = slice(g * GLA_DK, (g + 1) * GLA_DK)
        vr = slice(g * GLA_DV, (g + 1) * GLA_DV)
        o, state = _gla_head(q_ref[:, qk], k_ref[:, qk], v_ref[:, vr], r_ref[:, vr], la_ref[:, qk],
                             go_ref[...], s_ref[g], upd_ref.at[g])
        s_ref[g] = state
        o_ref[:, vr] = o


def _gla(u, la, g_out, batch, seq):
    n = u.shape[0]
    nt = seq // GLA_TC
    tok = lambda b, h, t: b * nt + t
    qk_w, v_w = GLA_G * GLA_DK, GLA_G * GLA_DV
    k_blk0 = GLA_QK // qk_w
    v_blk0 = 2 * GLA_QK // v_w
    r_blk0 = v_blk0 + GLA_WIDTH // v_w
    return pl.pallas_call(
        _gla_kernel,
        grid=(batch, GLA_HEADS // GLA_G, nt),
        in_specs=[
            pl.BlockSpec((GLA_TC, qk_w), lambda b, h, t: (tok(b, h, t), h)),
            pl.BlockSpec((GLA_TC, qk_w), lambda b, h, t: (tok(b, h, t), k_blk0 + h)),
            pl.BlockSpec((GLA_TC, v_w), lambda b, h, t: (tok(b, h, t), v_blk0 + h)),
            pl.BlockSpec((GLA_TC, v_w), lambda b, h, t: (tok(b, h, t), r_blk0 + h)),
            pl.BlockSpec((GLA_TC, qk_w), lambda b, h, t: (tok(b, h, t), h)),
            pl.BlockSpec((1, GLA_DV), lambda b, h, t: (0, 0)),
        ],
        out_specs=pl.BlockSpec((GLA_TC, v_w), lambda b, h, t: (tok(b, h, t), h)),
        out_shape=jax.ShapeDtypeStruct((n, GLA_WIDTH), F32),
        scratch_shapes=[pltpu.VMEM((GLA_G, GLA_DK, GLA_DV), F32),
                        pltpu.VMEM((GLA_G, GLA_TC // GLA_CHUNK, GLA_DK, GLA_DV), F32)],
        compiler_params=_params("parallel", "parallel", "arbitrary"),
        name="gla",
    )(u, u, u, u, la, g_out.reshape(1, -1))


def _moba_kernel(qt_ref, kb_ref, v3_ref, kmean_ref, o_ref, kaug_ref, snext_ref):
    ti = pl.program_id(2)
    last_tile = pl.num_programs(2) - 1
    bs = MOBA_BLOCK
    tq = MOBA_TQ
    nb = kb_ref.shape[0] // bs
    dh = MOBA_DH
    vr = MOBA_VROWS
    scale = (dh ** -0.5) * LOG2_E

    @pl.when(ti == 0)
    def _():
        seq = kb_ref.shape[0]
        row_blk = lax.broadcasted_iota(jnp.int32, (seq, MXU_DIM - dh), 0) // bs
        lane = lax.broadcasted_iota(jnp.int32, (seq, MXU_DIM - dh), 1)
        onehot = (row_blk == lane).astype(BF16)
        for g in range(MOBA_G):
            kaug_ref[g, :, :dh] = kb_ref[:, g * dh:(g + 1) * dh]
            kaug_ref[g, :, dh:] = onehot

    blk = lax.broadcasted_iota(jnp.int32, (nb, tq), 0)
    q_blk = 2 * ti + (lax.broadcasted_iota(jnp.int32, (nb, tq), 1) // bs)
    past = blk < q_blk
    causal = (lax.broadcasted_iota(jnp.int32, (bs, bs), 0)
              <= lax.broadcasted_iota(jnp.int32, (bs, bs), 1))
    diag = pl.multiple_of(ti * tq, tq)

    def pv_dot(g, t, p):
        return (_dot(v3_ref[2 * t, g * vr:(g + 1) * vr, :], p[:bs])
                + _dot(v3_ref[2 * t + 1, g * vr:(g + 1) * vr, :], p[bs:]))

    q_aug, biases, state = [], [], []

    def score(g, t):
        off = pl.multiple_of(t * tq, tq)
        s = _dot(kaug_ref[g, pl.ds(off, tq), :], q_aug[g])
        return s, jnp.max(s, axis=0, keepdims=True)

    for g in range(MOBA_G):
        qt = qt_ref[g * dh:(g + 1) * dh, :]
        gate = jnp.dot(kmean_ref[:, g * dh:(g + 1) * dh], qt,
                       precision=lax.Precision.HIGHEST, preferred_element_type=F32)
        gate = jnp.where(past, gate, NEG_INF)
        rank = jnp.zeros((nb, tq), jnp.int32)
        for j in range(nb):
            gj = gate[j:j + 1, :]
            ahead = jnp.logical_or(gj > gate, jnp.logical_and(gj == gate, blk > j))
            rank = rank + ahead.astype(jnp.int32)
        chosen = jnp.logical_and(past, rank < MOBA_TOPK)
        bias = jnp.where(chosen, 0.0, NEG_INF)
        qa = jnp.concatenate([(qt * scale).astype(BF16), bias.astype(BF16),
                              jnp.zeros((MXU_DIM - dh - nb, tq), BF16)], axis=0)
        q_aug.append(qa)
        biases.append(bias)

    first_maxes = []
    for g in range(MOBA_G):
        s_first, first_max = score(g, 0)
        snext_ref[g] = s_first
        first_maxes.append(first_max)

    for g in range(MOBA_G):
        qa, bias = q_aug[g], biases[g]
        first_bias = jnp.sum(jnp.where(blk == 2 * ti, bias, 0.0), axis=0, keepdims=True)[:, bs:]
        s0 = _dot(kaug_ref[g, pl.ds(diag, bs), :dh], qa[:dh])
        s00 = jnp.where(causal, s0[:, :bs], NEG_INF)
        s01 = s0[:, bs:] + first_bias
        s11 = jnp.where(causal, _dot(kaug_ref[g, pl.ds(pl.multiple_of(diag + bs, bs), bs), :dh],
                                     qa[:dh, bs:]), NEG_INF)
        m0 = jnp.max(s00, axis=0, keepdims=True)
        m1 = jnp.maximum(jnp.max(s01, axis=0, keepdims=True), jnp.max(s11, axis=0, keepdims=True))
        p0 = jnp.concatenate([jnp.exp2(s00 - m0), jnp.exp2(s01 - m1)], axis=1).astype(BF16)
        p11 = jnp.exp2(s11 - m1).astype(BF16)
        acc0 = _dot(v3_ref[2 * ti, g * vr:(g + 1) * vr, :], p0)
        acc1 = _dot(v3_ref[2 * ti + 1, g * vr:(g + 1) * vr, :], p11)
        acc = jnp.concatenate([acc0[:, :bs], acc0[:, bs:] + acc1], axis=1)
        state.append((jnp.concatenate([m0, m1], axis=1), acc, first_maxes[g]))

    def consume(g, t, s, tile_max, m, acc, valid=None):
        if valid is not None:
            tile_max = jnp.where(valid, tile_max, NEG_INF)
        m_new = jnp.maximum(m, tile_max)
        alpha = jnp.exp2(m - m_new)
        shift = m_new if valid is None else jnp.where(valid, m_new, -NEG_INF)
        p = jnp.exp2(s - shift).astype(BF16)
        return m_new, alpha * acc + pv_dot(g, t, p)

    def body(u, carry):
        out = []
        for g in range(MOBA_G):
            m, acc, max_a = carry[g]
            s_b, max_b = score(g, 2 * u + 1)
            m, acc = consume(g, 2 * u, snext_ref[g], max_a, m, acc)
            s_c, max_c = score(g, jnp.minimum(2 * u + 2, last_tile))
            snext_ref[g] = s_c
            m, acc = consume(g, 2 * u + 1, s_b, max_b, m, acc, valid=2 * u + 1 < ti)
            out.append((m, acc, max_c))
        return tuple(out)

    carry = lax.fori_loop(0, (ti + 1) // 2, body, tuple(state))
    for g in range(MOBA_G):
        m, acc, _ = carry[g]
        o_ref[:, g * dh:(g + 1) * dh] = (acc[:dh] * (1.0 / acc[dh:dh + 1])).T


def _moba(q_t, kb, kmean, v3, batch, seq):
    n = kb.shape[0]
    nb = seq // MOBA_BLOCK
    nt = seq // MOBA_TQ
    gw = MOBA_G * MOBA_DH
    return pl.pallas_call(
        _moba_kernel,
        grid=(batch, MOBA_HEADS // MOBA_G, nt),
        in_specs=[
            pl.BlockSpec((gw, MOBA_TQ), lambda b, h, i: (h, b * nt + i)),
            pl.BlockSpec((seq, gw), lambda b, h, i: (b, h)),
            pl.BlockSpec((nb, MOBA_G * MOBA_VROWS, MOBA_BLOCK), lambda b, h, i: (b, h, 0)),
            pl.BlockSpec((nb, gw), lambda b, h, i: (b, h)),
        ],
        out_specs=pl.BlockSpec((MOBA_TQ, gw), lambda b, h, i: (b * nt + i, h)),
        out_shape=jax.ShapeDtypeStruct((n, MOBA_WIDTH), F32),
        scratch_shapes=[pltpu.VMEM((MOBA_G, seq, MXU_DIM), BF16),
                        pltpu.VMEM((MOBA_G, MOBA_TQ, MOBA_TQ), F32)],
        compiler_params=_params("parallel", "parallel", "arbitrary"),
        name="moba",
    )(q_t, kb, v3, kmean)


def _memkv_kernel(mem_ref, g_ref, wkv_ref, gk_ref, k_ref, v_ref):
    h = _rms(mem_ref[...], g_ref[...]).astype(BF16)
    kv = _dot(h, wkv_ref[...].astype(BF16))
    gk = gk_ref[...]
    ks = [_rms(kv[:, i * XATTN_DH:(i + 1) * XATTN_DH], gk) for i in range(XATTN_HEADS)]
    k_ref[...] = jnp.concatenate(ks, axis=1)
    v_ref[...] = kv[:, XATTN_WIDTH:]


def _memkv(mem, g, wkv_b, gk):
    m, d = mem.shape
    return pl.pallas_call(
        _memkv_kernel,
        out_shape=[jax.ShapeDtypeStruct((m, XATTN_WIDTH), F32),
                   jax.ShapeDtypeStruct((m, XATTN_WIDTH), F32)],
        compiler_params=pltpu.CompilerParams(vmem_limit_bytes=VMEM_LIMIT),
        name="memkv",
    )(mem, g.reshape(1, d), wkv_b, gk.reshape(1, -1))


def _outx_kernel(x1_ref, og_ref, om_ref, wout_ref, gx_ref, wq_ref, gq_ref, kx_ref, vx_ref, wo_ref,
                 o_ref):
    x2 = (x1_ref[...]
          + _dot(og_ref[...].astype(BF16), wout_ref[:GLA_WIDTH, :])
          + _dot(om_ref[...].astype(BF16), wout_ref[GLA_WIDTH:, :]))
    h = _rms(x2, gx_ref[...]).astype(BF16)
    q = _dot(h, wq_ref[...])
    gq = gq_ref[...]
    scale = XATTN_DH ** -0.5
    heads = []
    for i in range(XATTN_HEADS):
        cols = slice(i * XATTN_DH, (i + 1) * XATTN_DH)
        qh = _rms(q[:, cols], gq).astype(BF16)
        s = _dot_nt(qh, kx_ref[:, cols].astype(BF16)) * scale
        p = jnp.exp(s - jnp.max(s, axis=-1, keepdims=True))
        p = p * (1.0 / jnp.sum(p, axis=-1, keepdims=True))
        heads.append(_dot(p.astype(BF16), vx_ref[:, cols].astype(BF16)))
    o = jnp.concatenate(heads, axis=1).astype(BF16)
    o_ref[...] = x2 + _dot(o, wo_ref[...])


def _outx(x1, og, om, wout_b, gx, wq_b, gq, kx, vx, wo_b, seq, mem_len):
    n, d = x1.shape
    tiles_per_seq = seq // OUTX_TM
    const = lambda i: (0, 0)
    return pl.pallas_call(
        _outx_kernel,
        grid=(n // OUTX_TM,),
        in_specs=[
            pl.BlockSpec((OUTX_TM, d), lambda i: (i, 0)),
            pl.BlockSpec((OUTX_TM, GLA_WIDTH), lambda i: (i, 0)),
            pl.BlockSpec((OUTX_TM, MOBA_WIDTH), lambda i: (i, 0)),
            pl.BlockSpec((GLA_WIDTH + MOBA_WIDTH, d), const),
            pl.BlockSpec((1, d), const),
            pl.BlockSpec((d, XATTN_WIDTH), const),
            pl.BlockSpec((1, XATTN_DH), const),
            pl.BlockSpec((mem_len, XATTN_WIDTH), lambda i: (i // tiles_per_seq, 0)),
            pl.BlockSpec((mem_len, XATTN_WIDTH), lambda i: (i // tiles_per_seq, 0)),
            pl.BlockSpec((XATTN_WIDTH, d), const),
        ],
        out_specs=pl.BlockSpec((OUTX_TM, d), lambda i: (i, 0)),
        out_shape=jax.ShapeDtypeStruct((n, d), F32),
        compiler_params=_params("parallel"),
        name="outx",
    )(x1, og, om, wout_b, gx.reshape(1, d), wq_b, gq.reshape(1, -1), kx, vx, wo_b)


def kernel(x, mem, ffn1_norm, ffn1_w_gate, ffn1_w_up, ffn1_w_down, mix_norm, w_in, gla_w_gate2, gla_b_gate2, gla_out_norm, moba_q_norm, moba_k_norm, w_out, xattn_norm, mem_norm, xattn_w_q, xattn_w_kv, xattn_w_o, xattn_q_norm, xattn_k_norm, ffn2_norm, ffn2_w_gate, ffn2_w_up, ffn2_w_down):
    batch, seq, d = x.shape
    mem_len = mem.shape[1]
    depth = ffn1_norm.shape[0]
    n = batch * seq
    assert seq % GLA_TC == 0 and seq % (2 * MOBA_BLOCK) == 0 and seq % OUTX_TM == 0
    assert seq // MOBA_BLOCK + MOBA_DH <= MXU_DIM and (seq // MOBA_BLOCK) % BF16_SUBLANES == 0
    assert n % FFN_TM == 0 and n % NORM_TM == 0 and n % PROJ_TM == 0 and PROJ_TM % MOBA_BLOCK == 0

    xf = x.reshape(n, d)
    memf = mem.reshape(batch * mem_len, d)
    lr0 = GLA_COLS
    mq0 = lr0 + GLA_GATE_RANK
    w_in_t = jnp.swapaxes(w_in, 1, 2)
    for l in range(depth):
        w_lr_t = w_in_t[l, lr0:mq0]

        x1 = _ffn(xf, ffn1_norm[l], ffn1_w_gate[l], ffn1_w_up[l], ffn1_w_down[l])
        hn, la = _mixnorm(x1, mix_norm[l], w_lr_t, gla_w_gate2[l], gla_b_gate2[l])
        u, q_t, kb, kmean, v3 = _proj(hn, w_in_t, l, moba_q_norm[l], moba_k_norm[l])
        o_gla = _gla(u, la, gla_out_norm[l], batch, seq)
        o_moba = _moba(q_t, kb, kmean, v3, batch, seq)
        kx, vx = _memkv(memf, mem_norm[l], xattn_w_kv[l], xattn_k_norm[l])
        x3 = _outx(x1, o_gla, o_moba, w_out[l].astype(BF16), xattn_norm[l],
                   xattn_w_q[l].astype(BF16), xattn_q_norm[l], kx, vx,
                   xattn_w_o[l].astype(BF16), seq, mem_len)
        xf = _ffn(x3, ffn2_norm[l], ffn2_w_gate[l], ffn2_w_up[l], ffn2_w_down[l])
    return xf.reshape(batch, seq, d)
```

```python
import jax
import jax.numpy as jnp
from jax import lax
from jax.experimental import pallas as pl
from jax.experimental.pallas import tpu as pltpu

F32 = jnp.float32
BF16 = jnp.bfloat16

EPS = 1e-6
NEG_INF = -1e30
LOG2_E = 1.4426950408889634
D_FF = 5632
GLA_HEADS = 4
GLA_DV = 256
GLA_DK = 128
GLA_QK = GLA_HEADS * GLA_DK
GLA_WIDTH = GLA_HEADS * GLA_DV
GLA_COLS = 2 * GLA_QK + 2 * GLA_WIDTH
GLA_GATE_RANK = 16
GLA_GATE_TAU = 16.0
GLA_CHUNK = 64
MOBA_DH = 128
MOBA_HEADS = 8
MOBA_WIDTH = MOBA_HEADS * MOBA_DH
MOBA_BLOCK = 256
MOBA_TOPK = 3
XATTN_HEADS = 4
XATTN_DH = 128
XATTN_WIDTH = XATTN_HEADS * XATTN_DH

V7X_VMEM_BYTES = 64 * 1024 * 1024
VMEM_LIMIT = V7X_VMEM_BYTES - 6 * 1024 * 1024
FFN_VMEM_LIMIT = V7X_VMEM_BYTES - 2 * 1024 * 1024
MXU_DIM = 256

FFN_TM = 1024
FFN_TF = 512
NORM_TM = 1024
PROJ_TM = 2048
PROJ_TN = 512
GLA_TC = 512
GLA_SUB = 256
GLA_G = 4
MOBA_G = 4
MOBA_TQ = 2 * MOBA_BLOCK
BF16_SUBLANES = 16
MOBA_VPAD = BF16_SUBLANES
MOBA_VROWS = MOBA_DH + MOBA_VPAD
OUTX_TM = 512


def _params(*semantics):
    return pltpu.CompilerParams(dimension_semantics=semantics, vmem_limit_bytes=VMEM_LIMIT)


def _rms(x, g):
    return x * lax.rsqrt(jnp.mean(x * x, axis=-1, keepdims=True) + EPS) * g


def _dot(a, b):
    return jnp.dot(a, b, preferred_element_type=F32)


def _dot_nt(a, b):
    return lax.dot_general(a, b, (((1,), (1,)), ((), ())), preferred_element_type=F32)


def _silu(x):
    return x * (1.0 / (1.0 + jnp.exp(-x)))


def _ffn_kernel(x_hbm, g_ref, wg_ref, wu_ref, wd_ref, o_hbm, h_ref, xbuf_ref, acc_ref, xsem, osem):
    i = pl.program_id(0)
    f = pl.program_id(1)
    last_i = pl.num_programs(0) - 1
    last_f = pl.num_programs(1) - 1

    def x_copy(tile):
        return pltpu.make_async_copy(x_hbm.at[pl.ds(tile * FFN_TM, FFN_TM)], xbuf_ref, xsem)

    def o_copy(tile):
        return pltpu.make_async_copy(acc_ref, o_hbm.at[pl.ds(tile * FFN_TM, FFN_TM)], osem)

    @pl.when(jnp.logical_and(i == 0, f == 0))
    def _():
        x_copy(0).start()

    @pl.when(f == 0)
    def _():
        x_copy(i).wait()
        x = xbuf_ref[...]
        h_ref[...] = _rms(x, g_ref[...]).astype(BF16)

        @pl.when(i > 0)
        def _():
            o_copy(i - 1).wait()

        acc_ref[...] = xbuf_ref[...]

    @pl.when(jnp.logical_and(f == 1, i < last_i))
    def _():
        x_copy(i + 1).start()

    h = h_ref[...]
    gate = _dot(h, wg_ref[...].astype(BF16))
    up = _dot(h, wu_ref[...].astype(BF16))
    act = (0.5 * _silu(gate) * up).astype(BF16)
    acc_ref[...] += _dot(act, wd_ref[...].astype(BF16))

    @pl.when(f == last_f)
    def _():
        o_copy(i).start()

        @pl.when(i == last_i)
        def _():
            o_copy(i).wait()


def _ffn(x, g, wg, wu, wd):
    n, d = x.shape
    grid = (n // FFN_TM, D_FF // FFN_TF)
    assert grid[1] >= 2
    return pl.pallas_call(
        _ffn_kernel,
        grid=grid,
        in_specs=[
            pl.BlockSpec(memory_space=pl.ANY),
            pl.BlockSpec((1, d), lambda i, f: (0, 0)),
            pl.BlockSpec((d, FFN_TF), lambda i, f: (0, f)),
            pl.BlockSpec((d, FFN_TF), lambda i, f: (0, f)),
            pl.BlockSpec((FFN_TF, d), lambda i, f: (f, 0)),
        ],
        out_specs=pl.BlockSpec(memory_space=pl.ANY),
        out_shape=jax.ShapeDtypeStruct((n, d), F32),
        scratch_shapes=[pltpu.VMEM((FFN_TM, d), BF16), pltpu.VMEM((FFN_TM, d), F32),
                        pltpu.VMEM((FFN_TM, d), F32),
                        pltpu.SemaphoreType.DMA, pltpu.SemaphoreType.DMA],
        compiler_params=pltpu.CompilerParams(dimension_semantics=("arbitrary", "arbitrary"),
                                             vmem_limit_bytes=FFN_VMEM_LIMIT),
        name="ffn",
    )(x, g.reshape(1, d), wg, wu, wd)


def _mixnorm_kernel(x_ref, g_ref, wlr_ref, wg2_ref, bg2_ref, hn_ref, la_ref):
    h = _rms(x_ref[...], g_ref[...]).astype(BF16)
    hn_ref[...] = h
    lr = _dot_nt(h, wlr_ref[...].astype(BF16))
    pre = _dot(lr.astype(BF16), wg2_ref[...].astype(BF16)) + bg2_ref[...]
    log_sig = -(jnp.maximum(-pre, 0.0) + jnp.log(1.0 + jnp.exp(-jnp.abs(pre))))
    la_ref[...] = log_sig * (1.0 / GLA_GATE_TAU)


def _mixnorm(x1, g, w_lr_t, wg2, bg2):
    n, d = x1.shape
    const = lambda i: (0, 0)
    return pl.pallas_call(
        _mixnorm_kernel,
        grid=(n // NORM_TM,),
        in_specs=[
            pl.BlockSpec((NORM_TM, d), lambda i: (i, 0)),
            pl.BlockSpec((1, d), const),
            pl.BlockSpec((GLA_GATE_RANK, d), const),
            pl.BlockSpec((GLA_GATE_RANK, GLA_QK), const),
            pl.BlockSpec((1, GLA_QK), const),
        ],
        out_specs=[pl.BlockSpec((NORM_TM, d), lambda i: (i, 0)),
                   pl.BlockSpec((NORM_TM, GLA_QK), lambda i: (i, 0))],
        out_shape=[jax.ShapeDtypeStruct((n, d), BF16),
                   jax.ShapeDtypeStruct((n, GLA_QK), F32)],
        compiler_params=_params("parallel"),
        name="mixnorm",
    )(x1, g.reshape(1, d), w_lr_t, wg2, bg2.reshape(1, -1))


def _projg_kernel(hn_ref, wt_ref, u_ref):
    u_ref[...] = _dot_nt(hn_ref[...], wt_ref[0].astype(BF16))


def _projq_kernel(hn_ref, wt_ref, g_ref, qt_ref):
    yt = _dot_nt(wt_ref[0].astype(BF16), hn_ref[...])
    g = g_ref[...]
    parts = []
    for k in range(PROJ_TN // MOBA_DH):
        p = yt[k * MOBA_DH:(k + 1) * MOBA_DH, :]
        inv = lax.rsqrt(jnp.mean(p * p, axis=0, keepdims=True) + EPS)
        parts.append(p * inv * g)
    qt_ref[...] = jnp.concatenate(parts, axis=0)


def _projk_kernel(hn_ref, wt_ref, g_ref, kb_ref, kmean_ref):
    y = _dot_nt(hn_ref[...], wt_ref[0].astype(BF16))
    g = g_ref[...]
    parts = [_rms(y[:, k * MOBA_DH:(k + 1) * MOBA_DH], g) for k in range(PROJ_TN // MOBA_DH)]
    kn = jnp.concatenate(parts, axis=1)
    kb_ref[...] = kn.astype(BF16)
    blocks = PROJ_TM // MOBA_BLOCK
    kmean_ref[0] = jnp.sum(kn.reshape(blocks, MOBA_BLOCK, PROJ_TN), axis=1) * (1.0 / MOBA_BLOCK)


def _projv_kernel(hn_ref, wt_ref, v3_ref):
    yt = _dot_nt(wt_ref[0].astype(BF16), hn_ref[...]).astype(BF16)
    ones = jnp.ones((MOBA_VPAD, PROJ_TM), BF16)
    parts = []
    for k in range(PROJ_TN // MOBA_DH):
        parts += [yt[k * MOBA_DH:(k + 1) * MOBA_DH], ones]
    ya = jnp.concatenate(parts, axis=0)
    for t in range(PROJ_TM // MOBA_BLOCK):
        v3_ref[t] = ya[:, t * MOBA_BLOCK:(t + 1) * MOBA_BLOCK]


def _proj_kernel(hn_ref, wt_ref, gq_ref, gk_ref, u_ref, qt_ref, kb_ref, kmean_ref, v3_ref):
    j = pl.program_id(1)
    gla = GLA_COLS // PROJ_TN
    steps = MOBA_WIDTH // PROJ_TN

    @pl.when(j < gla)
    def _():
        _projg_kernel(hn_ref, wt_ref, u_ref)

    @pl.when(jnp.logical_and(j >= gla, j < gla + steps))
    def _():
        _projq_kernel(hn_ref, wt_ref, gq_ref, qt_ref)

    @pl.when(jnp.logical_and(j >= gla + steps, j < gla + 2 * steps))
    def _():
        _projk_kernel(hn_ref, wt_ref, gk_ref, kb_ref, kmean_ref)

    @pl.when(j >= gla + 2 * steps)
    def _():
        _projv_kernel(hn_ref, wt_ref, v3_ref)


def _proj(hn, w_in_t, layer, gq, gk):
    n, d = hn.shape
    gla = GLA_COLS // PROJ_TN
    steps = MOBA_WIDTH // PROJ_TN
    blocks = PROJ_TM // MOBA_BLOCK
    moba_gap = GLA_GATE_RANK
    own = lambda j, first: jnp.clip(j - gla - first * steps, 0, steps - 1)
    const = lambda i, j: (0, 0)
    u, q_t, kb, kmean, v3 = pl.pallas_call(
        _proj_kernel,
        grid=(n // PROJ_TM, gla + 3 * steps),
        in_specs=[
            pl.BlockSpec((PROJ_TM, d), lambda i, j: (i, 0)),
            pl.BlockSpec((pl.Element(1), pl.Element(PROJ_TN), pl.Element(d)),
                         lambda i, j: (layer,
                                       pl.multiple_of(j * PROJ_TN + jnp.where(j >= gla, moba_gap, 0), 8), 0)),
            pl.BlockSpec((MOBA_DH, 1), const),
            pl.BlockSpec((1, MOBA_DH), const),
        ],
        out_specs=[
            pl.BlockSpec((PROJ_TM, PROJ_TN), lambda i, j: (i, jnp.minimum(j, gla - 1))),
            pl.BlockSpec((PROJ_TN, PROJ_TM), lambda i, j: (own(j, 0), i)),
            pl.BlockSpec((PROJ_TM, PROJ_TN), lambda i, j: (i, own(j, 1))),
            pl.BlockSpec((1, blocks, PROJ_TN), lambda i, j: (i, 0, own(j, 1))),
            pl.BlockSpec((blocks, PROJ_TN // MOBA_DH * MOBA_VROWS, MOBA_BLOCK),
                         lambda i, j: (i, own(j, 2), 0)),
        ],
        out_shape=[jax.ShapeDtypeStruct((n, GLA_COLS), F32),
                   jax.ShapeDtypeStruct((MOBA_WIDTH, n), F32),
                   jax.ShapeDtypeStruct((n, MOBA_WIDTH), BF16),
                   jax.ShapeDtypeStruct((n // PROJ_TM, blocks, MOBA_WIDTH), F32),
                   jax.ShapeDtypeStruct((n // MOBA_BLOCK, MOBA_HEADS * MOBA_VROWS, MOBA_BLOCK), BF16)],
        compiler_params=_params("arbitrary", "arbitrary"),
        name="proj",
    )(hn, w_in_t, gq.reshape(-1, 1), gk.reshape(1, -1))
    return u, q_t, kb, kmean.reshape(n // MOBA_BLOCK, MOBA_WIDTH), v3


def _dot_exact_rhs(m, a):
    a1 = a.astype(BF16)
    r1 = a - a1.astype(F32)
    a2 = r1.astype(BF16)
    a3 = (r1 - a2.astype(F32)).astype(BF16)
    return _dot(m, a1) + _dot(m, a2) + _dot(m, a3)


def _gla_head(q, k, v, r, la, g_out, state, upd_ref):
    tc = GLA_TC
    nchunk = tc // GLA_CHUNK
    sub = GLA_SUB

    crow = lax.broadcasted_iota(jnp.int32, (GLA_CHUNK, GLA_CHUNK), 0)
    ccol = lax.broadcasted_iota(jnp.int32, (GLA_CHUNK, GLA_CHUNK), 1)
    tri = (ccol <= crow).astype(BF16)
    la_wide = jnp.concatenate([la[c * GLA_CHUNK:(c + 1) * GLA_CHUNK] for c in range(nchunk)], axis=1)
    bc_wide = _dot_exact_rhs(tri, la_wide)
    bcum = jnp.concatenate([bc_wide[:, c * GLA_DK:(c + 1) * GLA_DK] for c in range(nchunk)], axis=0)
    btot = jnp.concatenate(
        [jnp.broadcast_to(bc_wide[GLA_CHUNK - 1:GLA_CHUNK, c * GLA_DK:(c + 1) * GLA_DK],
                          (GLA_CHUNK, GLA_DK)) for c in range(nchunk)], axis=0)

    q_dec_b = ((q * (GLA_DK ** -0.5)) * jnp.exp(bcum)).astype(BF16)
    k_inv_b = (k * jnp.exp(-bcum)).astype(BF16)
    k_tail_t = (k * jnp.exp(btot - bcum)).T.astype(BF16)
    decay_t = jnp.exp(btot).T
    vb = v.astype(BF16)

    row = lax.broadcasted_iota(jnp.int32, (sub, sub), 0)
    col = lax.broadcasted_iota(jnp.int32, (sub, sub), 1)
    causal = jnp.logical_and((row // GLA_CHUNK) == (col // GLA_CHUNK), col <= row)
    intra = []
    for p in range(tc // sub):
        rows = slice(p * sub, (p + 1) * sub)
        att = jnp.where(causal, _dot_nt(q_dec_b[rows], k_inv_b[rows]), 0.0)
        intra.append(_dot(att.astype(BF16), vb[rows]))

    for c in range(nchunk):
        rows = slice(c * GLA_CHUNK, (c + 1) * GLA_CHUNK)
        upd_ref[c] = _dot(k_tail_t[:, rows], vb[rows])

    inter = []
    for c in range(nchunk):
        rows = slice(c * GLA_CHUNK, (c + 1) * GLA_CHUNK)
        inter.append(_dot(q_dec_b[rows], state.astype(BF16)))
        dec_c = decay_t[:, c * GLA_CHUNK:c * GLA_CHUNK + 1]
        state = dec_c * state + upd_ref[c]
    o = jnp.concatenate(intra, axis=0) + jnp.concatenate(inter, axis=0)
    return _rms(o, g_out) * _silu(r), state


def _gla_kernel(q_ref, k_ref, v_ref, r_ref, la_ref, go_ref, o_ref, s_ref, upd_ref):
    @pl.when(pl.program_id(2) == 0)
    def _():
        s_ref[...] = jnp.zeros_like(s_ref)

    for g in range(GLA_G):
        qk = slice(g * GLA_DK, (g + 1) * GLA_DK)
        vr = slice(g * GLA_DV, (g + 1) * GLA_DV)
        o, state = _gla_head(q_ref[:, qk], k_ref[:, qk], v_ref[:, vr], r_ref[:, vr], la_ref[:, qk],
                             go_ref[...], s_ref[g], upd_ref.at[g])
        s_ref[g] = state
        o_ref[:, vr] = o


def _gla(u, la, g_out, batch, seq):
    n = u.shape[0]
    nt = seq // GLA_TC
    tok = lambda b, h, t: b * nt + t
    qk_w, v_w = GLA_G * GLA_DK, GLA_G * GLA_DV
    k_blk0 = GLA_QK // qk_w
    v_blk0 = 2 * GLA_QK // v_w
    r_blk0 = v_blk0 + GLA_WIDTH // v_w
    return pl.pallas_call(
        _gla_kernel,
        grid=(batch, GLA_HEADS // GLA_G, nt),
        in_specs=[
            pl.BlockSpec((GLA_TC, qk_w), lambda b, h, t: (tok(b, h, t), h)),
            pl.BlockSpec((GLA_TC, qk_w), lambda b, h, t: (tok(b, h, t), k_blk0 + h)),
            pl.BlockSpec((GLA_TC, v_w), lambda b, h, t: (tok(b, h, t), v_blk0 + h)),
            pl.BlockSpec((GLA_TC, v_w), lambda b, h, t: (tok(b, h, t), r_blk0 + h)),
            pl.BlockSpec((GLA_TC, qk_w), lambda b, h, t: (tok(b, h, t), h)),
            pl.BlockSpec((1, GLA_DV), lambda b, h, t: (0, 0)),
        ],
        out_specs=pl.BlockSpec((GLA_TC, v_w), lambda b, h, t: (tok(b, h, t), h)),
        out_shape=jax.ShapeDtypeStruct((n, GLA_WIDTH), F32),
        scratch_shapes=[pltpu.VMEM((GLA_G, GLA_DK, GLA_DV), F32),
                        pltpu.VMEM((GLA_G, GLA_TC // GLA_CHUNK, GLA_DK, GLA_DV), F32)],
        compiler_params=_params("parallel", "parallel", "arbitrary"),
        name="gla",
    )(u, u, u, u, la, g_out.reshape(1, -1))


def _moba_kernel(qt_ref, kb_ref, v3_ref, kmean_ref, o_ref, kaug_ref, snext_ref):
    ti = pl.program_id(2)
    last_tile = pl.num_programs(2) - 1
    bs = MOBA_BLOCK
    tq = MOBA_TQ
    nb = kb_ref.shape[0] // bs
    dh = MOBA_DH
    vr = MOBA_VROWS
    scale = (dh ** -0.5) * LOG2_E

    @pl.when(ti == 0)
    def _():
        seq = kb_ref.shape[0]
        row_blk = lax.broadcasted_iota(jnp.int32, (seq, MXU_DIM - dh), 0) // bs
        lane = lax.broadcasted_iota(jnp.int32, (seq, MXU_DIM - dh), 1)
        onehot = (row_blk == lane).astype(BF16)
        for g in range(MOBA_G):
            kaug_ref[g, :, :dh] = kb_ref[:, g * dh:(g + 1) * dh]
            kaug_ref[g, :, dh:] = onehot

    blk = lax.broadcasted_iota(jnp.int32, (nb, tq), 0)
    q_blk = 2 * ti + (lax.broadcasted_iota(jnp.int32, (nb, tq), 1) // bs)
    past = blk < q_blk
    causal = (lax.broadcasted_iota(jnp.int32, (bs, bs), 0)
              <= lax.broadcasted_iota(jnp.int32, (bs, bs), 1))
    diag = pl.multiple_of(ti * tq, tq)

    def pv_dot(g, t, p):
        return (_dot(v3_ref[2 * t, g * vr:(g + 1) * vr, :], p[:bs])
                + _dot(v3_ref[2 * t + 1, g * vr:(g + 1) * vr, :], p[bs:]))

    q_aug, biases, state = [], [], []

    def score(g, t):
        off = pl.multiple_of(t * tq, tq)
        s = _dot(kaug_ref[g, pl.ds(off, tq), :], q_aug[g])
        return s, jnp.max(s, axis=0, keepdims=True)

    for g in range(MOBA_G):
        qt = qt_ref[g * dh:(g + 1) * dh, :]
        gate = jnp.dot(kmean_ref[:, g * dh:(g + 1) * dh], qt,
                       precision=lax.Precision.HIGHEST, preferred_element_type=F32)
        gate = jnp.where(past, gate, NEG_INF)
        rank = jnp.zeros((nb, tq), jnp.int32)
        for j in range(nb):
            gj = gate[j:j + 1, :]
            ahead = jnp.logical_or(gj > gate, jnp.logical_and(gj == gate, blk > j))
            rank = rank + ahead.astype(jnp.int32)
        chosen = jnp.logical_and(past, rank < MOBA_TOPK)
        bias = jnp.where(chosen, 0.0, NEG_INF)
        qa = jnp.concatenate([(qt * scale).astype(BF16), bias.astype(BF16),
                              jnp.zeros((MXU_DIM - dh - nb, tq), BF16)], axis=0)
        q_aug.append(qa)
        biases.append(bias)

    own_scores, first_maxes = [], []
    for g in range(MOBA_G):
        qa, bias = q_aug[g], biases[g]
        first_bias = jnp.sum(jnp.where(blk == 2 * ti, bias, 0.0), axis=0, keepdims=True)[:, bs:]
        s0 = _dot(kaug_ref[g, pl.ds(diag, bs), :dh], qa[:dh])
        s00 = jnp.where(causal, s0[:, :bs], NEG_INF)
        s01 = s0[:, bs:] + first_bias
        s11 = jnp.where(causal, _dot(kaug_ref[g, pl.ds(pl.multiple_of(diag + bs, bs), bs), :dh],
                                     qa[:dh, bs:]), NEG_INF)
        own_scores.append((s00, s01, s11))
    for g in range(MOBA_G):
        s_first, first_max = score(g, 0)
        snext_ref[g] = s_first
        first_maxes.append(first_max)
    for g in range(MOBA_G):
        s00, s01, s11 = own_scores[g]
        m0 = jnp.max(s00, axis=0, keepdims=True)
        m1 = jnp.maximum(jnp.max(s01, axis=0, keepdims=True), jnp.max(s11, axis=0, keepdims=True))
        p0 = jnp.concatenate([jnp.exp2(s00 - m0), jnp.exp2(s01 - m1)], axis=1).astype(BF16)
        p11 = jnp.exp2(s11 - m1).astype(BF16)
        acc0 = _dot(v3_ref[2 * ti, g * vr:(g + 1) * vr, :], p0)
        acc1 = _dot(v3_ref[2 * ti + 1, g * vr:(g + 1) * vr, :], p11)
        acc = jnp.concatenate([acc0[:, :bs], acc0[:, bs:] + acc1], axis=1)
        state.append((jnp.concatenate([m0, m1], axis=1), acc, first_maxes[g]))

    def consume(g, t, s, tile_max, m, acc, valid=None):
        if valid is not None:
            tile_max = jnp.where(valid, tile_max, NEG_INF)
        m_new = jnp.maximum(m, tile_max)
        alpha = jnp.exp2(m - m_new)
        shift = m_new if valid is None else jnp.where(valid, m_new, -NEG_INF)
        p = jnp.exp2(s - shift).astype(BF16)
        return m_new, alpha * acc + pv_dot(g, t, p)

    def body(u, carry):
        out = []
        for g in range(MOBA_G):
            m, acc, max_a = carry[g]
            s_b, max_b = score(g, 2 * u + 1)
            m, acc = consume(g, 2 * u, snext_ref[g], max_a, m, acc)
            s_c, max_c = score(g, jnp.minimum(2 * u + 2, last_tile))
            snext_ref[g] = s_c
            m, acc = consume(g, 2 * u + 1, s_b, max_b, m, acc, valid=2 * u + 1 < ti)
            out.append((m, acc, max_c))
        return tuple(out)

    carry = lax.fori_loop(0, (ti + 1) // 2, body, tuple(state))
    for g in range(MOBA_G):
        m, acc, _ = carry[g]
        o_ref[:, g * dh:(g + 1) * dh] = (acc[:dh] * (1.0 / acc[dh:dh + 1])).T


def _moba(q_t, kb, kmean, v3, batch, seq):
    n = kb.shape[0]
    nb = seq // MOBA_BLOCK
    nt = seq // MOBA_TQ
    gw = MOBA_G * MOBA_DH
    return pl.pallas_call(
        _moba_kernel,
        grid=(batch, MOBA_HEADS // MOBA_G, nt),
        in_specs=[
            pl.BlockSpec((gw, MOBA_TQ), lambda b, h, i: (h, b * nt + i)),
            pl.BlockSpec((seq, gw), lambda b, h, i: (b, h)),
            pl.BlockSpec((nb, MOBA_G * MOBA_VROWS, MOBA_BLOCK), lambda b, h, i: (b, h, 0)),
            pl.BlockSpec((nb, gw), lambda b, h, i: (b, h)),
        ],
        out_specs=pl.BlockSpec((MOBA_TQ, gw), lambda b, h, i: (b * nt + i, h)),
        out_shape=jax.ShapeDtypeStruct((n, MOBA_WIDTH), F32),
        scratch_shapes=[pltpu.VMEM((MOBA_G, seq, MXU_DIM), BF16),
                        pltpu.VMEM((MOBA_G, MOBA_TQ, MOBA_TQ), F32)],
        compiler_params=_params("parallel", "parallel", "arbitrary"),
        name="moba",
    )(q_t, kb, v3, kmean)


def _memkv_kernel(mem_ref, g_ref, wkv_ref, gk_ref, k_ref, v_ref):
    h = _rms(mem_ref[...], g_ref[...]).astype(BF16)
    kv = _dot(h, wkv_ref[...].astype(BF16))
    gk = gk_ref[...]
    ks = [_rms(kv[:, i * XATTN_DH:(i + 1) * XATTN_DH], gk) for i in range(XATTN_HEADS)]
    k_ref[...] = jnp.concatenate(ks, axis=1)
    v_ref[...] = kv[:, XATTN_WIDTH:]


def _memkv(mem, g, wkv_b, gk):
    m, d = mem.shape
    return pl.pallas_call(
        _memkv_kernel,
        out_shape=[jax.ShapeDtypeStruct((m, XATTN_WIDTH), F32),
                   jax.ShapeDtypeStruct((m, XATTN_WIDTH), F32)],
        compiler_params=pltpu.CompilerParams(vmem_limit_bytes=VMEM_LIMIT),
        name="memkv",
    )(mem, g.reshape(1, d), wkv_b, gk.reshape(1, -1))


def _outx_kernel(x1_ref, og_ref, om_ref, wout_ref, gx_ref, wq_ref, gq_ref, kx_ref, vx_ref, wo_ref,
                 o_ref):
    x2 = (x1_ref[...]
          + _dot(og_ref[...].astype(BF16), wout_ref[:GLA_WIDTH, :])
          + _dot(om_ref[...].astype(BF16), wout_ref[GLA_WIDTH:, :]))
    h = _rms(x2, gx_ref[...]).astype(BF16)
    q = _dot(h, wq_ref[...])
    gq = gq_ref[...]
    scale = (XATTN_DH ** -0.5) * LOG2_E
    cols = [slice(i * XATTN_DH, (i + 1) * XATTN_DH) for i in range(XATTN_HEADS)]
    qh = [(_rms(q[:, c], gq) * scale).astype(BF16) for c in cols]
    ss = [_dot_nt(qh[i], kx_ref[:, c].astype(BF16)) for i, c in enumerate(cols)]
    ps = []
    for s in ss:
        p = jnp.exp2(s - jnp.max(s, axis=-1, keepdims=True))
        ps.append((p * (1.0 / jnp.sum(p, axis=-1, keepdims=True))).astype(BF16))
    heads = [_dot(ps[i], vx_ref[:, c].astype(BF16)) for i, c in enumerate(cols)]
    o = jnp.concatenate(heads, axis=1).astype(BF16)
    o_ref[...] = x2 + _dot(o, wo_ref[...])


def _outx(x1, og, om, wout_b, gx, wq_b, gq, kx, vx, wo_b, seq, mem_len):
    n, d = x1.shape
    tiles_per_seq = seq // OUTX_TM
    const = lambda i: (0, 0)
    return pl.pallas_call(
        _outx_kernel,
        grid=(n // OUTX_TM,),
        in_specs=[
            pl.BlockSpec((OUTX_TM, d), lambda i: (i, 0)),
            pl.BlockSpec((OUTX_TM, GLA_WIDTH), lambda i: (i, 0)),
            pl.BlockSpec((OUTX_TM, MOBA_WIDTH), lambda i: (i, 0)),
            pl.BlockSpec((GLA_WIDTH + MOBA_WIDTH, d), const),
            pl.BlockSpec((1, d), const),
            pl.BlockSpec((d, XATTN_WIDTH), const),
            pl.BlockSpec((1, XATTN_DH), const),
            pl.BlockSpec((mem_len, XATTN_WIDTH), lambda i: (i // tiles_per_seq, 0)),
            pl.BlockSpec((mem_len, XATTN_WIDTH), lambda i: (i // tiles_per_seq, 0)),
            pl.BlockSpec((XATTN_WIDTH, d), const),
        ],
        out_specs=pl.BlockSpec((OUTX_TM, d), lambda i: (i, 0)),
        out_shape=jax.ShapeDtypeStruct((n, d), F32),
        compiler_params=_params("parallel"),
        name="outx",
    )(x1, og, om, wout_b, gx.reshape(1, d), wq_b, gq.reshape(1, -1), kx, vx, wo_b)


def kernel(x, mem, ffn1_norm, ffn1_w_gate, ffn1_w_up, ffn1_w_down, mix_norm, w_in, gla_w_gate2, gla_b_gate2, gla_out_norm, moba_q_norm, moba_k_norm, w_out, xattn_norm, mem_norm, xattn_w_q, xattn_w_kv, xattn_w_o, xattn_q_norm, xattn_k_norm, ffn2_norm, ffn2_w_gate, ffn2_w_up, ffn2_w_down):
    batch, seq, d = x.shape
    mem_len = mem.shape[1]
    depth = ffn1_norm.shape[0]
    n = batch * seq
    assert seq % GLA_TC == 0 and seq % (2 * MOBA_BLOCK) == 0 and seq % OUTX_TM == 0
    assert seq // MOBA_BLOCK + MOBA_DH <= MXU_DIM and (seq // MOBA_BLOCK) % BF16_SUBLANES == 0
    assert n % FFN_TM == 0 and n % NORM_TM == 0 and n % PROJ_TM == 0 and PROJ_TM % MOBA_BLOCK == 0

    xf = x.reshape(n, d)
    memf = mem.reshape(batch * mem_len, d)
    lr0 = GLA_COLS
    mq0 = lr0 + GLA_GATE_RANK
    w_in_t = jnp.swapaxes(w_in, 1, 2)
    for l in range(depth):
        w_lr_t = w_in_t[l, lr0:mq0]

        x1 = _ffn(xf, ffn1_norm[l], ffn1_w_gate[l], ffn1_w_up[l], ffn1_w_down[l])
        hn, la = _mixnorm(x1, mix_norm[l], w_lr_t, gla_w_gate2[l], gla_b_gate2[l])
        u, q_t, kb, kmean, v3 = _proj(hn, w_in_t, l, moba_q_norm[l], moba_k_norm[l])
        o_gla = _gla(u, la, gla_out_norm[l], batch, seq)
        o_moba = _moba(q_t, kb, kmean, v3, batch, seq)
        kx, vx = _memkv(memf, mem_norm[l], xattn_w_kv[l], xattn_k_norm[l])
        x3 = _outx(x1, o_gla, o_moba, w_out[l].astype(BF16), xattn_norm[l],
                   xattn_w_q[l].astype(BF16), xattn_q_norm[l], kx, vx,
                   xattn_w_o[l].astype(BF16), seq, mem_len)
        xf = _ffn(x3, ffn2_norm[l], ffn2_w_gate[l], ffn2_w_up[l], ffn2_w_down[l])
    return xf.reshape(batch, seq, d)
```

```python
import jax
import jax.numpy as jnp
from jax import lax
from jax.experimental import pallas as pl
from jax.experimental.pallas import tpu as pltpu

F32 = jnp.float32
BF16 = jnp.bfloat16

EPS = 1e-6
NEG_INF = -1e30
LOG2_E = 1.4426950408889634
D_FF = 5632
GLA_HEADS = 4
GLA_DV = 256
GLA_DK = 128
GLA_QK = GLA_HEADS * GLA_DK
GLA_WIDTH = GLA_HEADS * GLA_DV
GLA_COLS = 2 * GLA_QK + 2 * GLA_WIDTH
GLA_GATE_RANK = 16
GLA_GATE_TAU = 16.0
GLA_CHUNK = 64
MOBA_DH = 128
MOBA_HEADS = 8
MOBA_WIDTH = MOBA_HEADS * MOBA_DH
MOBA_BLOCK = 256
MOBA_TOPK = 3
XATTN_HEADS = 4
XATTN_DH = 128
XATTN_WIDTH = XATTN_HEADS * XATTN_DH

V7X_VMEM_BYTES = 64 * 1024 * 1024
VMEM_LIMIT = V7X_VMEM_BYTES - 6 * 1024 * 1024
FFN_VMEM_LIMIT = V7X_VMEM_BYTES - 2 * 1024 * 1024
MXU_DIM = 256

FFN_TM = 1024
FFN_TF = 512
NORM_TM = 1024
PROJ_TM = 2048
PROJ_TN = 512
GLA_TC = 512
GLA_SUB = 256
GLA_G = 4
MOBA_G = 4
MOBA_TQ = 2 * MOBA_BLOCK
BF16_SUBLANES = 16
MOBA_VPAD = BF16_SUBLANES
MOBA_VROWS = MOBA_DH + MOBA_VPAD
OUTX_TM = 512


def _params(*semantics):
    return pltpu.CompilerParams(dimension_semantics=semantics, vmem_limit_bytes=VMEM_LIMIT)


def _rms(x, g):
    return x * lax.rsqrt(jnp.mean(x * x, axis=-1, keepdims=True) + EPS) * g


def _dot(a, b):
    return jnp.dot(a, b, preferred_element_type=F32)


def _dot_nt(a, b):
    return lax.dot_general(a, b, (((1,), (1,)), ((), ())), preferred_element_type=F32)


def _silu(x):
    return x * (1.0 / (1.0 + jnp.exp(-x)))


def _ffn_kernel(x_hbm, g_ref, wg_ref, wu_ref, wd_ref, o_hbm, h_ref, xbuf_ref, acc_ref, xsem, osem):
    i = pl.program_id(0)
    f = pl.program_id(1)
    last_i = pl.num_programs(0) - 1
    last_f = pl.num_programs(1) - 1

    def x_copy(tile):
        return pltpu.make_async_copy(x_hbm.at[pl.ds(tile * FFN_TM, FFN_TM)], xbuf_ref, xsem)

    def o_copy(tile):
        return pltpu.make_async_copy(acc_ref, o_hbm.at[pl.ds(tile * FFN_TM, FFN_TM)], osem)

    @pl.when(jnp.logical_and(i == 0, f == 0))
    def _():
        x_copy(0).start()

    @pl.when(f == 0)
    def _():
        x_copy(i).wait()
        x = xbuf_ref[...]
        h_ref[...] = _rms(x, g_ref[...]).astype(BF16)

        @pl.when(i > 0)
        def _():
            o_copy(i - 1).wait()

        acc_ref[...] = xbuf_ref[...]

    @pl.when(jnp.logical_and(f == 1, i < last_i))
    def _():
        x_copy(i + 1).start()

    h = h_ref[...]
    gate = _dot(h, wg_ref[...].astype(BF16))
    up = _dot(h, wu_ref[...].astype(BF16))
    act = (0.5 * _silu(gate) * up).astype(BF16)
    acc_ref[...] += _dot(act, wd_ref[...].astype(BF16))

    @pl.when(f == last_f)
    def _():
        o_copy(i).start()

        @pl.when(i == last_i)
        def _():
            o_copy(i).wait()


def _ffn(x, g, wg, wu, wd):
    n, d = x.shape
    grid = (n // FFN_TM, D_FF // FFN_TF)
    assert grid[1] >= 2
    return pl.pallas_call(
        _ffn_kernel,
        grid=grid,
        in_specs=[
            pl.BlockSpec(memory_space=pl.ANY),
            pl.BlockSpec((1, d), lambda i, f: (0, 0)),
            pl.BlockSpec((d, FFN_TF), lambda i, f: (0, f)),
            pl.BlockSpec((d, FFN_TF), lambda i, f: (0, f)),
            pl.BlockSpec((FFN_TF, d), lambda i, f: (f, 0)),
        ],
        out_specs=pl.BlockSpec(memory_space=pl.ANY),
        out_shape=jax.ShapeDtypeStruct((n, d), F32),
        scratch_shapes=[pltpu.VMEM((FFN_TM, d), BF16), pltpu.VMEM((FFN_TM, d), F32),
                        pltpu.VMEM((FFN_TM, d), F32),
                        pltpu.SemaphoreType.DMA, pltpu.SemaphoreType.DMA],
        compiler_params=pltpu.CompilerParams(dimension_semantics=("arbitrary", "arbitrary"),
                                             vmem_limit_bytes=FFN_VMEM_LIMIT),
        name="ffn",
    )(x, g.reshape(1, d), wg, wu, wd)


def _mixnorm_kernel(x_ref, g_ref, wlr_ref, wg2_ref, bg2_ref, hn_ref, la_ref):
    h = _rms(x_ref[...], g_ref[...]).astype(BF16)
    hn_ref[...] = h
    lr = _dot_nt(h, wlr_ref[...].astype(BF16))
    pre = _dot(lr.astype(BF16), wg2_ref[...].astype(BF16)) + bg2_ref[...]
    log_sig = -(jnp.maximum(-pre, 0.0) + jnp.log(1.0 + jnp.exp(-jnp.abs(pre))))
    la_ref[...] = log_sig * (1.0 / GLA_GATE_TAU)


def _mixnorm(x1, g, w_lr_t, wg2, bg2):
    n, d = x1.shape
    const = lambda i: (0, 0)
    return pl.pallas_call(
        _mixnorm_kernel,
        grid=(n // NORM_TM,),
        in_specs=[
            pl.BlockSpec((NORM_TM, d), lambda i: (i, 0)),
            pl.BlockSpec((1, d), const),
            pl.BlockSpec((GLA_GATE_RANK, d), const),
            pl.BlockSpec((GLA_GATE_RANK, GLA_QK), const),
            pl.BlockSpec((1, GLA_QK), const),
        ],
        out_specs=[pl.BlockSpec((NORM_TM, d), lambda i: (i, 0)),
                   pl.BlockSpec((NORM_TM, GLA_QK), lambda i: (i, 0))],
        out_shape=[jax.ShapeDtypeStruct((n, d), BF16),
                   jax.ShapeDtypeStruct((n, GLA_QK), F32)],
        compiler_params=_params("parallel"),
        name="mixnorm",
    )(x1, g.reshape(1, d), w_lr_t, wg2, bg2.reshape(1, -1))


def _projg_kernel(hn_ref, wt_ref, u_ref):
    u_ref[...] = _dot_nt(hn_ref[...], wt_ref[0].astype(BF16))


def _projq_kernel(hn_ref, wt_ref, g_ref, qt_ref):
    yt = _dot_nt(wt_ref[0].astype(BF16), hn_ref[...])
    g = g_ref[...]
    parts = []
    for k in range(PROJ_TN // MOBA_DH):
        p = yt[k * MOBA_DH:(k + 1) * MOBA_DH, :]
        inv = lax.rsqrt(jnp.mean(p * p, axis=0, keepdims=True) + EPS)
        parts.append(p * inv * g)
    qt_ref[...] = jnp.concatenate(parts, axis=0)


def _projk_kernel(hn_ref, wt_ref, g_ref, kb_ref, kmean_ref):
    y = _dot_nt(hn_ref[...], wt_ref[0].astype(BF16))
    g = g_ref[...]
    parts = [_rms(y[:, k * MOBA_DH:(k + 1) * MOBA_DH], g) for k in range(PROJ_TN // MOBA_DH)]
    kn = jnp.concatenate(parts, axis=1)
    kb_ref[...] = kn.astype(BF16)
    blocks = PROJ_TM // MOBA_BLOCK
    kmean_ref[0] = jnp.sum(kn.reshape(blocks, MOBA_BLOCK, PROJ_TN), axis=1) * (1.0 / MOBA_BLOCK)


def _projv_kernel(hn_ref, wt_ref, v3_ref):
    yt = _dot_nt(wt_ref[0].astype(BF16), hn_ref[...]).astype(BF16)
    ones = jnp.ones((MOBA_VPAD, PROJ_TM), BF16)
    parts = []
    for k in range(PROJ_TN // MOBA_DH):
        parts += [yt[k * MOBA_DH:(k + 1) * MOBA_DH], ones]
    ya = jnp.concatenate(parts, axis=0)
    for t in range(PROJ_TM // MOBA_BLOCK):
        v3_ref[t] = ya[:, t * MOBA_BLOCK:(t + 1) * MOBA_BLOCK]


def _proj_kernel(hn_ref, wt_ref, gq_ref, gk_ref, u_ref, qt_ref, kb_ref, kmean_ref, v3_ref):
    j = pl.program_id(1)
    gla = GLA_COLS // PROJ_TN
    steps = MOBA_WIDTH // PROJ_TN

    @pl.when(j < gla)
    def _():
        _projg_kernel(hn_ref, wt_ref, u_ref)

    @pl.when(jnp.logical_and(j >= gla, j < gla + steps))
    def _():
        _projq_kernel(hn_ref, wt_ref, gq_ref, qt_ref)

    @pl.when(jnp.logical_and(j >= gla + steps, j < gla + 2 * steps))
    def _():
        _projk_kernel(hn_ref, wt_ref, gk_ref, kb_ref, kmean_ref)

    @pl.when(j >= gla + 2 * steps)
    def _():
        _projv_kernel(hn_ref, wt_ref, v3_ref)


def _proj(hn, w_in_t, layer, gq, gk):
    n, d = hn.shape
    gla = GLA_COLS // PROJ_TN
    steps = MOBA_WIDTH // PROJ_TN
    blocks = PROJ_TM // MOBA_BLOCK
    moba_gap = GLA_GATE_RANK
    own = lambda j, first: jnp.clip(j - gla - first * steps, 0, steps - 1)
    const = lambda i, j: (0, 0)
    u, q_t, kb, kmean, v3 = pl.pallas_call(
        _proj_kernel,
        grid=(n // PROJ_TM, gla + 3 * steps),
        in_specs=[
            pl.BlockSpec((PROJ_TM, d), lambda i, j: (i, 0)),
            pl.BlockSpec((pl.Element(1), pl.Element(PROJ_TN), pl.Element(d)),
                         lambda i, j: (layer,
                                       pl.multiple_of(j * PROJ_TN + jnp.where(j >= gla, moba_gap, 0), 8), 0)),
            pl.BlockSpec((MOBA_DH, 1), const),
            pl.BlockSpec((1, MOBA_DH), const),
        ],
        out_specs=[
            pl.BlockSpec((PROJ_TM, PROJ_TN), lambda i, j: (i, jnp.minimum(j, gla - 1))),
            pl.BlockSpec((PROJ_TN, PROJ_TM), lambda i, j: (own(j, 0), i)),
            pl.BlockSpec((PROJ_TM, PROJ_TN), lambda i, j: (i, own(j, 1))),
            pl.BlockSpec((1, blocks, PROJ_TN), lambda i, j: (i, 0, own(j, 1))),
            pl.BlockSpec((blocks, PROJ_TN // MOBA_DH * MOBA_VROWS, MOBA_BLOCK),
                         lambda i, j: (i, own(j, 2), 0)),
        ],
        out_shape=[jax.ShapeDtypeStruct((n, GLA_COLS), F32),
                   jax.ShapeDtypeStruct((MOBA_WIDTH, n), F32),
                   jax.ShapeDtypeStruct((n, MOBA_WIDTH), BF16),
                   jax.ShapeDtypeStruct((n // PROJ_TM, blocks, MOBA_WIDTH), F32),
                   jax.ShapeDtypeStruct((n // MOBA_BLOCK, MOBA_HEADS * MOBA_VROWS, MOBA_BLOCK), BF16)],
        compiler_params=_params("arbitrary", "arbitrary"),
        name="proj",
    )(hn, w_in_t, gq.reshape(-1, 1), gk.reshape(1, -1))
    return u, q_t, kb, kmean.reshape(n // MOBA_BLOCK, MOBA_WIDTH), v3


def _dot_exact_rhs(m, a):
    a1 = a.astype(BF16)
    r1 = a - a1.astype(F32)
    a2 = r1.astype(BF16)
    a3 = (r1 - a2.astype(F32)).astype(BF16)
    return _dot(m, a1) + _dot(m, a2) + _dot(m, a3)


def _gla_head(q, k, v, r, la, g_out, state, upd_ref):
    tc = GLA_TC
    nchunk = tc // GLA_CHUNK
    sub = GLA_SUB

    crow = lax.broadcasted_iota(jnp.int32, (GLA_CHUNK, GLA_CHUNK), 0)
    ccol = lax.broadcasted_iota(jnp.int32, (GLA_CHUNK, GLA_CHUNK), 1)
    tri = (ccol <= crow).astype(BF16)
    la_wide = jnp.concatenate([la[c * GLA_CHUNK:(c + 1) * GLA_CHUNK] for c in range(nchunk)], axis=1)
    bc_wide = _dot_exact_rhs(tri, la_wide)
    bcum = jnp.concatenate([bc_wide[:, c * GLA_DK:(c + 1) * GLA_DK] for c in range(nchunk)], axis=0)
    btot = jnp.concatenate(
        [jnp.broadcast_to(bc_wide[GLA_CHUNK - 1:GLA_CHUNK, c * GLA_DK:(c + 1) * GLA_DK],
                          (GLA_CHUNK, GLA_DK)) for c in range(nchunk)], axis=0)

    q_dec_b = ((q * (GLA_DK ** -0.5)) * jnp.exp(bcum)).astype(BF16)
    k_inv_b = (k * jnp.exp(-bcum)).astype(BF16)
    k_tail_t = (k * jnp.exp(btot - bcum)).T.astype(BF16)
    decay_t = jnp.exp(btot).T
    vb = v.astype(BF16)

    row = lax.broadcasted_iota(jnp.int32, (sub, sub), 0)
    col = lax.broadcasted_iota(jnp.int32, (sub, sub), 1)
    causal = jnp.logical_and((row // GLA_CHUNK) == (col // GLA_CHUNK), col <= row)
    intra = []
    for p in range(tc // sub):
        rows = slice(p * sub, (p + 1) * sub)
        att = jnp.where(causal, _dot_nt(q_dec_b[rows], k_inv_b[rows]), 0.0)
        intra.append(_dot(att.astype(BF16), vb[rows]))

    for c in range(nchunk):
        rows = slice(c * GLA_CHUNK, (c + 1) * GLA_CHUNK)
        upd_ref[c] = _dot(k_tail_t[:, rows], vb[rows])

    inter = []
    for c in range(nchunk):
        rows = slice(c * GLA_CHUNK, (c + 1) * GLA_CHUNK)
        inter.append(_dot(q_dec_b[rows], state.astype(BF16)))
        dec_c = decay_t[:, c * GLA_CHUNK:c * GLA_CHUNK + 1]
        state = dec_c * state + upd_ref[c]
    o = jnp.concatenate(intra, axis=0) + jnp.concatenate(inter, axis=0)
    return _rms(o, g_out) * _silu(r), state


def _gla_kernel(q_ref, k_ref, v_ref, r_ref, la_ref, go_ref, o_ref, s_ref, upd_ref):
    @pl.when(pl.program_id(2) == 0)
    def _():
        s_ref[...] = jnp.zeros_like(s_ref)

    for g in range(GLA_G):
        qk = slice(g * GLA_DK, (g + 1) * GLA_DK)
        vr = slice(g * GLA_DV, (g + 1) * GLA_DV)
        o, state = _gla_head(q_ref[:, qk], k_ref[:, qk], v_ref[:, vr], r_ref[:, vr], la_ref[:, qk],
                             go_ref[...], s_ref[g], upd_ref.at[g])
        s_ref[g] = state
        o_ref[:, vr] = o


def _gla(u, la, g_out, batch, seq):
    n = u.shape[0]
    nt = seq // GLA_TC
    tok = lambda b, h, t: b * nt + t
    qk_w, v_w = GLA_G * GLA_DK, GLA_G * GLA_DV
    k_blk0 = GLA_QK // qk_w
    v_blk0 = 2 * GLA_QK // v_w
    r_blk0 = v_blk0 + GLA_WIDTH // v_w
    return pl.pallas_call(
        _gla_kernel,
        grid=(batch, GLA_HEADS // GLA_G, nt),
        in_specs=[
            pl.BlockSpec((GLA_TC, qk_w), lambda b, h, t: (tok(b, h, t), h)),
            pl.BlockSpec((GLA_TC, qk_w), lambda b, h, t: (tok(b, h, t), k_blk0 + h)),
            pl.BlockSpec((GLA_TC, v_w), lambda b, h, t: (tok(b, h, t), v_blk0 + h)),
            pl.BlockSpec((GLA_TC, v_w), lambda b, h, t: (tok(b, h, t), r_blk0 + h)),
            pl.BlockSpec((GLA_TC, qk_w), lambda b, h, t: (tok(b, h, t), h)),
            pl.BlockSpec((1, GLA_DV), lambda b, h, t: (0, 0)),
        ],
        out_specs=pl.BlockSpec((GLA_TC, v_w), lambda b, h, t: (tok(b, h, t), h)),
        out_shape=jax.ShapeDtypeStruct((n, GLA_WIDTH), F32),
        scratch_shapes=[pltpu.VMEM((GLA_G, GLA_DK, GLA_DV), F32),
                        pltpu.VMEM((GLA_G, GLA_TC // GLA_CHUNK, GLA_DK, GLA_DV), F32)],
        compiler_params=_params("parallel", "parallel", "arbitrary"),
        name="gla",
    )(u, u, u, u, la, g_out.reshape(1, -1))


def _moba_kernel(qt_ref, kb_ref, v3_ref, kmean_ref, o_ref, kaug_ref, snext_ref, mask_ref):
    ti = pl.program_id(2)
    last_tile = pl.num_programs(2) - 1
    bs = MOBA_BLOCK
    tq = MOBA_TQ
    nb = kb_ref.shape[0] // bs
    dh = MOBA_DH
    vr = MOBA_VROWS
    scale = (dh ** -0.5) * LOG2_E

    @pl.when(ti == 0)
    def _():
        seq = kb_ref.shape[0]
        row_blk = lax.broadcasted_iota(jnp.int32, (seq, MXU_DIM - dh), 0) // bs
        lane = lax.broadcasted_iota(jnp.int32, (seq, MXU_DIM - dh), 1)
        onehot = (row_blk == lane).astype(BF16)
        for g in range(MOBA_G):
            kaug_ref[g, :, :dh] = kb_ref[:, g * dh:(g + 1) * dh]
            kaug_ref[g, :, dh:] = onehot
        key_row = lax.broadcasted_iota(jnp.int32, (tq, tq), 0)
        qry_col = lax.broadcasted_iota(jnp.int32, (tq, tq), 1)
        mask_ref[0] = jnp.zeros((tq, tq), F32)
        mask_ref[1] = jnp.where(key_row <= qry_col, 0.0, NEG_INF)

    blk = lax.broadcasted_iota(jnp.int32, (nb, tq), 0)
    q_blk = 2 * ti + (lax.broadcasted_iota(jnp.int32, (nb, tq), 1) // bs)
    past = blk < q_blk

    def pv_dot(g, t, p):
        return (_dot(v3_ref[2 * t, g * vr:(g + 1) * vr, :], p[:bs])
                + _dot(v3_ref[2 * t + 1, g * vr:(g + 1) * vr, :], p[bs:]))

    q_feat, q_bias = [], []
    for g in range(MOBA_G):
        qt = qt_ref[g * dh:(g + 1) * dh, :]
        gate = jnp.dot(kmean_ref[:, g * dh:(g + 1) * dh], qt,
                       precision=lax.Precision.HIGHEST, preferred_element_type=F32)
        gate = jnp.where(past, gate, NEG_INF)
        rank = jnp.zeros((nb, tq), jnp.int32)
        for j in range(nb):
            gj = gate[j:j + 1, :]
            ahead = jnp.logical_or(gj > gate, jnp.logical_and(gj == gate, blk > j))
            rank = rank + ahead.astype(jnp.int32)
        chosen = jnp.logical_and(past, rank < MOBA_TOPK)
        bias = jnp.where(chosen, 0.0, NEG_INF)
        bias_own = jnp.where(blk == q_blk, 0.0, bias)
        q_feat.append((qt * scale).astype(BF16))
        q_bias.append((bias.astype(BF16), bias_own.astype(BF16)))

    pad = jnp.zeros((MXU_DIM - dh - nb, tq), BF16)

    def score(g, t):
        own = t == ti
        off = pl.multiple_of(t * tq, tq)
        bias = jnp.where(own, q_bias[g][1], q_bias[g][0])
        rhs = jnp.concatenate([q_feat[g], bias, pad], axis=0)
        s = _dot(kaug_ref[g, pl.ds(off, tq), :], rhs) + mask_ref[own.astype(jnp.int32)]
        return s, jnp.max(s, axis=0, keepdims=True)

    def consume(g, t, s, tile_max, m, acc, valid=None):
        if valid is not None:
            tile_max = jnp.where(valid, tile_max, NEG_INF)
        m_new = jnp.maximum(m, tile_max)
        alpha = jnp.exp2(m - m_new)
        shift = m_new if valid is None else jnp.where(valid, m_new, -NEG_INF)
        p = jnp.exp2(s - shift).astype(BF16)
        return m_new, alpha * acc + pv_dot(g, t, p)

    state = []
    for g in range(MOBA_G):
        s_first, first_max = score(g, 0)
        snext_ref[g] = s_first
        state.append((jnp.full((1, tq), 0.5 * NEG_INF, F32), jnp.zeros((vr, tq), F32), first_max))

    def body(u, carry):
        out = []
        for g in range(MOBA_G):
            m, acc, max_a = carry[g]
            s_b, max_b = score(g, jnp.minimum(2 * u + 1, last_tile))
            m, acc = consume(g, 2 * u, snext_ref[g], max_a, m, acc)
            s_c, max_c = score(g, jnp.minimum(2 * u + 2, last_tile))
            snext_ref[g] = s_c
            m, acc = consume(g, jnp.minimum(2 * u + 1, last_tile), s_b, max_b, m, acc, valid=2 * u + 1 <= ti)
            out.append((m, acc, max_c))
        return tuple(out)

    carry = lax.fori_loop(0, (ti + 2) // 2, body, tuple(state))
    for g in range(MOBA_G):
        m, acc, _ = carry[g]
        o_ref[:, g * dh:(g + 1) * dh] = (acc[:dh] * (1.0 / acc[dh:dh + 1])).T


def _moba(q_t, kb, kmean, v3, batch, seq):
    n = kb.shape[0]
    nb = seq // MOBA_BLOCK
    nt = seq // MOBA_TQ
    gw = MOBA_G * MOBA_DH
    return pl.pallas_call(
        _moba_kernel,
        grid=(batch, MOBA_HEADS // MOBA_G, nt),
        in_specs=[
            pl.BlockSpec((gw, MOBA_TQ), lambda b, h, i: (h, b * nt + i)),
            pl.BlockSpec((seq, gw), lambda b, h, i: (b, h)),
            pl.BlockSpec((nb, MOBA_G * MOBA_VROWS, MOBA_BLOCK), lambda b, h, i: (b, h, 0)),
            pl.BlockSpec((nb, gw), lambda b, h, i: (b, h)),
        ],
        out_specs=pl.BlockSpec((MOBA_TQ, gw), lambda b, h, i: (b * nt + i, h)),
        out_shape=jax.ShapeDtypeStruct((n, MOBA_WIDTH), F32),
        scratch_shapes=[pltpu.VMEM((MOBA_G, seq, MXU_DIM), BF16),
                        pltpu.VMEM((MOBA_G, MOBA_TQ, MOBA_TQ), F32),
                        pltpu.VMEM((2, MOBA_TQ, MOBA_TQ), F32)],
        compiler_params=_params("parallel", "parallel", "arbitrary"),
        name="moba",
    )(q_t, kb, v3, kmean)


def _memkv_kernel(mem_ref, g_ref, wkv_ref, gk_ref, k_ref, v_ref):
    h = _rms(mem_ref[...], g_ref[...]).astype(BF16)
    kv = _dot(h, wkv_ref[...].astype(BF16))
    gk = gk_ref[...]
    ks = [_rms(kv[:, i * XATTN_DH:(i + 1) * XATTN_DH], gk) for i in range(XATTN_HEADS)]
    k_ref[...] = jnp.concatenate(ks, axis=1)
    v_ref[...] = kv[:, XATTN_WIDTH:]


def _memkv(mem, g, wkv_b, gk):
    m, d = mem.shape
    return pl.pallas_call(
        _memkv_kernel,
        out_shape=[jax.ShapeDtypeStruct((m, XATTN_WIDTH), F32),
                   jax.ShapeDtypeStruct((m, XATTN_WIDTH), F32)],
        compiler_params=pltpu.CompilerParams(vmem_limit_bytes=VMEM_LIMIT),
        name="memkv",
    )(mem, g.reshape(1, d), wkv_b, gk.reshape(1, -1))


def _outx_kernel(x1_ref, og_ref, om_ref, wout_ref, gx_ref, wq_ref, gq_ref, kx_ref, vx_ref, wo_ref,
                 o_ref):
    x2 = (x1_ref[...]
          + _dot(og_ref[...].astype(BF16), wout_ref[:GLA_WIDTH, :])
          + _dot(om_ref[...].astype(BF16), wout_ref[GLA_WIDTH:, :]))
    h = _rms(x2, gx_ref[...]).astype(BF16)
    q = _dot(h, wq_ref[...])
    gq = gq_ref[...]
    scale = (XATTN_DH ** -0.5) * LOG2_E
    cols = [slice(i * XATTN_DH, (i + 1) * XATTN_DH) for i in range(XATTN_HEADS)]
    qh = [(_rms(q[:, c], gq) * scale).astype(BF16) for c in cols]
    ss = [_dot_nt(qh[i], kx_ref[:, c].astype(BF16)) for i, c in enumerate(cols)]
    ps = []
    for s in ss:
        p = jnp.exp2(s - jnp.max(s, axis=-1, keepdims=True))
        ps.append((p * (1.0 / jnp.sum(p, axis=-1, keepdims=True))).astype(BF16))
    heads = [_dot(ps[i], vx_ref[:, c].astype(BF16)) for i, c in enumerate(cols)]
    o = jnp.concatenate(heads, axis=1).astype(BF16)
    o_ref[...] = x2 + _dot(o, wo_ref[...])


def _outx(x1, og, om, wout_b, gx, wq_b, gq, kx, vx, wo_b, seq, mem_len):
    n, d = x1.shape
    tiles_per_seq = seq // OUTX_TM
    const = lambda i: (0, 0)
    return pl.pallas_call(
        _outx_kernel,
        grid=(n // OUTX_TM,),
        in_specs=[
            pl.BlockSpec((OUTX_TM, d), lambda i: (i, 0)),
            pl.BlockSpec((OUTX_TM, GLA_WIDTH), lambda i: (i, 0)),
            pl.BlockSpec((OUTX_TM, MOBA_WIDTH), lambda i: (i, 0)),
            pl.BlockSpec((GLA_WIDTH + MOBA_WIDTH, d), const),
            pl.BlockSpec((1, d), const),
            pl.BlockSpec((d, XATTN_WIDTH), const),
            pl.BlockSpec((1, XATTN_DH), const),
            pl.BlockSpec((mem_len, XATTN_WIDTH), lambda i: (i // tiles_per_seq, 0)),
            pl.BlockSpec((mem_len, XATTN_WIDTH), lambda i: (i // tiles_per_seq, 0)),
            pl.BlockSpec((XATTN_WIDTH, d), const),
        ],
        out_specs=pl.BlockSpec((OUTX_TM, d), lambda i: (i, 0)),
        out_shape=jax.ShapeDtypeStruct((n, d), F32),
        compiler_params=_params("parallel"),
        name="outx",
    )(x1, og, om, wout_b, gx.reshape(1, d), wq_b, gq.reshape(1, -1), kx, vx, wo_b)


def kernel(x, mem, ffn1_norm, ffn1_w_gate, ffn1_w_up, ffn1_w_down, mix_norm, w_in, gla_w_gate2, gla_b_gate2, gla_out_norm, moba_q_norm, moba_k_norm, w_out, xattn_norm, mem_norm, xattn_w_q, xattn_w_kv, xattn_w_o, xattn_q_norm, xattn_k_norm, ffn2_norm, ffn2_w_gate, ffn2_w_up, ffn2_w_down):
    batch, seq, d = x.shape
    mem_len = mem.shape[1]
    depth = ffn1_norm.shape[0]
    n = batch * seq
    assert seq % GLA_TC == 0 and seq % (2 * MOBA_BLOCK) == 0 and seq % OUTX_TM == 0
    assert seq // MOBA_BLOCK + MOBA_DH <= MXU_DIM and (seq // MOBA_BLOCK) % BF16_SUBLANES == 0
    assert n % FFN_TM == 0 and n % NORM_TM == 0 and n % PROJ_TM == 0 and PROJ_TM % MOBA_BLOCK == 0

    xf = x.reshape(n, d)
    memf = mem.reshape(batch * mem_len, d)
    lr0 = GLA_COLS
    mq0 = lr0 + GLA_GATE_RANK
    w_in_t = jnp.swapaxes(w_in, 1, 2)
    for l in range(depth):
        w_lr_t = w_in_t[l, lr0:mq0]

        x1 = _ffn(xf, ffn1_norm[l], ffn1_w_gate[l], ffn1_w_up[l], ffn1_w_down[l])
        hn, la = _mixnorm(x1, mix_norm[l], w_lr_t, gla_w_gate2[l], gla_b_gate2[l])
        u, q_t, kb, kmean, v3 = _proj(hn, w_in_t, l, moba_q_norm[l], moba_k_norm[l])
        o_gla = _gla(u, la, gla_out_norm[l], batch, seq)
        o_moba = _moba(q_t, kb, kmean, v3, batch, seq)
        kx, vx = _memkv(memf, mem_norm[l], xattn_w_kv[l], xattn_k_norm[l])
        x3 = _outx(x1, o_gla, o_moba, w_out[l].astype(BF16), xattn_norm[l],
                   xattn_w_q[l].astype(BF16), xattn_q_norm[l], kx, vx,
                   xattn_w_o[l].astype(BF16), seq, mem_len)
        xf = _ffn(x3, ffn2_norm[l], ffn2_w_gate[l], ffn2_w_up[l], ffn2_w_down[l])
    return xf.reshape(batch, seq, d)
```

```python
import jax
import jax.numpy as jnp
from jax import lax
from jax.experimental import pallas as pl
from jax.experimental.pallas import tpu as pltpu

F32 = jnp.float32
BF16 = jnp.bfloat16

EPS = 1e-6
NEG_INF = -1e30
LOG2_E = 1.4426950408889634
D_FF = 5632
GLA_HEADS = 4
GLA_DV = 256
GLA_DK = 128
GLA_QK = GLA_HEADS * GLA_DK
GLA_WIDTH = GLA_HEADS * GLA_DV
GLA_COLS = 2 * GLA_QK + 2 * GLA_WIDTH
GLA_GATE_RANK = 16
GLA_GATE_TAU = 16.0
GLA_CHUNK = 64
MOBA_DH = 128
MOBA_HEADS = 8
MOBA_WIDTH = MOBA_HEADS * MOBA_DH
MOBA_BLOCK = 256
MOBA_TOPK = 3
XATTN_HEADS = 4
XATTN_DH = 128
XATTN_WIDTH = XATTN_HEADS * XATTN_DH

V7X_VMEM_BYTES = 64 * 1024 * 1024
VMEM_LIMIT = V7X_VMEM_BYTES - 6 * 1024 * 1024
FFN_VMEM_LIMIT = V7X_VMEM_BYTES - 2 * 1024 * 1024
MXU_DIM = 256

FFN_TM = 1024
FFN_TF = 512
NORM_TM = 1024
PROJ_TM = 2048
PROJ_TN = 512
GLA_TC = 512
GLA_SUB = 256
GLA_G = 4
MOBA_G = 4
MOBA_TQ = 2 * MOBA_BLOCK
BF16_SUBLANES = 16
MOBA_VPAD = BF16_SUBLANES
MOBA_VROWS = MOBA_DH + MOBA_VPAD
OUTX_TM = 512
OUTX_WCHUNK = 256


def _params(*semantics):
    return pltpu.CompilerParams(dimension_semantics=semantics, vmem_limit_bytes=VMEM_LIMIT)


def _rms(x, g):
    return x * lax.rsqrt(jnp.mean(x * x, axis=-1, keepdims=True) + EPS) * g


def _dot(a, b):
    return jnp.dot(a, b, preferred_element_type=F32)


def _dot_nt(a, b):
    return lax.dot_general(a, b, (((1,), (1,)), ((), ())), preferred_element_type=F32)


def _silu(x):
    return x * (1.0 / (1.0 + jnp.exp(-x)))


def _ffn_kernel(x_hbm, g_ref, wg_ref, wu_ref, wd_ref, o_hbm, h_ref, xbuf_ref, acc_ref, xsem, osem):
    i = pl.program_id(0)
    f = pl.program_id(1)
    last_i = pl.num_programs(0) - 1
    last_f = pl.num_programs(1) - 1

    def x_copy(tile):
        return pltpu.make_async_copy(x_hbm.at[pl.ds(tile * FFN_TM, FFN_TM)], xbuf_ref, xsem)

    def o_copy(tile):
        return pltpu.make_async_copy(acc_ref, o_hbm.at[pl.ds(tile * FFN_TM, FFN_TM)], osem)

    @pl.when(jnp.logical_and(i == 0, f == 0))
    def _():
        x_copy(0).start()

    @pl.when(f == 0)
    def _():
        x_copy(i).wait()
        x = xbuf_ref[...]
        h_ref[...] = _rms(x, g_ref[...]).astype(BF16)

        @pl.when(i > 0)
        def _():
            o_copy(i - 1).wait()

        acc_ref[...] = xbuf_ref[...]

    @pl.when(jnp.logical_and(f == 1, i < last_i))
    def _():
        x_copy(i + 1).start()

    h = h_ref[...]
    gate = _dot(h, wg_ref[...].astype(BF16))
    up = _dot(h, wu_ref[...].astype(BF16))
    act = (0.5 * _silu(gate) * up).astype(BF16)
    acc_ref[...] += _dot(act, wd_ref[...].astype(BF16))

    @pl.when(f == last_f)
    def _():
        o_copy(i).start()

        @pl.when(i == last_i)
        def _():
            o_copy(i).wait()


def _ffn(x, g, wg, wu, wd):
    n, d = x.shape
    grid = (n // FFN_TM, D_FF // FFN_TF)
    assert grid[1] >= 2
    return pl.pallas_call(
        _ffn_kernel,
        grid=grid,
        in_specs=[
            pl.BlockSpec(memory_space=pl.ANY),
            pl.BlockSpec((1, d), lambda i, f: (0, 0)),
            pl.BlockSpec((d, FFN_TF), lambda i, f: (0, f)),
            pl.BlockSpec((d, FFN_TF), lambda i, f: (0, f)),
            pl.BlockSpec((FFN_TF, d), lambda i, f: (f, 0)),
        ],
        out_specs=pl.BlockSpec(memory_space=pl.ANY),
        out_shape=jax.ShapeDtypeStruct((n, d), F32),
        scratch_shapes=[pltpu.VMEM((FFN_TM, d), BF16), pltpu.VMEM((FFN_TM, d), F32),
                        pltpu.VMEM((FFN_TM, d), F32),
                        pltpu.SemaphoreType.DMA, pltpu.SemaphoreType.DMA],
        compiler_params=pltpu.CompilerParams(dimension_semantics=("arbitrary", "arbitrary"),
                                             vmem_limit_bytes=FFN_VMEM_LIMIT),
        name="ffn",
    )(x, g.reshape(1, d), wg, wu, wd)


def _mixnorm_kernel(x_ref, g_ref, wlr_ref, wg2_ref, bg2_ref, hn_ref, la_ref):
    h = _rms(x_ref[...], g_ref[...]).astype(BF16)
    hn_ref[...] = h
    lr = _dot_nt(h, wlr_ref[...].astype(BF16))
    pre = _dot(lr.astype(BF16), wg2_ref[...].astype(BF16)) + bg2_ref[...]
    log_sig = -(jnp.maximum(-pre, 0.0) + jnp.log(1.0 + jnp.exp(-jnp.abs(pre))))
    la_ref[...] = log_sig * (1.0 / GLA_GATE_TAU)


def _mixnorm(x1, g, w_lr_t, wg2, bg2):
    n, d = x1.shape
    const = lambda i: (0, 0)
    return pl.pallas_call(
        _mixnorm_kernel,
        grid=(n // NORM_TM,),
        in_specs=[
            pl.BlockSpec((NORM_TM, d), lambda i: (i, 0)),
            pl.BlockSpec((1, d), const),
            pl.BlockSpec((GLA_GATE_RANK, d), const),
            pl.BlockSpec((GLA_GATE_RANK, GLA_QK), const),
            pl.BlockSpec((1, GLA_QK), const),
        ],
        out_specs=[pl.BlockSpec((NORM_TM, d), lambda i: (i, 0)),
                   pl.BlockSpec((NORM_TM, GLA_QK), lambda i: (i, 0))],
        out_shape=[jax.ShapeDtypeStruct((n, d), BF16),
                   jax.ShapeDtypeStruct((n, GLA_QK), F32)],
        compiler_params=_params("parallel"),
        name="mixnorm",
    )(x1, g.reshape(1, d), w_lr_t, wg2, bg2.reshape(1, -1))


def _projg_kernel(hn_ref, wt_ref, u_ref):
    u_ref[...] = _dot_nt(hn_ref[...], wt_ref[0].astype(BF16))


def _projq_kernel(hn_ref, wt_ref, g_ref, qt_ref):
    yt = _dot_nt(wt_ref[0].astype(BF16), hn_ref[...])
    g = g_ref[...]
    parts = []
    for k in range(PROJ_TN // MOBA_DH):
        p = yt[k * MOBA_DH:(k + 1) * MOBA_DH, :]
        inv = lax.rsqrt(jnp.mean(p * p, axis=0, keepdims=True) + EPS)
        parts.append(p * inv * g)
    qt_ref[...] = jnp.concatenate(parts, axis=0)


def _projk_kernel(hn_ref, wt_ref, g_ref, kb_ref, kmean_ref):
    y = _dot_nt(hn_ref[...], wt_ref[0].astype(BF16))
    g = g_ref[...]
    parts = [_rms(y[:, k * MOBA_DH:(k + 1) * MOBA_DH], g) for k in range(PROJ_TN // MOBA_DH)]
    kn = jnp.concatenate(parts, axis=1)
    kb_ref[...] = kn.astype(BF16)
    blocks = PROJ_TM // MOBA_BLOCK
    kmean_ref[0] = jnp.sum(kn.reshape(blocks, MOBA_BLOCK, PROJ_TN), axis=1) * (1.0 / MOBA_BLOCK)


def _projv_kernel(hn_ref, wt_ref, v3_ref):
    yt = _dot_nt(wt_ref[0].astype(BF16), hn_ref[...]).astype(BF16)
    ones = jnp.ones((MOBA_VPAD, PROJ_TM), BF16)
    parts = []
    for k in range(PROJ_TN // MOBA_DH):
        parts += [yt[k * MOBA_DH:(k + 1) * MOBA_DH], ones]
    ya = jnp.concatenate(parts, axis=0)
    for t in range(PROJ_TM // MOBA_BLOCK):
        v3_ref[t] = ya[:, t * MOBA_BLOCK:(t + 1) * MOBA_BLOCK]


def _proj_kernel(hn_ref, wt_ref, gq_ref, gk_ref, u_ref, qt_ref, kb_ref, kmean_ref, v3_ref):
    j = pl.program_id(1)
    gla = GLA_COLS // PROJ_TN
    steps = MOBA_WIDTH // PROJ_TN

    @pl.when(j < gla)
    def _():
        _projg_kernel(hn_ref, wt_ref, u_ref)

    @pl.when(jnp.logical_and(j >= gla, j < gla + steps))
    def _():
        _projq_kernel(hn_ref, wt_ref, gq_ref, qt_ref)

    @pl.when(jnp.logical_and(j >= gla + steps, j < gla + 2 * steps))
    def _():
        _projk_kernel(hn_ref, wt_ref, gk_ref, kb_ref, kmean_ref)

    @pl.when(j >= gla + 2 * steps)
    def _():
        _projv_kernel(hn_ref, wt_ref, v3_ref)


def _proj(hn, w_in_t, layer, gq, gk):
    n, d = hn.shape
    gla = GLA_COLS // PROJ_TN
    steps = MOBA_WIDTH // PROJ_TN
    blocks = PROJ_TM // MOBA_BLOCK
    moba_gap = GLA_GATE_RANK
    own = lambda j, first: jnp.clip(j - gla - first * steps, 0, steps - 1)
    const = lambda i, j: (0, 0)
    u, q_t, kb, kmean, v3 = pl.pallas_call(
        _proj_kernel,
        grid=(n // PROJ_TM, gla + 3 * steps),
        in_specs=[
            pl.BlockSpec((PROJ_TM, d), lambda i, j: (i, 0)),
            pl.BlockSpec((pl.Element(1), pl.Element(PROJ_TN), pl.Element(d)),
                         lambda i, j: (layer,
                                       pl.multiple_of(j * PROJ_TN + jnp.where(j >= gla, moba_gap, 0), 8), 0)),
            pl.BlockSpec((MOBA_DH, 1), const),
            pl.BlockSpec((1, MOBA_DH), const),
        ],
        out_specs=[
            pl.BlockSpec((PROJ_TM, PROJ_TN), lambda i, j: (i, jnp.minimum(j, gla - 1))),
            pl.BlockSpec((PROJ_TN, PROJ_TM), lambda i, j: (own(j, 0), i)),
            pl.BlockSpec((PROJ_TM, PROJ_TN), lambda i, j: (i, own(j, 1))),
            pl.BlockSpec((1, blocks, PROJ_TN), lambda i, j: (i, 0, own(j, 1))),
            pl.BlockSpec((blocks, PROJ_TN // MOBA_DH * MOBA_VROWS, MOBA_BLOCK),
                         lambda i, j: (i, own(j, 2), 0)),
        ],
        out_shape=[jax.ShapeDtypeStruct((n, GLA_COLS), F32),
                   jax.ShapeDtypeStruct((MOBA_WIDTH, n), F32),
                   jax.ShapeDtypeStruct((n, MOBA_WIDTH), BF16),
                   jax.ShapeDtypeStruct((n // PROJ_TM, blocks, MOBA_WIDTH), F32),
                   jax.ShapeDtypeStruct((n // MOBA_BLOCK, MOBA_HEADS * MOBA_VROWS, MOBA_BLOCK), BF16)],
        compiler_params=_params("arbitrary", "arbitrary"),
        name="proj",
    )(hn, w_in_t, gq.reshape(-1, 1), gk.reshape(1, -1))
    return u, q_t, kb, kmean.reshape(n // MOBA_BLOCK, MOBA_WIDTH), v3


def _dot_exact_rhs(m, a):
    a1 = a.astype(BF16)
    r1 = a - a1.astype(F32)
    a2 = r1.astype(BF16)
    a3 = (r1 - a2.astype(F32)).astype(BF16)
    return _dot(m, a1) + _dot(m, a2) + _dot(m, a3)


def _gla_head(q, k, v, r, la, g_out, state, upd_ref):
    tc = GLA_TC
    nchunk = tc // GLA_CHUNK
    sub = GLA_SUB

    crow = lax.broadcasted_iota(jnp.int32, (GLA_CHUNK, GLA_CHUNK), 0)
    ccol = lax.broadcasted_iota(jnp.int32, (GLA_CHUNK, GLA_CHUNK), 1)
    tri = (ccol <= crow).astype(BF16)
    la_wide = jnp.concatenate([la[c * GLA_CHUNK:(c + 1) * GLA_CHUNK] for c in range(nchunk)], axis=1)
    bc_wide = _dot_exact_rhs(tri, la_wide)
    bcum = jnp.concatenate([bc_wide[:, c * GLA_DK:(c + 1) * GLA_DK] for c in range(nchunk)], axis=0)
    btot = jnp.concatenate(
        [jnp.broadcast_to(bc_wide[GLA_CHUNK - 1:GLA_CHUNK, c * GLA_DK:(c + 1) * GLA_DK],
                          (GLA_CHUNK, GLA_DK)) for c in range(nchunk)], axis=0)

    q_dec_b = ((q * (GLA_DK ** -0.5)) * jnp.exp(bcum)).astype(BF16)
    k_inv_b = (k * jnp.exp(-bcum)).astype(BF16)
    k_tail_t = (k * jnp.exp(btot - bcum)).T.astype(BF16)
    decay_t = jnp.exp(btot).T
    vb = v.astype(BF16)

    row = lax.broadcasted_iota(jnp.int32, (sub, sub), 0)
    col = lax.broadcasted_iota(jnp.int32, (sub, sub), 1)
    causal = jnp.logical_and((row // GLA_CHUNK) == (col // GLA_CHUNK), col <= row)
    intra = []
    for p in range(tc // sub):
        rows = slice(p * sub, (p + 1) * sub)
        att = jnp.where(causal, _dot_nt(q_dec_b[rows], k_inv_b[rows]), 0.0)
        intra.append(_dot(att.astype(BF16), vb[rows]))

    for c in range(nchunk):
        rows = slice(c * GLA_CHUNK, (c + 1) * GLA_CHUNK)
        upd_ref[c] = _dot(k_tail_t[:, rows], vb[rows])

    inter = []
    for c in range(nchunk):
        rows = slice(c * GLA_CHUNK, (c + 1) * GLA_CHUNK)
        inter.append(_dot(q_dec_b[rows], state.astype(BF16)))
        dec_c = decay_t[:, c * GLA_CHUNK:c * GLA_CHUNK + 1]
        state = dec_c * state + upd_ref[c]
    o = jnp.concatenate(intra, axis=0) + jnp.concatenate(inter, axis=0)
    return _rms(o, g_out) * _silu(r), state


def _gla_kernel(q_ref, k_ref, v_ref, r_ref, la_ref, go_ref, o_ref, s_ref, upd_ref):
    @pl.when(pl.program_id(2) == 0)
    def _():
        s_ref[...] = jnp.zeros_like(s_ref)

    for g in range(GLA_G):
        qk = slice(g * GLA_DK, (g + 1) * GLA_DK)
        vr = slice(g * GLA_DV, (g + 1) * GLA_DV)
        o, state = _gla_head(q_ref[:, qk], k_ref[:, qk], v_ref[:, vr], r_ref[:, vr], la_ref[:, qk],
                             go_ref[...], s_ref[g], upd_ref.at[g])
        s_ref[g] = state
        o_ref[:, vr] = o


def _gla(u, la, g_out, batch, seq):
    n = u.shape[0]
    nt = seq // GLA_TC
    tok = lambda b, h, t: b * nt + t
    qk_w, v_w = GLA_G * GLA_DK, GLA_G * GLA_DV
    k_blk0 = GLA_QK // qk_w
    v_blk0 = 2 * GLA_QK // v_w
    r_blk0 = v_blk0 + GLA_WIDTH // v_w
    return pl.pallas_call(
        _gla_kernel,
        grid=(batch, GLA_HEADS // GLA_G, nt),
        in_specs=[
            pl.BlockSpec((GLA_TC, qk_w), lambda b, h, t: (tok(b, h, t), h)),
            pl.BlockSpec((GLA_TC, qk_w), lambda b, h, t: (tok(b, h, t), k_blk0 + h)),
            pl.BlockSpec((GLA_TC, v_w), lambda b, h, t: (tok(b, h, t), v_blk0 + h)),
            pl.BlockSpec((GLA_TC, v_w), lambda b, h, t: (tok(b, h, t), r_blk0 + h)),
            pl.BlockSpec((GLA_TC, qk_w), lambda b, h, t: (tok(b, h, t), h)),
            pl.BlockSpec((1, GLA_DV), lambda b, h, t: (0, 0)),
        ],
        out_specs=pl.BlockSpec((GLA_TC, v_w), lambda b, h, t: (tok(b, h, t), h)),
        out_shape=jax.ShapeDtypeStruct((n, GLA_WIDTH), F32),
        scratch_shapes=[pltpu.VMEM((GLA_G, GLA_DK, GLA_DV), F32),
                        pltpu.VMEM((GLA_G, GLA_TC // GLA_CHUNK, GLA_DK, GLA_DV), F32)],
        compiler_params=_params("parallel", "parallel", "arbitrary"),
        name="gla",
    )(u, u, u, u, la, g_out.reshape(1, -1))


def _moba_kernel(qt_ref, kb_ref, v3_ref, kmean_ref, o_ref, kaug_ref, snext_ref):
    ti = pl.program_id(2)
    last_tile = pl.num_programs(2) - 1
    bs = MOBA_BLOCK
    tq = MOBA_TQ
    nb = kb_ref.shape[0] // bs
    dh = MOBA_DH
    vr = MOBA_VROWS
    scale = (dh ** -0.5) * LOG2_E

    @pl.when(ti == 0)
    def _():
        seq = kb_ref.shape[0]
        row_blk = lax.broadcasted_iota(jnp.int32, (seq, MXU_DIM - dh), 0) // bs
        lane = lax.broadcasted_iota(jnp.int32, (seq, MXU_DIM - dh), 1)
        onehot = (row_blk == lane).astype(BF16)
        for g in range(MOBA_G):
            kaug_ref[g, :, :dh] = kb_ref[:, g * dh:(g + 1) * dh]
            kaug_ref[g, :, dh:] = onehot

    blk = lax.broadcasted_iota(jnp.int32, (nb, tq), 0)
    q_blk = 2 * ti + (lax.broadcasted_iota(jnp.int32, (nb, tq), 1) // bs)
    past = blk < q_blk
    causal = (lax.broadcasted_iota(jnp.int32, (bs, bs), 0)
              <= lax.broadcasted_iota(jnp.int32, (bs, bs), 1))
    diag = pl.multiple_of(ti * tq, tq)

    def pv_dot(g, t, p):
        return (_dot(v3_ref[2 * t, g * vr:(g + 1) * vr, :], p[:bs])
                + _dot(v3_ref[2 * t + 1, g * vr:(g + 1) * vr, :], p[bs:]))

    q_aug, biases, state = [], [], []

    def score(g, t):
        off = pl.multiple_of(t * tq, tq)
        s = _dot(kaug_ref[g, pl.ds(off, tq), :], q_aug[g])
        return s, jnp.max(s, axis=0, keepdims=True)

    for g in range(MOBA_G):
        qt = qt_ref[g * dh:(g + 1) * dh, :]
        gate = jnp.dot(kmean_ref[:, g * dh:(g + 1) * dh], qt,
                       precision=lax.Precision.HIGHEST, preferred_element_type=F32)
        gate = jnp.where(past, gate, NEG_INF)
        rank = jnp.zeros((nb, tq), jnp.int32)
        for j in range(nb):
            gj = gate[j:j + 1, :]
            ahead = jnp.logical_or(gj > gate, jnp.logical_and(gj == gate, blk > j))
            rank = rank + ahead.astype(jnp.int32)
        chosen = jnp.logical_and(past, rank < MOBA_TOPK)
        bias = jnp.where(chosen, 0.0, NEG_INF)
        qa = jnp.concatenate([(qt * scale).astype(BF16), bias.astype(BF16),
                              jnp.zeros((MXU_DIM - dh - nb, tq), BF16)], axis=0)
        q_aug.append(qa)
        biases.append(bias)

    own_scores, first_maxes = [], []
    for g in range(MOBA_G):
        qa, bias = q_aug[g], biases[g]
        first_bias = jnp.sum(jnp.where(blk == 2 * ti, bias, 0.0), axis=0, keepdims=True)[:, bs:]
        s0 = _dot(kaug_ref[g, pl.ds(diag, bs), :dh], qa[:dh])
        s00 = jnp.where(causal, s0[:, :bs], NEG_INF)
        s01 = s0[:, bs:] + first_bias
        s11 = jnp.where(causal, _dot(kaug_ref[g, pl.ds(pl.multiple_of(diag + bs, bs), bs), :dh],
                                     qa[:dh, bs:]), NEG_INF)
        own_scores.append((s00, s01, s11))
    for g in range(MOBA_G):
        s_first, first_max = score(g, 0)
        snext_ref[g] = s_first
        first_maxes.append(first_max)
    for g in range(MOBA_G):
        s00, s01, s11 = own_scores[g]
        m0 = jnp.max(s00, axis=0, keepdims=True)
        m1 = jnp.maximum(jnp.max(s01, axis=0, keepdims=True), jnp.max(s11, axis=0, keepdims=True))
        p0 = jnp.concatenate([jnp.exp2(s00 - m0), jnp.exp2(s01 - m1)], axis=1).astype(BF16)
        p11 = jnp.exp2(s11 - m1).astype(BF16)
        acc0 = _dot(v3_ref[2 * ti, g * vr:(g + 1) * vr, :], p0)
        acc1 = _dot(v3_ref[2 * ti + 1, g * vr:(g + 1) * vr, :], p11)
        acc = jnp.concatenate([acc0[:, :bs], acc0[:, bs:] + acc1], axis=1)
        state.append((jnp.concatenate([m0, m1], axis=1), acc, first_maxes[g]))

    def consume(g, t, s, tile_max, m, acc, valid=None):
        if valid is not None:
            tile_max = jnp.where(valid, tile_max, NEG_INF)
        m_new = jnp.maximum(m, tile_max)
        alpha = jnp.exp2(m - m_new)
        shift = m_new if valid is None else jnp.where(valid, m_new, -NEG_INF)
        p = jnp.exp2(s - shift).astype(BF16)
        return m_new, alpha * acc + pv_dot(g, t, p)

    def body(u, carry):
        out = []
        for g in range(MOBA_G):
            m, acc, max_a = carry[g]
            s_b, max_b = score(g, 2 * u + 1)
            m, acc = consume(g, 2 * u, snext_ref[g], max_a, m, acc)
            s_c, max_c = score(g, jnp.minimum(2 * u + 2, last_tile))
            snext_ref[g] = s_c
            m, acc = consume(g, 2 * u + 1, s_b, max_b, m, acc, valid=2 * u + 1 < ti)
            out.append((m, acc, max_c))
        return tuple(out)

    carry = lax.fori_loop(0, (ti + 1) // 2, body, tuple(state))
    for g in range(MOBA_G):
        m, acc, _ = carry[g]
        o_ref[:, g * dh:(g + 1) * dh] = (acc[:dh] * (1.0 / acc[dh:dh + 1])).T


def _moba(q_t, kb, kmean, v3, batch, seq):
    n = kb.shape[0]
    nb = seq // MOBA_BLOCK
    nt = seq // MOBA_TQ
    gw = MOBA_G * MOBA_DH
    return pl.pallas_call(
        _moba_kernel,
        grid=(batch, MOBA_HEADS // MOBA_G, nt),
        in_specs=[
            pl.BlockSpec((gw, MOBA_TQ), lambda b, h, i: (h, b * nt + i)),
            pl.BlockSpec((seq, gw), lambda b, h, i: (b, h)),
            pl.BlockSpec((nb, MOBA_G * MOBA_VROWS, MOBA_BLOCK), lambda b, h, i: (b, h, 0)),
            pl.BlockSpec((nb, gw), lambda b, h, i: (b, h)),
        ],
        out_specs=pl.BlockSpec((MOBA_TQ, gw), lambda b, h, i: (b * nt + i, h)),
        out_shape=jax.ShapeDtypeStruct((n, MOBA_WIDTH), F32),
        scratch_shapes=[pltpu.VMEM((MOBA_G, seq, MXU_DIM), BF16),
                        pltpu.VMEM((MOBA_G, MOBA_TQ, MOBA_TQ), F32)],
        compiler_params=_params("parallel", "parallel", "arbitrary"),
        name="moba",
    )(q_t, kb, v3, kmean)


def _memkv_kernel(mem_ref, g_ref, wkv_ref, gk_ref, k_ref, v_ref):
    h = _rms(mem_ref[...], g_ref[...]).astype(BF16)
    kv = _dot(h, wkv_ref[...].astype(BF16))
    gk = gk_ref[...]
    ks = [_rms(kv[:, i * XATTN_DH:(i + 1) * XATTN_DH], gk) for i in range(XATTN_HEADS)]
    k_ref[...] = jnp.concatenate(ks, axis=1)
    v_ref[...] = kv[:, XATTN_WIDTH:]


def _memkv(mem, g, wkv_b, gk):
    m, d = mem.shape
    return pl.pallas_call(
        _memkv_kernel,
        out_shape=[jax.ShapeDtypeStruct((m, XATTN_WIDTH), F32),
                   jax.ShapeDtypeStruct((m, XATTN_WIDTH), F32)],
        compiler_params=pltpu.CompilerParams(vmem_limit_bytes=VMEM_LIMIT),
        name="memkv",
    )(mem, g.reshape(1, d), wkv_b, gk.reshape(1, -1))


def _load_bf16_weights(pairs, stage_ref, sem):
    chunks = [(src, dst, r) for src, dst in pairs for r in range(0, dst.shape[0], OUTX_WCHUNK)]

    def copy(k):
        src, dst, r = chunks[k]
        return pltpu.make_async_copy(src.at[pl.ds(r, OUTX_WCHUNK), :],
                                     stage_ref.at[k % 2, :, :dst.shape[1]], sem.at[k % 2])

    copy(0).start()
    for k, (_, dst, r) in enumerate(chunks):
        if k + 1 < len(chunks):
            copy(k + 1).start()
        copy(k).wait()
        dst[r:r + OUTX_WCHUNK, :] = stage_ref[k % 2, :, :dst.shape[1]].astype(BF16)


def _outx_kernel(x1_ref, og_ref, om_ref, wout_hbm, gx_ref, wq_hbm, gq_ref, kx_ref, vx_ref, wo_hbm,
                 o_ref, wout_ref, wq_ref, wo_ref, stage_ref, wsem):
    @pl.when(pl.program_id(0) == 0)
    def _():
        _load_bf16_weights(((wout_hbm, wout_ref), (wq_hbm, wq_ref), (wo_hbm, wo_ref)), stage_ref, wsem)

    x2 = (x1_ref[...]
          + _dot(og_ref[...].astype(BF16), wout_ref[:GLA_WIDTH, :])
          + _dot(om_ref[...].astype(BF16), wout_ref[GLA_WIDTH:, :]))
    h = _rms(x2, gx_ref[...]).astype(BF16)
    q = _dot(h, wq_ref[...])
    gq = gq_ref[...]
    scale = (XATTN_DH ** -0.5) * LOG2_E
    cols = [slice(i * XATTN_DH, (i + 1) * XATTN_DH) for i in range(XATTN_HEADS)]
    qh = [(_rms(q[:, c], gq) * scale).astype(BF16) for c in cols]
    ss = [_dot_nt(qh[i], kx_ref[:, c].astype(BF16)) for i, c in enumerate(cols)]
    ps = []
    for s in ss:
        p = jnp.exp2(s - jnp.max(s, axis=-1, keepdims=True))
        ps.append((p * (1.0 / jnp.sum(p, axis=-1, keepdims=True))).astype(BF16))
    heads = [_dot(ps[i], vx_ref[:, c].astype(BF16)) for i, c in enumerate(cols)]
    o = jnp.concatenate(heads, axis=1).astype(BF16)
    o_ref[...] = x2 + _dot(o, wo_ref[...])


def _outx(x1, og, om, w_out, gx, w_q, gq, kx, vx, w_o, seq, mem_len):
    n, d = x1.shape
    any_spec = pl.BlockSpec(memory_space=pl.ANY)
    tiles_per_seq = seq // OUTX_TM
    const = lambda i: (0, 0)
    return pl.pallas_call(
        _outx_kernel,
        grid=(n // OUTX_TM,),
        in_specs=[
            pl.BlockSpec((OUTX_TM, d), lambda i: (i, 0)),
            pl.BlockSpec((OUTX_TM, GLA_WIDTH), lambda i: (i, 0)),
            pl.BlockSpec((OUTX_TM, MOBA_WIDTH), lambda i: (i, 0)),
            any_spec,
            pl.BlockSpec((1, d), const),
            any_spec,
            pl.BlockSpec((1, XATTN_DH), const),
            pl.BlockSpec((mem_len, XATTN_WIDTH), lambda i: (i // tiles_per_seq, 0)),
            pl.BlockSpec((mem_len, XATTN_WIDTH), lambda i: (i // tiles_per_seq, 0)),
            any_spec,
        ],
        out_specs=pl.BlockSpec((OUTX_TM, d), lambda i: (i, 0)),
        out_shape=jax.ShapeDtypeStruct((n, d), F32),
        scratch_shapes=[pltpu.VMEM(w_out.shape, BF16), pltpu.VMEM(w_q.shape, BF16), pltpu.VMEM(w_o.shape, BF16),
                        pltpu.VMEM((2, OUTX_WCHUNK, d), F32), pltpu.SemaphoreType.DMA((2,))],
        compiler_params=_params("arbitrary"),
        name="outx",
    )(x1, og, om, w_out, gx.reshape(1, d), w_q, gq.reshape(1, -1), kx, vx, w_o)


def kernel(x, mem, ffn1_norm, ffn1_w_gate, ffn1_w_up, ffn1_w_down, mix_norm, w_in, gla_w_gate2, gla_b_gate2, gla_out_norm, moba_q_norm, moba_k_norm, w_out, xattn_norm, mem_norm, xattn_w_q, xattn_w_kv, xattn_w_o, xattn_q_norm, xattn_k_norm, ffn2_norm, ffn2_w_gate, ffn2_w_up, ffn2_w_down):
    batch, seq, d = x.shape
    mem_len = mem.shape[1]
    depth = ffn1_norm.shape[0]
    n = batch * seq
    assert seq % GLA_TC == 0 and seq % (2 * MOBA_BLOCK) == 0 and seq % OUTX_TM == 0
    assert seq // MOBA_BLOCK + MOBA_DH <= MXU_DIM and (seq // MOBA_BLOCK) % BF16_SUBLANES == 0
    assert n % FFN_TM == 0 and n % NORM_TM == 0 and n % PROJ_TM == 0 and PROJ_TM % MOBA_BLOCK == 0

    xf = x.reshape(n, d)
    memf = mem.reshape(batch * mem_len, d)
    lr0 = GLA_COLS
    mq0 = lr0 + GLA_GATE_RANK
    w_in_t = jnp.swapaxes(w_in, 1, 2)
    for l in range(depth):
        w_lr_t = w_in_t[l, lr0:mq0]

        x1 = _ffn(xf, ffn1_norm[l], ffn1_w_gate[l], ffn1_w_up[l], ffn1_w_down[l])
        hn, la = _mixnorm(x1, mix_norm[l], w_lr_t, gla_w_gate2[l], gla_b_gate2[l])
        u, q_t, kb, kmean, v3 = _proj(hn, w_in_t, l, moba_q_norm[l], moba_k_norm[l])
        o_gla = _gla(u, la, gla_out_norm[l], batch, seq)
        o_moba = _moba(q_t, kb, kmean, v3, batch, seq)
        kx, vx = _memkv(memf, mem_norm[l], xattn_w_kv[l], xattn_k_norm[l])
        x3 = _outx(x1, o_gla, o_moba, w_out[l], xattn_norm[l], xattn_w_q[l], xattn_q_norm[l], kx, vx,
                   xattn_w_o[l], seq, mem_len)
        xf = _ffn(x3, ffn2_norm[l], ffn2_w_gate[l], ffn2_w_up[l], ffn2_w_down[l])
    return xf.reshape(batch, seq, d)
```

```python
import jax
import jax.numpy as jnp
from jax import lax
from jax.experimental import pallas as pl
from jax.experimental.pallas import tpu as pltpu

F32 = jnp.float32
BF16 = jnp.bfloat16

EPS = 1e-6
NEG_INF = -1e30
LOG2_E = 1.4426950408889634
D_FF = 5632
GLA_HEADS = 4
GLA_DV = 256
GLA_DK = 128
GLA_QK = GLA_HEADS * GLA_DK
GLA_WIDTH = GLA_HEADS * GLA_DV
GLA_COLS = 2 * GLA_QK + 2 * GLA_WIDTH
GLA_GATE_RANK = 16
GLA_GATE_TAU = 16.0
GLA_CHUNK = 64
MOBA_DH = 128
MOBA_HEADS = 8
MOBA_WIDTH = MOBA_HEADS * MOBA_DH
MOBA_BLOCK = 256
MOBA_TOPK = 3
XATTN_HEADS = 4
XATTN_DH = 128
XATTN_WIDTH = XATTN_HEADS * XATTN_DH

V7X_VMEM_BYTES = 64 * 1024 * 1024
VMEM_LIMIT = V7X_VMEM_BYTES - 6 * 1024 * 1024
FFN_VMEM_LIMIT = V7X_VMEM_BYTES - 2 * 1024 * 1024
MXU_DIM = 256

FFN_TM = 1024
FFN_TF = 512
NORM_TM = 1024
PROJ_TM = 2048
PROJ_TN = 512
GLA_TC = 512
GLA_SUB = 256
GLA_G = 4
MOBA_G = 4
MOBA_TQ = 2 * MOBA_BLOCK
BF16_SUBLANES = 16
MOBA_VPAD = BF16_SUBLANES
MOBA_VROWS = MOBA_DH + MOBA_VPAD
OUTX_TM = 512
OUTX_WCHUNK = 512


def _params(*semantics):
    return pltpu.CompilerParams(dimension_semantics=semantics, vmem_limit_bytes=VMEM_LIMIT)


def _rms(x, g):
    return x * lax.rsqrt(jnp.mean(x * x, axis=-1, keepdims=True) + EPS) * g


def _dot(a, b):
    return jnp.dot(a, b, preferred_element_type=F32)


def _dot_nt(a, b):
    return lax.dot_general(a, b, (((1,), (1,)), ((), ())), preferred_element_type=F32)


def _silu(x):
    return x * (1.0 / (1.0 + jnp.exp(-x)))


def _ffn_kernel(x_hbm, g_ref, wg_ref, wu_ref, wd_ref, o_hbm, h_ref, xbuf_ref, acc_ref, xsem, osem):
    i = pl.program_id(0)
    f = pl.program_id(1)
    last_i = pl.num_programs(0) - 1
    last_f = pl.num_programs(1) - 1

    def x_copy(tile):
        return pltpu.make_async_copy(x_hbm.at[pl.ds(tile * FFN_TM, FFN_TM)], xbuf_ref, xsem)

    def o_copy(tile):
        return pltpu.make_async_copy(acc_ref, o_hbm.at[pl.ds(tile * FFN_TM, FFN_TM)], osem)

    @pl.when(jnp.logical_and(i == 0, f == 0))
    def _():
        x_copy(0).start()

    @pl.when(f == 0)
    def _():
        x_copy(i).wait()
        x = xbuf_ref[...]
        h_ref[...] = _rms(x, g_ref[...]).astype(BF16)

        @pl.when(i > 0)
        def _():
            o_copy(i - 1).wait()

        acc_ref[...] = xbuf_ref[...]

    @pl.when(jnp.logical_and(f == 1, i < last_i))
    def _():
        x_copy(i + 1).start()

    h = h_ref[...]
    gate = _dot(h, wg_ref[...].astype(BF16))
    up = _dot(h, wu_ref[...].astype(BF16))
    act = (0.5 * _silu(gate) * up).astype(BF16)
    acc_ref[...] += _dot(act, wd_ref[...].astype(BF16))

    @pl.when(f == last_f)
    def _():
        o_copy(i).start()

        @pl.when(i == last_i)
        def _():
            o_copy(i).wait()


def _ffn(x, g, wg, wu, wd):
    n, d = x.shape
    grid = (n // FFN_TM, D_FF // FFN_TF)
    assert grid[1] >= 2
    return pl.pallas_call(
        _ffn_kernel,
        grid=grid,
        in_specs=[
            pl.BlockSpec(memory_space=pl.ANY),
            pl.BlockSpec((1, d), lambda i, f: (0, 0)),
            pl.BlockSpec((d, FFN_TF), lambda i, f: (0, f)),
            pl.BlockSpec((d, FFN_TF), lambda i, f: (0, f)),
            pl.BlockSpec((FFN_TF, d), lambda i, f: (f, 0)),
        ],
        out_specs=pl.BlockSpec(memory_space=pl.ANY),
        out_shape=jax.ShapeDtypeStruct((n, d), F32),
        scratch_shapes=[pltpu.VMEM((FFN_TM, d), BF16), pltpu.VMEM((FFN_TM, d), F32),
                        pltpu.VMEM((FFN_TM, d), F32),
                        pltpu.SemaphoreType.DMA, pltpu.SemaphoreType.DMA],
        compiler_params=pltpu.CompilerParams(dimension_semantics=("arbitrary", "arbitrary"),
                                             vmem_limit_bytes=FFN_VMEM_LIMIT),
        name="ffn",
    )(x, g.reshape(1, d), wg, wu, wd)


def _mixnorm_kernel(x_ref, g_ref, wlr_ref, wg2_ref, bg2_ref, hn_ref, la_ref):
    h = _rms(x_ref[...], g_ref[...]).astype(BF16)
    hn_ref[...] = h
    lr = _dot_nt(h, wlr_ref[...].astype(BF16))
    pre = _dot(lr.astype(BF16), wg2_ref[...].astype(BF16)) + bg2_ref[...]
    log_sig = -(jnp.maximum(-pre, 0.0) + jnp.log(1.0 + jnp.exp(-jnp.abs(pre))))
    la_ref[...] = log_sig * (1.0 / GLA_GATE_TAU)


def _mixnorm(x1, g, w_lr_t, wg2, bg2):
    n, d = x1.shape
    const = lambda i: (0, 0)
    return pl.pallas_call(
        _mixnorm_kernel,
        grid=(n // NORM_TM,),
        in_specs=[
            pl.BlockSpec((NORM_TM, d), lambda i: (i, 0)),
            pl.BlockSpec((1, d), const),
            pl.BlockSpec((GLA_GATE_RANK, d), const),
            pl.BlockSpec((GLA_GATE_RANK, GLA_QK), const),
            pl.BlockSpec((1, GLA_QK), const),
        ],
        out_specs=[pl.BlockSpec((NORM_TM, d), lambda i: (i, 0)),
                   pl.BlockSpec((NORM_TM, GLA_QK), lambda i: (i, 0))],
        out_shape=[jax.ShapeDtypeStruct((n, d), BF16),
                   jax.ShapeDtypeStruct((n, GLA_QK), F32)],
        compiler_params=_params("parallel"),
        name="mixnorm",
    )(x1, g.reshape(1, d), w_lr_t, wg2, bg2.reshape(1, -1))


def _projg_kernel(hn_ref, wt_ref, u_ref):
    u_ref[...] = _dot_nt(hn_ref[...], wt_ref[0].astype(BF16))


def _projq_kernel(hn_ref, wt_ref, g_ref, qt_ref):
    yt = _dot_nt(wt_ref[0].astype(BF16), hn_ref[...])
    g = g_ref[...]
    parts = []
    for k in range(PROJ_TN // MOBA_DH):
        p = yt[k * MOBA_DH:(k + 1) * MOBA_DH, :]
        inv = lax.rsqrt(jnp.mean(p * p, axis=0, keepdims=True) + EPS)
        parts.append(p * inv * g)
    qt_ref[...] = jnp.concatenate(parts, axis=0)


def _projk_kernel(hn_ref, wt_ref, g_ref, kb_ref, kmean_ref):
    y = _dot_nt(hn_ref[...], wt_ref[0].astype(BF16))
    g = g_ref[...]
    parts = [_rms(y[:, k * MOBA_DH:(k + 1) * MOBA_DH], g) for k in range(PROJ_TN // MOBA_DH)]
    kn = jnp.concatenate(parts, axis=1)
    kb_ref[...] = kn.astype(BF16)
    blocks = PROJ_TM // MOBA_BLOCK
    kmean_ref[0] = jnp.sum(kn.reshape(blocks, MOBA_BLOCK, PROJ_TN), axis=1) * (1.0 / MOBA_BLOCK)


def _projv_kernel(hn_ref, wt_ref, v3_ref):
    yt = _dot_nt(wt_ref[0].astype(BF16), hn_ref[...]).astype(BF16)
    ones = jnp.ones((MOBA_VPAD, PROJ_TM), BF16)
    parts = []
    for k in range(PROJ_TN // MOBA_DH):
        parts += [yt[k * MOBA_DH:(k + 1) * MOBA_DH], ones]
    ya = jnp.concatenate(parts, axis=0)
    for t in range(PROJ_TM // MOBA_BLOCK):
        v3_ref[t] = ya[:, t * MOBA_BLOCK:(t + 1) * MOBA_BLOCK]


def _proj_kernel(hn_ref, wt_ref, gq_ref, gk_ref, u_ref, qt_ref, kb_ref, kmean_ref, v3_ref):
    j = pl.program_id(1)
    gla = GLA_COLS // PROJ_TN
    steps = MOBA_WIDTH // PROJ_TN

    @pl.when(j < gla)
    def _():
        _projg_kernel(hn_ref, wt_ref, u_ref)

    @pl.when(jnp.logical_and(j >= gla, j < gla + steps))
    def _():
        _projq_kernel(hn_ref, wt_ref, gq_ref, qt_ref)

    @pl.when(jnp.logical_and(j >= gla + steps, j < gla + 2 * steps))
    def _():
        _projk_kernel(hn_ref, wt_ref, gk_ref, kb_ref, kmean_ref)

    @pl.when(j >= gla + 2 * steps)
    def _():
        _projv_kernel(hn_ref, wt_ref, v3_ref)


def _proj(hn, w_in_t, layer, gq, gk):
    n, d = hn.shape
    gla = GLA_COLS // PROJ_TN
    steps = MOBA_WIDTH // PROJ_TN
    blocks = PROJ_TM // MOBA_BLOCK
    moba_gap = GLA_GATE_RANK
    own = lambda j, first: jnp.clip(j - gla - first * steps, 0, steps - 1)
    const = lambda i, j: (0, 0)
    u, q_t, kb, kmean, v3 = pl.pallas_call(
        _proj_kernel,
        grid=(n // PROJ_TM, gla + 3 * steps),
        in_specs=[
            pl.BlockSpec((PROJ_TM, d), lambda i, j: (i, 0)),
            pl.BlockSpec((pl.Element(1), pl.Element(PROJ_TN), pl.Element(d)),
                         lambda i, j: (layer,
                                       pl.multiple_of(j * PROJ_TN + jnp.where(j >= gla, moba_gap, 0), 8), 0)),
            pl.BlockSpec((MOBA_DH, 1), const),
            pl.BlockSpec((1, MOBA_DH), const),
        ],
        out_specs=[
            pl.BlockSpec((PROJ_TM, PROJ_TN), lambda i, j: (i, jnp.minimum(j, gla - 1))),
            pl.BlockSpec((PROJ_TN, PROJ_TM), lambda i, j: (own(j, 0), i)),
            pl.BlockSpec((PROJ_TM, PROJ_TN), lambda i, j: (i, own(j, 1))),
            pl.BlockSpec((1, blocks, PROJ_TN), lambda i, j: (i, 0, own(j, 1))),
            pl.BlockSpec((blocks, PROJ_TN // MOBA_DH * MOBA_VROWS, MOBA_BLOCK),
                         lambda i, j: (i, own(j, 2), 0)),
        ],
        out_shape=[jax.ShapeDtypeStruct((n, GLA_COLS), F32),
                   jax.ShapeDtypeStruct((MOBA_WIDTH, n), F32),
                   jax.ShapeDtypeStruct((n, MOBA_WIDTH), BF16),
                   jax.ShapeDtypeStruct((n // PROJ_TM, blocks, MOBA_WIDTH), F32),
                   jax.ShapeDtypeStruct((n // MOBA_BLOCK, MOBA_HEADS * MOBA_VROWS, MOBA_BLOCK), BF16)],
        compiler_params=_params("arbitrary", "arbitrary"),
        name="proj",
    )(hn, w_in_t, gq.reshape(-1, 1), gk.reshape(1, -1))
    return u, q_t, kb, kmean.reshape(n // MOBA_BLOCK, MOBA_WIDTH), v3


def _dot_exact_rhs(m, a):
    a1 = a.astype(BF16)
    r1 = a - a1.astype(F32)
    a2 = r1.astype(BF16)
    a3 = (r1 - a2.astype(F32)).astype(BF16)
    return _dot(m, a1) + _dot(m, a2) + _dot(m, a3)


def _gla_head(q, k, v, r, la, g_out, state, upd_ref):
    tc = GLA_TC
    nchunk = tc // GLA_CHUNK
    sub = GLA_SUB

    crow = lax.broadcasted_iota(jnp.int32, (GLA_CHUNK, GLA_CHUNK), 0)
    ccol = lax.broadcasted_iota(jnp.int32, (GLA_CHUNK, GLA_CHUNK), 1)
    tri = (ccol <= crow).astype(BF16)
    la_wide = jnp.concatenate([la[c * GLA_CHUNK:(c + 1) * GLA_CHUNK] for c in range(nchunk)], axis=1)
    bc_wide = _dot_exact_rhs(tri, la_wide)
    bcum = jnp.concatenate([bc_wide[:, c * GLA_DK:(c + 1) * GLA_DK] for c in range(nchunk)], axis=0)
    btot = jnp.concatenate(
        [jnp.broadcast_to(bc_wide[GLA_CHUNK - 1:GLA_CHUNK, c * GLA_DK:(c + 1) * GLA_DK],
                          (GLA_CHUNK, GLA_DK)) for c in range(nchunk)], axis=0)

    q_dec_b = ((q * (GLA_DK ** -0.5)) * jnp.exp(bcum)).astype(BF16)
    k_inv_b = (k * jnp.exp(-bcum)).astype(BF16)
    k_tail_t = (k * jnp.exp(btot - bcum)).T.astype(BF16)
    decay_t = jnp.exp(btot).T
    vb = v.astype(BF16)

    row = lax.broadcasted_iota(jnp.int32, (sub, sub), 0)
    col = lax.broadcasted_iota(jnp.int32, (sub, sub), 1)
    causal = jnp.logical_and((row // GLA_CHUNK) == (col // GLA_CHUNK), col <= row)
    intra = []
    for p in range(tc // sub):
        rows = slice(p * sub, (p + 1) * sub)
        att = jnp.where(causal, _dot_nt(q_dec_b[rows], k_inv_b[rows]), 0.0)
        intra.append(_dot(att.astype(BF16), vb[rows]))

    for c in range(nchunk):
        rows = slice(c * GLA_CHUNK, (c + 1) * GLA_CHUNK)
        upd_ref[c] = _dot(k_tail_t[:, rows], vb[rows])

    inter = []
    for c in range(nchunk):
        rows = slice(c * GLA_CHUNK, (c + 1) * GLA_CHUNK)
        inter.append(_dot(q_dec_b[rows], state.astype(BF16)))
        dec_c = decay_t[:, c * GLA_CHUNK:c * GLA_CHUNK + 1]
        state = dec_c * state + upd_ref[c]
    o = jnp.concatenate(intra, axis=0) + jnp.concatenate(inter, axis=0)
    return _rms(o, g_out) * _silu(r), state


def _gla_kernel(q_ref, k_ref, v_ref, r_ref, la_ref, go_ref, o_ref, s_ref, upd_ref):
    @pl.when(pl.program_id(2) == 0)
    def _():
        s_ref[...] = jnp.zeros_like(s_ref)

    for g in range(GLA_G):
        qk = slice(g * GLA_DK, (g + 1) * GLA_DK)
        vr = slice(g * GLA_DV, (g + 1) * GLA_DV)
        o, state = _gla_head(q_ref[:, qk], k_ref[:, qk], v_ref[:, vr], r_ref[:, vr], la_ref[:, qk],
                             go_ref[...], s_ref[g], upd_ref.at[g])
        s_ref[g] = state
        o_ref[:, vr] = o


def _gla(u, la, g_out, batch, seq):
    n = u.shape[0]
    nt = seq // GLA_TC
    tok = lambda b, h, t: b * nt + t
    qk_w, v_w = GLA_G * GLA_DK, GLA_G * GLA_DV
    k_blk0 = GLA_QK // qk_w
    v_blk0 = 2 * GLA_QK // v_w
    r_blk0 = v_blk0 + GLA_WIDTH // v_w
    return pl.pallas_call(
        _gla_kernel,
        grid=(batch, GLA_HEADS // GLA_G, nt),
        in_specs=[
            pl.BlockSpec((GLA_TC, qk_w), lambda b, h, t: (tok(b, h, t), h)),
            pl.BlockSpec((GLA_TC, qk_w), lambda b, h, t: (tok(b, h, t), k_blk0 + h)),
            pl.BlockSpec((GLA_TC, v_w), lambda b, h, t: (tok(b, h, t), v_blk0 + h)),
            pl.BlockSpec((GLA_TC, v_w), lambda b, h, t: (tok(b, h, t), r_blk0 + h)),
            pl.BlockSpec((GLA_TC, qk_w), lambda b, h, t: (tok(b, h, t), h)),
            pl.BlockSpec((1, GLA_DV), lambda b, h, t: (0, 0)),
        ],
        out_specs=pl.BlockSpec((GLA_TC, v_w), lambda b, h, t: (tok(b, h, t), h)),
        out_shape=jax.ShapeDtypeStruct((n, GLA_WIDTH), F32),
        scratch_shapes=[pltpu.VMEM((GLA_G, GLA_DK, GLA_DV), F32),
                        pltpu.VMEM((GLA_G, GLA_TC // GLA_CHUNK, GLA_DK, GLA_DV), F32)],
        compiler_params=_params("parallel", "parallel", "arbitrary"),
        name="gla",
    )(u, u, u, u, la, g_out.reshape(1, -1))


def _moba_kernel(qt_ref, kb_ref, v3_ref, kmean_ref, o_ref, kaug_ref, snext_ref):
    ti = pl.program_id(2)
    last_tile = pl.num_programs(2) - 1
    bs = MOBA_BLOCK
    tq = MOBA_TQ
    nb = kb_ref.shape[0] // bs
    dh = MOBA_DH
    vr = MOBA_VROWS
    scale = (dh ** -0.5) * LOG2_E

    @pl.when(ti == 0)
    def _():
        seq = kb_ref.shape[0]
        row_blk = lax.broadcasted_iota(jnp.int32, (seq, MXU_DIM - dh), 0) // bs
        lane = lax.broadcasted_iota(jnp.int32, (seq, MXU_DIM - dh), 1)
        onehot = (row_blk == lane).astype(BF16)
        for g in range(MOBA_G):
            kaug_ref[g, :, :dh] = kb_ref[:, g * dh:(g + 1) * dh]
            kaug_ref[g, :, dh:] = onehot

    blk = lax.broadcasted_iota(jnp.int32, (nb, tq), 0)
    q_blk = 2 * ti + (lax.broadcasted_iota(jnp.int32, (nb, tq), 1) // bs)
    past = blk < q_blk
    causal = (lax.broadcasted_iota(jnp.int32, (bs, bs), 0)
              <= lax.broadcasted_iota(jnp.int32, (bs, bs), 1))
    diag = pl.multiple_of(ti * tq, tq)

    def pv_dot(g, t, p):
        return (_dot(v3_ref[2 * t, g * vr:(g + 1) * vr, :], p[:bs])
                + _dot(v3_ref[2 * t + 1, g * vr:(g + 1) * vr, :], p[bs:]))

    q_aug, biases, state = [], [], []

    def score(g, t):
        off = pl.multiple_of(t * tq, tq)
        s = _dot(kaug_ref[g, pl.ds(off, tq), :], q_aug[g])
        return s, jnp.max(s, axis=0, keepdims=True)

    for g in range(MOBA_G):
        qt = qt_ref[g * dh:(g + 1) * dh, :]
        gate = jnp.dot(kmean_ref[:, g * dh:(g + 1) * dh], qt,
                       precision=lax.Precision.HIGHEST, preferred_element_type=F32)
        gate = jnp.where(past, gate, NEG_INF)
        rank = jnp.zeros((nb, tq), jnp.int32)
        for j in range(nb):
            gj = gate[j:j + 1, :]
            ahead = jnp.logical_or(gj > gate, jnp.logical_and(gj == gate, blk > j))
            rank = rank + ahead.astype(jnp.int32)
        chosen = jnp.logical_and(past, rank < MOBA_TOPK)
        bias = jnp.where(chosen, 0.0, NEG_INF)
        qa = jnp.concatenate([(qt * scale).astype(BF16), bias.astype(BF16),
                              jnp.zeros((MXU_DIM - dh - nb, tq), BF16)], axis=0)
        q_aug.append(qa)
        biases.append(bias)

    own_scores, first_maxes = [], []
    for g in range(MOBA_G):
        qa, bias = q_aug[g], biases[g]
        first_bias = jnp.sum(jnp.where(blk == 2 * ti, bias, 0.0), axis=0, keepdims=True)[:, bs:]
        s0 = _dot(kaug_ref[g, pl.ds(diag, bs), :dh], qa[:dh])
        s00 = jnp.where(causal, s0[:, :bs], NEG_INF)
        s01 = s0[:, bs:] + first_bias
        s11 = jnp.where(causal, _dot(kaug_ref[g, pl.ds(pl.multiple_of(diag + bs, bs), bs), :dh],
                                     qa[:dh, bs:]), NEG_INF)
        own_scores.append((s00, s01, s11))
    for g in range(MOBA_G):
        s_first, first_max = score(g, 0)
        snext_ref[g] = s_first
        first_maxes.append(first_max)
    for g in range(MOBA_G):
        s00, s01, s11 = own_scores[g]
        m0 = jnp.max(s00, axis=0, keepdims=True)
        m1 = jnp.maximum(jnp.max(s01, axis=0, keepdims=True), jnp.max(s11, axis=0, keepdims=True))
        p0 = jnp.concatenate([jnp.exp2(s00 - m0), jnp.exp2(s01 - m1)], axis=1).astype(BF16)
        p11 = jnp.exp2(s11 - m1).astype(BF16)
        acc0 = _dot(v3_ref[2 * ti, g * vr:(g + 1) * vr, :], p0)
        acc1 = _dot(v3_ref[2 * ti + 1, g * vr:(g + 1) * vr, :], p11)
        acc = jnp.concatenate([acc0[:, :bs], acc0[:, bs:] + acc1], axis=1)
        state.append((jnp.concatenate([m0, m1], axis=1), acc, first_maxes[g]))

    def consume(g, t, s, tile_max, m, acc, valid=None):
        if valid is not None:
            tile_max = jnp.where(valid, tile_max, NEG_INF)
        m_new = jnp.maximum(m, tile_max)
        alpha = jnp.exp2(m - m_new)
        shift = m_new if valid is None else jnp.where(valid, m_new, -NEG_INF)
        p = jnp.exp2(s - shift).astype(BF16)
        return m_new, alpha * acc + pv_dot(g, t, p)

    def body(u, carry):
        out = []
        for g in range(MOBA_G):
            m, acc, max_a = carry[g]
            s_b, max_b = score(g, 2 * u + 1)
            m, acc = consume(g, 2 * u, snext_ref[g], max_a, m, acc)
            s_c, max_c = score(g, jnp.minimum(2 * u + 2, last_tile))
            snext_ref[g] = s_c
            m, acc = consume(g, 2 * u + 1, s_b, max_b, m, acc, valid=2 * u + 1 < ti)
            out.append((m, acc, max_c))
        return tuple(out)

    carry = lax.fori_loop(0, (ti + 1) // 2, body, tuple(state))
    for g in range(MOBA_G):
        m, acc, _ = carry[g]
        o_ref[:, g * dh:(g + 1) * dh] = (acc[:dh] * (1.0 / acc[dh:dh + 1])).T


def _moba(q_t, kb, kmean, v3, batch, seq):
    n = kb.shape[0]
    nb = seq // MOBA_BLOCK
    nt = seq // MOBA_TQ
    gw = MOBA_G * MOBA_DH
    return pl.pallas_call(
        _moba_kernel,
        grid=(batch, MOBA_HEADS // MOBA_G, nt),
        in_specs=[
            pl.BlockSpec((gw, MOBA_TQ), lambda b, h, i: (h, b * nt + i)),
            pl.BlockSpec((seq, gw), lambda b, h, i: (b, h)),
            pl.BlockSpec((nb, MOBA_G * MOBA_VROWS, MOBA_BLOCK), lambda b, h, i: (b, h, 0)),
            pl.BlockSpec((nb, gw), lambda b, h, i: (b, h)),
        ],
        out_specs=pl.BlockSpec((MOBA_TQ, gw), lambda b, h, i: (b * nt + i, h)),
        out_shape=jax.ShapeDtypeStruct((n, MOBA_WIDTH), F32),
        scratch_shapes=[pltpu.VMEM((MOBA_G, seq, MXU_DIM), BF16),
                        pltpu.VMEM((MOBA_G, MOBA_TQ, MOBA_TQ), F32)],
        compiler_params=_params("parallel", "parallel", "arbitrary"),
        name="moba",
    )(q_t, kb, v3, kmean)


def _memkv_kernel(mem_ref, g_ref, wkv_ref, gk_ref, k_ref, v_ref):
    h = _rms(mem_ref[...], g_ref[...]).astype(BF16)
    kv = _dot(h, wkv_ref[...].astype(BF16))
    gk = gk_ref[...]
    ks = [_rms(kv[:, i * XATTN_DH:(i + 1) * XATTN_DH], gk) for i in range(XATTN_HEADS)]
    k_ref[...] = jnp.concatenate(ks, axis=1)
    v_ref[...] = kv[:, XATTN_WIDTH:]


def _memkv(mem, g, wkv_b, gk):
    m, d = mem.shape
    return pl.pallas_call(
        _memkv_kernel,
        out_shape=[jax.ShapeDtypeStruct((m, XATTN_WIDTH), F32),
                   jax.ShapeDtypeStruct((m, XATTN_WIDTH), F32)],
        compiler_params=pltpu.CompilerParams(vmem_limit_bytes=VMEM_LIMIT),
        name="memkv",
    )(mem, g.reshape(1, d), wkv_b, gk.reshape(1, -1))


def _load_bf16_weights(pairs, stage_ref, sem):
    chunks = [(src, dst, r, min(OUTX_WCHUNK, dst.shape[0] - r))
              for src, dst in pairs for r in range(0, dst.shape[0], OUTX_WCHUNK)]

    def copy(k):
        src, dst, r, rows = chunks[k]
        return pltpu.make_async_copy(src.at[pl.ds(r, rows), :],
                                     stage_ref.at[k % 2, :rows, :dst.shape[1]], sem.at[k % 2])

    copy(0).start()
    for k, (_, dst, r, rows) in enumerate(chunks):
        if k + 1 < len(chunks):
            copy(k + 1).start()
        copy(k).wait()
        dst[r:r + rows, :] = stage_ref[k % 2, :rows, :dst.shape[1]].astype(BF16)


def _outx_kernel(x1_ref, og_ref, om_ref, wout_hbm, gx_ref, wq_hbm, gq_ref, kx_ref, vx_ref, wo_hbm,
                 o_ref, wout_ref, wq_ref, wo_ref, stage_ref, wsem):
    @pl.when(pl.program_id(0) == 0)
    def _():
        _load_bf16_weights(((wout_hbm, wout_ref), (wq_hbm, wq_ref), (wo_hbm, wo_ref)), stage_ref, wsem)

    x2 = (x1_ref[...]
          + _dot(og_ref[...].astype(BF16), wout_ref[:GLA_WIDTH, :])
          + _dot(om_ref[...].astype(BF16), wout_ref[GLA_WIDTH:, :]))
    h = _rms(x2, gx_ref[...]).astype(BF16)
    q = _dot(h, wq_ref[...])
    gq = gq_ref[...]
    scale = (XATTN_DH ** -0.5) * LOG2_E
    cols = [slice(i * XATTN_DH, (i + 1) * XATTN_DH) for i in range(XATTN_HEADS)]
    qh = [(_rms(q[:, c], gq) * scale).astype(BF16) for c in cols]
    ss = [_dot_nt(qh[i], kx_ref[:, c].astype(BF16)) for i, c in enumerate(cols)]
    ps = []
    for s in ss:
        p = jnp.exp2(s - jnp.max(s, axis=-1, keepdims=True))
        ps.append((p * (1.0 / jnp.sum(p, axis=-1, keepdims=True))).astype(BF16))
    heads = [_dot(ps[i], vx_ref[:, c].astype(BF16)) for i, c in enumerate(cols)]
    o = jnp.concatenate(heads, axis=1).astype(BF16)
    o_ref[...] = x2 + _dot(o, wo_ref[...])


def _outx(x1, og, om, w_out, gx, w_q, gq, kx, vx, w_o, seq, mem_len):
    n, d = x1.shape
    any_spec = pl.BlockSpec(memory_space=pl.ANY)
    tiles_per_seq = seq // OUTX_TM
    const = lambda i: (0, 0)
    return pl.pallas_call(
        _outx_kernel,
        grid=(n // OUTX_TM,),
        in_specs=[
            pl.BlockSpec((OUTX_TM, d), lambda i: (i, 0)),
            pl.BlockSpec((OUTX_TM, GLA_WIDTH), lambda i: (i, 0)),
            pl.BlockSpec((OUTX_TM, MOBA_WIDTH), lambda i: (i, 0)),
            any_spec,
            pl.BlockSpec((1, d), const),
            any_spec,
            pl.BlockSpec((1, XATTN_DH), const),
            pl.BlockSpec((mem_len, XATTN_WIDTH), lambda i: (i // tiles_per_seq, 0)),
            pl.BlockSpec((mem_len, XATTN_WIDTH), lambda i: (i // tiles_per_seq, 0)),
            any_spec,
        ],
        out_specs=pl.BlockSpec((OUTX_TM, d), lambda i: (i, 0)),
        out_shape=jax.ShapeDtypeStruct((n, d), F32),
        scratch_shapes=[pltpu.VMEM(w_out.shape, BF16), pltpu.VMEM(w_q.shape, BF16), pltpu.VMEM(w_o.shape, BF16),
                        pltpu.VMEM((2, OUTX_WCHUNK, d), F32), pltpu.SemaphoreType.DMA((2,))],
        compiler_params=_params("arbitrary"),
        name="outx",
    )(x1, og, om, w_out, gx.reshape(1, d), w_q, gq.reshape(1, -1), kx, vx, w_o)


def kernel(x, mem, ffn1_norm, ffn1_w_gate, ffn1_w_up, ffn1_w_down, mix_norm, w_in, gla_w_gate2, gla_b_gate2, gla_out_norm, moba_q_norm, moba_k_norm, w_out, xattn_norm, mem_norm, xattn_w_q, xattn_w_kv, xattn_w_o, xattn_q_norm, xattn_k_norm, ffn2_norm, ffn2_w_gate, ffn2_w_up, ffn2_w_down):
    batch, seq, d = x.shape
    mem_len = mem.shape[1]
    depth = ffn1_norm.shape[0]
    n = batch * seq
    assert seq % GLA_TC == 0 and seq % (2 * MOBA_BLOCK) == 0 and seq % OUTX_TM == 0
    assert seq // MOBA_BLOCK + MOBA_DH <= MXU_DIM and (seq // MOBA_BLOCK) % BF16_SUBLANES == 0
    assert n % FFN_TM == 0 and n % NORM_TM == 0 and n % PROJ_TM == 0 and PROJ_TM % MOBA_BLOCK == 0

    xf = x.reshape(n, d)
    memf = mem.reshape(batch * mem_len, d)
    lr0 = GLA_COLS
    mq0 = lr0 + GLA_GATE_RANK
    w_in_t = jnp.swapaxes(w_in, 1, 2)
    for l in range(depth):
        w_lr_t = w_in_t[l, lr0:mq0]

        x1 = _ffn(xf, ffn1_norm[l], ffn1_w_gate[l], ffn1_w_up[l], ffn1_w_down[l])
        hn, la = _mixnorm(x1, mix_norm[l], w_lr_t, gla_w_gate2[l], gla_b_gate2[l])
        u, q_t, kb, kmean, v3 = _proj(hn, w_in_t, l, moba_q_norm[l], moba_k_norm[l])
        o_gla = _gla(u, la, gla_out_norm[l], batch, seq)
        o_moba = _moba(q_t, kb, kmean, v3, batch, seq)
        kx, vx = _memkv(memf, mem_norm[l], xattn_w_kv[l], xattn_k_norm[l])
        x3 = _outx(x1, o_gla, o_moba, w_out[l], xattn_norm[l], xattn_w_q[l], xattn_q_norm[l], kx, vx,
                   xattn_w_o[l], seq, mem_len)
        xf = _ffn(x3, ffn2_norm[l], ffn2_w_gate[l], ffn2_w_up[l], ffn2_w_down[l])
    return xf.reshape(batch, seq, d)
```

```python
import jax
import jax.numpy as jnp
from jax import lax
from jax.experimental import pallas as pl
from jax.experimental.pallas import tpu as pltpu

F32 = jnp.float32
BF16 = jnp.bfloat16

EPS = 1e-6
NEG_INF = -1e30
LOG2_E = 1.4426950408889634
D_FF = 5632
GLA_HEADS = 4
GLA_DV = 256
GLA_DK = 128
GLA_QK = GLA_HEADS * GLA_DK
GLA_WIDTH = GLA_HEADS * GLA_DV
GLA_COLS = 2 * GLA_QK + 2 * GLA_WIDTH
GLA_GATE_RANK = 16
GLA_GATE_TAU = 16.0
GLA_CHUNK = 64
MOBA_DH = 128
MOBA_HEADS = 8
MOBA_WIDTH = MOBA_HEADS * MOBA_DH
MOBA_BLOCK = 256
MOBA_TOPK = 3
XATTN_HEADS = 4
XATTN_DH = 128
XATTN_WIDTH = XATTN_HEADS * XATTN_DH

V7X_VMEM_BYTES = 64 * 1024 * 1024
VMEM_LIMIT = V7X_VMEM_BYTES - 6 * 1024 * 1024
FFN_VMEM_LIMIT = V7X_VMEM_BYTES - 2 * 1024 * 1024
MXU_DIM = 256

FFN_TM = 1024
FFN_TF = 512
NORM_TM = 1024
PROJ_TM = 2048
PROJ_TN = 512
GLA_TC = 512
GLA_SUB = 256
MOBA_G = 4
MOBA_TQ = 2 * MOBA_BLOCK
BF16_SUBLANES = 16
MOBA_VPAD = BF16_SUBLANES
MOBA_VROWS = MOBA_DH + MOBA_VPAD
OUTX_TM = 512
OUTX_WCHUNK = 512


def _params(*semantics):
    return pltpu.CompilerParams(dimension_semantics=semantics, vmem_limit_bytes=VMEM_LIMIT)


def _rms(x, g):
    return x * lax.rsqrt(jnp.mean(x * x, axis=-1, keepdims=True) + EPS) * g


def _dot(a, b):
    return jnp.dot(a, b, preferred_element_type=F32)


def _dot_nt(a, b):
    return lax.dot_general(a, b, (((1,), (1,)), ((), ())), preferred_element_type=F32)


def _silu(x):
    return x * (1.0 / (1.0 + jnp.exp(-x)))


def _ffn_kernel(x_hbm, g_ref, wg_ref, wu_ref, wd_ref, o_hbm, h_ref, xbuf_ref, acc_ref, xsem, osem):
    i = pl.program_id(0)
    f = pl.program_id(1)
    last_i = pl.num_programs(0) - 1
    last_f = pl.num_programs(1) - 1

    def x_copy(tile):
        return pltpu.make_async_copy(x_hbm.at[pl.ds(tile * FFN_TM, FFN_TM)], xbuf_ref, xsem)

    def o_copy(tile):
        return pltpu.make_async_copy(acc_ref, o_hbm.at[pl.ds(tile * FFN_TM, FFN_TM)], osem)

    @pl.when(jnp.logical_and(i == 0, f == 0))
    def _():
        x_copy(0).start()

    @pl.when(f == 0)
    def _():
        x_copy(i).wait()
        x = xbuf_ref[...]
        h_ref[...] = _rms(x, g_ref[...]).astype(BF16)

        @pl.when(i > 0)
        def _():
            o_copy(i - 1).wait()

        acc_ref[...] = xbuf_ref[...]

    @pl.when(jnp.logical_and(f == 1, i < last_i))
    def _():
        x_copy(i + 1).start()

    h = h_ref[...]
    gate = _dot(h, wg_ref[...].astype(BF16))
    up = _dot(h, wu_ref[...].astype(BF16))
    act = (0.5 * _silu(gate) * up).astype(BF16)
    acc_ref[...] += _dot(act, wd_ref[...].astype(BF16))

    @pl.when(f == last_f)
    def _():
        o_copy(i).start()

        @pl.when(i == last_i)
        def _():
            o_copy(i).wait()


def _ffn(x, g, wg, wu, wd):
    n, d = x.shape
    grid = (n // FFN_TM, D_FF // FFN_TF)
    assert grid[1] >= 2
    return pl.pallas_call(
        _ffn_kernel,
        grid=grid,
        in_specs=[
            pl.BlockSpec(memory_space=pl.ANY),
            pl.BlockSpec((1, d), lambda i, f: (0, 0)),
            pl.BlockSpec((d, FFN_TF), lambda i, f: (0, f)),
            pl.BlockSpec((d, FFN_TF), lambda i, f: (0, f)),
            pl.BlockSpec((FFN_TF, d), lambda i, f: (f, 0)),
        ],
        out_specs=pl.BlockSpec(memory_space=pl.ANY),
        out_shape=jax.ShapeDtypeStruct((n, d), F32),
        scratch_shapes=[pltpu.VMEM((FFN_TM, d), BF16), pltpu.VMEM((FFN_TM, d), F32),
                        pltpu.VMEM((FFN_TM, d), F32),
                        pltpu.SemaphoreType.DMA, pltpu.SemaphoreType.DMA],
        compiler_params=pltpu.CompilerParams(dimension_semantics=("arbitrary", "arbitrary"),
                                             vmem_limit_bytes=FFN_VMEM_LIMIT),
        name="ffn",
    )(x, g.reshape(1, d), wg, wu, wd)


def _mixnorm_kernel(x_ref, g_ref, wlr_ref, wg2_ref, bg2_ref, hn_ref, la_ref):
    h = _rms(x_ref[...], g_ref[...]).astype(BF16)
    hn_ref[...] = h
    lr = _dot_nt(h, wlr_ref[...].astype(BF16))
    pre = _dot(lr.astype(BF16), wg2_ref[...].astype(BF16)) + bg2_ref[...]
    log_sig = -(jnp.maximum(-pre, 0.0) + jnp.log(1.0 + jnp.exp(-jnp.abs(pre))))
    la_ref[...] = log_sig * (1.0 / GLA_GATE_TAU)


def _mixnorm(x1, g, w_lr_t, wg2, bg2):
    n, d = x1.shape
    const = lambda i: (0, 0)
    return pl.pallas_call(
        _mixnorm_kernel,
        grid=(n // NORM_TM,),
        in_specs=[
            pl.BlockSpec((NORM_TM, d), lambda i: (i, 0)),
            pl.BlockSpec((1, d), const),
            pl.BlockSpec((GLA_GATE_RANK, d), const),
            pl.BlockSpec((GLA_GATE_RANK, GLA_QK), const),
            pl.BlockSpec((1, GLA_QK), const),
        ],
        out_specs=[pl.BlockSpec((NORM_TM, d), lambda i: (i, 0)),
                   pl.BlockSpec((NORM_TM, GLA_QK), lambda i: (i, 0))],
        out_shape=[jax.ShapeDtypeStruct((n, d), BF16),
                   jax.ShapeDtypeStruct((n, GLA_QK), F32)],
        compiler_params=_params("parallel"),
        name="mixnorm",
    )(x1, g.reshape(1, d), w_lr_t, wg2, bg2.reshape(1, -1))


def _projg_kernel(hn_ref, wt_ref, u_ref):
    u_ref[...] = _dot_nt(hn_ref[...], wt_ref[0].astype(BF16))


def _projq_kernel(hn_ref, wt_ref, g_ref, qt_ref):
    yt = _dot_nt(wt_ref[0].astype(BF16), hn_ref[...])
    g = g_ref[...]
    parts = []
    for k in range(PROJ_TN // MOBA_DH):
        p = yt[k * MOBA_DH:(k + 1) * MOBA_DH, :]
        inv = lax.rsqrt(jnp.mean(p * p, axis=0, keepdims=True) + EPS)
        parts.append(p * inv * g)
    qt_ref[...] = jnp.concatenate(parts, axis=0)


def _projk_kernel(hn_ref, wt_ref, g_ref, kb_ref, kmean_ref):
    y = _dot_nt(hn_ref[...], wt_ref[0].astype(BF16))
    g = g_ref[...]
    parts = [_rms(y[:, k * MOBA_DH:(k + 1) * MOBA_DH], g) for k in range(PROJ_TN // MOBA_DH)]
    kn = jnp.concatenate(parts, axis=1)
    kb_ref[...] = kn.astype(BF16)
    blocks = PROJ_TM // MOBA_BLOCK
    kmean_ref[0] = jnp.sum(kn.reshape(blocks, MOBA_BLOCK, PROJ_TN), axis=1) * (1.0 / MOBA_BLOCK)


def _projv_kernel(hn_ref, wt_ref, v3_ref):
    yt = _dot_nt(wt_ref[0].astype(BF16), hn_ref[...]).astype(BF16)
    ones = jnp.ones((MOBA_VPAD, PROJ_TM), BF16)
    parts = []
    for k in range(PROJ_TN // MOBA_DH):
        parts += [yt[k * MOBA_DH:(k + 1) * MOBA_DH], ones]
    ya = jnp.concatenate(parts, axis=0)
    for t in range(PROJ_TM // MOBA_BLOCK):
        v3_ref[t] = ya[:, t * MOBA_BLOCK:(t + 1) * MOBA_BLOCK]


def _proj_kernel(hn_ref, wt_ref, gq_ref, gk_ref, u_ref, qt_ref, kb_ref, kmean_ref, v3_ref):
    j = pl.program_id(1)
    gla = GLA_COLS // PROJ_TN
    steps = MOBA_WIDTH // PROJ_TN

    @pl.when(j < gla)
    def _():
        _projg_kernel(hn_ref, wt_ref, u_ref)

    @pl.when(jnp.logical_and(j >= gla, j < gla + steps))
    def _():
        _projq_kernel(hn_ref, wt_ref, gq_ref, qt_ref)

    @pl.when(jnp.logical_and(j >= gla + steps, j < gla + 2 * steps))
    def _():
        _projk_kernel(hn_ref, wt_ref, gk_ref, kb_ref, kmean_ref)

    @pl.when(j >= gla + 2 * steps)
    def _():
        _projv_kernel(hn_ref, wt_ref, v3_ref)


def _proj(hn, w_in_t, layer, gq, gk):
    n, d = hn.shape
    gla = GLA_COLS // PROJ_TN
    steps = MOBA_WIDTH // PROJ_TN
    blocks = PROJ_TM // MOBA_BLOCK
    moba_gap = GLA_GATE_RANK
    own = lambda j, first: jnp.clip(j - gla - first * steps, 0, steps - 1)
    const = lambda i, j: (0, 0)
    u, q_t, kb, kmean, v3 = pl.pallas_call(
        _proj_kernel,
        grid=(n // PROJ_TM, gla + 3 * steps),
        in_specs=[
            pl.BlockSpec((PROJ_TM, d), lambda i, j: (i, 0)),
            pl.BlockSpec((pl.Element(1), pl.Element(PROJ_TN), pl.Element(d)),
                         lambda i, j: (layer,
                                       pl.multiple_of(j * PROJ_TN + jnp.where(j >= gla, moba_gap, 0), 8), 0)),
            pl.BlockSpec((MOBA_DH, 1), const),
            pl.BlockSpec((1, MOBA_DH), const),
        ],
        out_specs=[
            pl.BlockSpec((PROJ_TM, PROJ_TN), lambda i, j: (i, jnp.minimum(j, gla - 1))),
            pl.BlockSpec((PROJ_TN, PROJ_TM), lambda i, j: (own(j, 0), i)),
            pl.BlockSpec((PROJ_TM, PROJ_TN), lambda i, j: (i, own(j, 1))),
            pl.BlockSpec((1, blocks, PROJ_TN), lambda i, j: (i, 0, own(j, 1))),
            pl.BlockSpec((blocks, PROJ_TN // MOBA_DH * MOBA_VROWS, MOBA_BLOCK),
                         lambda i, j: (i, own(j, 2), 0)),
        ],
        out_shape=[jax.ShapeDtypeStruct((n, GLA_COLS), F32),
                   jax.ShapeDtypeStruct((MOBA_WIDTH, n), F32),
                   jax.ShapeDtypeStruct((n, MOBA_WIDTH), BF16),
                   jax.ShapeDtypeStruct((n // PROJ_TM, blocks, MOBA_WIDTH), F32),
                   jax.ShapeDtypeStruct((n // MOBA_BLOCK, MOBA_HEADS * MOBA_VROWS, MOBA_BLOCK), BF16)],
        compiler_params=_params("arbitrary", "arbitrary"),
        name="proj",
    )(hn, w_in_t, gq.reshape(-1, 1), gk.reshape(1, -1))
    return u, q_t, kb, kmean.reshape(n // MOBA_BLOCK, MOBA_WIDTH), v3


def _dot_exact_rhs(m, a):
    a1 = a.astype(BF16)
    r1 = a - a1.astype(F32)
    a2 = r1.astype(BF16)
    a3 = (r1 - a2.astype(F32)).astype(BF16)
    return _dot(m, a1) + _dot(m, a2) + _dot(m, a3)


def _gla_head(q, k, v, r, la, g_out, state, upd_ref):
    tc = GLA_TC
    nchunk = tc // GLA_CHUNK
    sub = GLA_SUB

    crow = lax.broadcasted_iota(jnp.int32, (GLA_CHUNK, GLA_CHUNK), 0)
    ccol = lax.broadcasted_iota(jnp.int32, (GLA_CHUNK, GLA_CHUNK), 1)
    tri = (ccol <= crow).astype(BF16)
    la_wide = jnp.concatenate([la[c * GLA_CHUNK:(c + 1) * GLA_CHUNK] for c in range(nchunk)], axis=1)
    bc_wide = _dot_exact_rhs(tri, la_wide)
    bcum = jnp.concatenate([bc_wide[:, c * GLA_DK:(c + 1) * GLA_DK] for c in range(nchunk)], axis=0)
    btot = jnp.concatenate(
        [jnp.broadcast_to(bc_wide[GLA_CHUNK - 1:GLA_CHUNK, c * GLA_DK:(c + 1) * GLA_DK],
                          (GLA_CHUNK, GLA_DK)) for c in range(nchunk)], axis=0)

    q_dec_b = ((q * (GLA_DK ** -0.5)) * jnp.exp(bcum)).astype(BF16)
    k_inv_b = (k * jnp.exp(-bcum)).astype(BF16)
    k_tail_t = (k * jnp.exp(btot - bcum)).T.astype(BF16)
    decay_t = jnp.exp(btot).T
    vb = v.astype(BF16)

    row = lax.broadcasted_iota(jnp.int32, (sub, sub), 0)
    col = lax.broadcasted_iota(jnp.int32, (sub, sub), 1)
    causal = jnp.logical_and((row // GLA_CHUNK) == (col // GLA_CHUNK), col <= row)
    intra = []
    for p in range(tc // sub):
        rows = slice(p * sub, (p + 1) * sub)
        att = jnp.where(causal, _dot_nt(q_dec_b[rows], k_inv_b[rows]), 0.0)
        intra.append(_dot(att.astype(BF16), vb[rows]))

    for c in range(nchunk):
        rows = slice(c * GLA_CHUNK, (c + 1) * GLA_CHUNK)
        upd_ref[c] = _dot(k_tail_t[:, rows], vb[rows])

    inter = []
    for c in range(nchunk):
        rows = slice(c * GLA_CHUNK, (c + 1) * GLA_CHUNK)
        inter.append(_dot(q_dec_b[rows], state.astype(BF16)))
        dec_c = decay_t[:, c * GLA_CHUNK:c * GLA_CHUNK + 1]
        state = dec_c * state + upd_ref[c]
    o = jnp.concatenate(intra, axis=0) + jnp.concatenate(inter, axis=0)
    return _rms(o, g_out) * _silu(r), state


def _gla_kernel(u_ref, la_ref, go_ref, o_ref, s_ref, upd_ref):
    @pl.when(pl.program_id(1) == 0)
    def _():
        s_ref[...] = jnp.zeros_like(s_ref)

    k0, v0, r0 = GLA_QK, 2 * GLA_QK, 2 * GLA_QK + GLA_WIDTH
    for g in range(GLA_HEADS):
        qk = slice(g * GLA_DK, (g + 1) * GLA_DK)
        vr = slice(g * GLA_DV, (g + 1) * GLA_DV)
        o, state = _gla_head(u_ref[:, qk], u_ref[:, k0 + g * GLA_DK:k0 + (g + 1) * GLA_DK],
                             u_ref[:, v0 + g * GLA_DV:v0 + (g + 1) * GLA_DV],
                             u_ref[:, r0 + g * GLA_DV:r0 + (g + 1) * GLA_DV], la_ref[:, qk],
                             go_ref[...], s_ref[g], upd_ref.at[g])
        s_ref[g] = state
        o_ref[:, vr] = o


def _gla(u, la, g_out, batch, seq):
    n = u.shape[0]
    nt = seq // GLA_TC
    tok = lambda b, t: (b * nt + t, 0)
    return pl.pallas_call(
        _gla_kernel,
        grid=(batch, nt),
        in_specs=[
            pl.BlockSpec((GLA_TC, GLA_COLS), tok),
            pl.BlockSpec((GLA_TC, GLA_QK), tok),
            pl.BlockSpec((1, GLA_DV), lambda b, t: (0, 0)),
        ],
        out_specs=pl.BlockSpec((GLA_TC, GLA_WIDTH), tok),
        out_shape=jax.ShapeDtypeStruct((n, GLA_WIDTH), F32),
        scratch_shapes=[pltpu.VMEM((GLA_HEADS, GLA_DK, GLA_DV), F32),
                        pltpu.VMEM((GLA_HEADS, GLA_TC // GLA_CHUNK, GLA_DK, GLA_DV), F32)],
        compiler_params=_params("parallel", "arbitrary"),
        name="gla",
    )(u, la, g_out.reshape(1, -1))


def _moba_kernel(qt_ref, kb_ref, v3_ref, kmean_ref, o_ref, kaug_ref, snext_ref):
    ti = pl.program_id(2)
    last_tile = pl.num_programs(2) - 1
    bs = MOBA_BLOCK
    tq = MOBA_TQ
    nb = kb_ref.shape[0] // bs
    dh = MOBA_DH
    vr = MOBA_VROWS
    scale = (dh ** -0.5) * LOG2_E

    @pl.when(ti == 0)
    def _():
        seq = kb_ref.shape[0]
        row_blk = lax.broadcasted_iota(jnp.int32, (seq, MXU_DIM - dh), 0) // bs
        lane = lax.broadcasted_iota(jnp.int32, (seq, MXU_DIM - dh), 1)
        onehot = (row_blk == lane).astype(BF16)
        for g in range(MOBA_G):
            kaug_ref[g, :, :dh] = kb_ref[:, g * dh:(g + 1) * dh]
            kaug_ref[g, :, dh:] = onehot

    blk = lax.broadcasted_iota(jnp.int32, (nb, tq), 0)
    q_blk = 2 * ti + (lax.broadcasted_iota(jnp.int32, (nb, tq), 1) // bs)
    past = blk < q_blk
    causal = (lax.broadcasted_iota(jnp.int32, (bs, bs), 0)
              <= lax.broadcasted_iota(jnp.int32, (bs, bs), 1))
    diag = pl.multiple_of(ti * tq, tq)

    def pv_dot(g, t, p):
        return (_dot(v3_ref[2 * t, g * vr:(g + 1) * vr, :], p[:bs])
                + _dot(v3_ref[2 * t + 1, g * vr:(g + 1) * vr, :], p[bs:]))

    q_aug, biases, state = [], [], []

    def score(g, t):
        off = pl.multiple_of(t * tq, tq)
        s = _dot(kaug_ref[g, pl.ds(off, tq), :], q_aug[g])
        return s, jnp.max(s, axis=0, keepdims=True)

    for g in range(MOBA_G):
        qt = qt_ref[g * dh:(g + 1) * dh, :]
        gate = jnp.dot(kmean_ref[:, g * dh:(g + 1) * dh], qt,
                       precision=lax.Precision.HIGHEST, preferred_element_type=F32)
        gate = jnp.where(past, gate, NEG_INF)
        rank = jnp.zeros((nb, tq), jnp.int32)
        for j in range(nb):
            gj = gate[j:j + 1, :]
            ahead = jnp.logical_or(gj > gate, jnp.logical_and(gj == gate, blk > j))
            rank = rank + ahead.astype(jnp.int32)
        chosen = jnp.logical_and(past, rank < MOBA_TOPK)
        bias = jnp.where(chosen, 0.0, NEG_INF)
        qa = jnp.concatenate([(qt * scale).astype(BF16), bias.astype(BF16),
                              jnp.zeros((MXU_DIM - dh - nb, tq), BF16)], axis=0)
        q_aug.append(qa)
        biases.append(bias)

    own_scores, first_maxes = [], []
    for g in range(MOBA_G):
        qa, bias = q_aug[g], biases[g]
        first_bias = jnp.sum(jnp.where(blk == 2 * ti, bias, 0.0), axis=0, keepdims=True)[:, bs:]
        s0 = _dot(kaug_ref[g, pl.ds(diag, bs), :dh], qa[:dh])
        s00 = jnp.where(causal, s0[:, :bs], NEG_INF)
        s01 = s0[:, bs:] + first_bias
        s11 = jnp.where(causal, _dot(kaug_ref[g, pl.ds(pl.multiple_of(diag + bs, bs), bs), :dh],
                                     qa[:dh, bs:]), NEG_INF)
        own_scores.append((s00, s01, s11))
    for g in range(MOBA_G):
        s_first, first_max = score(g, 0)
        snext_ref[g] = s_first
        first_maxes.append(first_max)
    for g in range(MOBA_G):
        s00, s01, s11 = own_scores[g]
        m0 = jnp.max(s00, axis=0, keepdims=True)
        m1 = jnp.maximum(jnp.max(s01, axis=0, keepdims=True), jnp.max(s11, axis=0, keepdims=True))
        p0 = jnp.concatenate([jnp.exp2(s00 - m0), jnp.exp2(s01 - m1)], axis=1).astype(BF16)
        p11 = jnp.exp2(s11 - m1).astype(BF16)
        acc0 = _dot(v3_ref[2 * ti, g * vr:(g + 1) * vr, :], p0)
        acc1 = _dot(v3_ref[2 * ti + 1, g * vr:(g + 1) * vr, :], p11)
        acc = jnp.concatenate([acc0[:, :bs], acc0[:, bs:] + acc1], axis=1)
        state.append((jnp.concatenate([m0, m1], axis=1), acc, first_maxes[g]))

    def consume(g, t, s, tile_max, m, acc, valid=None):
        if valid is not None:
            tile_max = jnp.where(valid, tile_max, NEG_INF)
        m_new = jnp.maximum(m, tile_max)
        alpha = jnp.exp2(m - m_new)
        shift = m_new if valid is None else jnp.where(valid, m_new, -NEG_INF)
        p = jnp.exp2(s - shift).astype(BF16)
        return m_new, alpha * acc + pv_dot(g, t, p)

    def body(u, carry):
        out = []
        for g in range(MOBA_G):
            m, acc, max_a = carry[g]
            s_b, max_b = score(g, 2 * u + 1)
            m, acc = consume(g, 2 * u, snext_ref[g], max_a, m, acc)
            s_c, max_c = score(g, jnp.minimum(2 * u + 2, last_tile))
            snext_ref[g] = s_c
            m, acc = consume(g, 2 * u + 1, s_b, max_b, m, acc, valid=2 * u + 1 < ti)
            out.append((m, acc, max_c))
        return tuple(out)

    carry = lax.fori_loop(0, (ti + 1) // 2, body, tuple(state))
    for g in range(MOBA_G):
        m, acc, _ = carry[g]
        o_ref[:, g * dh:(g + 1) * dh] = (acc[:dh] * (1.0 / acc[dh:dh + 1])).T


def _moba(q_t, kb, kmean, v3, batch, seq):
    n = kb.shape[0]
    nb = seq // MOBA_BLOCK
    nt = seq // MOBA_TQ
    gw = MOBA_G * MOBA_DH
    return pl.pallas_call(
        _moba_kernel,
        grid=(batch, MOBA_HEADS // MOBA_G, nt),
        in_specs=[
            pl.BlockSpec((gw, MOBA_TQ), lambda b, h, i: (h, b * nt + i)),
            pl.BlockSpec((seq, gw), lambda b, h, i: (b, h)),
            pl.BlockSpec((nb, MOBA_G * MOBA_VROWS, MOBA_BLOCK), lambda b, h, i: (b, h, 0)),
            pl.BlockSpec((nb, gw), lambda b, h, i: (b, h)),
        ],
        out_specs=pl.BlockSpec((MOBA_TQ, gw), lambda b, h, i: (b * nt + i, h)),
        out_shape=jax.ShapeDtypeStruct((n, MOBA_WIDTH), F32),
        scratch_shapes=[pltpu.VMEM((MOBA_G, seq, MXU_DIM), BF16),
                        pltpu.VMEM((MOBA_G, MOBA_TQ, MOBA_TQ), F32)],
        compiler_params=_params("parallel", "parallel", "arbitrary"),
        name="moba",
    )(q_t, kb, v3, kmean)


def _memkv_kernel(mem_ref, g_ref, wkv_ref, gk_ref, k_ref, v_ref):
    h = _rms(mem_ref[...], g_ref[...]).astype(BF16)
    kv = _dot(h, wkv_ref[...].astype(BF16))
    gk = gk_ref[...]
    ks = [_rms(kv[:, i * XATTN_DH:(i + 1) * XATTN_DH], gk) for i in range(XATTN_HEADS)]
    k_ref[...] = jnp.concatenate(ks, axis=1)
    v_ref[...] = kv[:, XATTN_WIDTH:]


def _memkv(mem, g, wkv_b, gk):
    m, d = mem.shape
    return pl.pallas_call(
        _memkv_kernel,
        out_shape=[jax.ShapeDtypeStruct((m, XATTN_WIDTH), F32),
                   jax.ShapeDtypeStruct((m, XATTN_WIDTH), F32)],
        compiler_params=pltpu.CompilerParams(vmem_limit_bytes=VMEM_LIMIT),
        name="memkv",
    )(mem, g.reshape(1, d), wkv_b, gk.reshape(1, -1))


def _load_bf16_weights(pairs, stage_ref, sem):
    chunks = [(src, dst, r, min(OUTX_WCHUNK, dst.shape[0] - r))
              for src, dst in pairs for r in range(0, dst.shape[0], OUTX_WCHUNK)]

    def copy(k):
        src, dst, r, rows = chunks[k]
        return pltpu.make_async_copy(src.at[pl.ds(r, rows), :],
                                     stage_ref.at[k % 2, :rows, :dst.shape[1]], sem.at[k % 2])

    copy(0).start()
    for k, (_, dst, r, rows) in enumerate(chunks):
        if k + 1 < len(chunks):
            copy(k + 1).start()
        copy(k).wait()
        dst[r:r + rows, :] = stage_ref[k % 2, :rows, :dst.shape[1]].astype(BF16)


def _outx_kernel(x1_ref, og_ref, om_ref, wout_hbm, gx_ref, wq_hbm, gq_ref, kx_ref, vx_ref, wo_hbm,
                 o_ref, wout_ref, wq_ref, wo_ref, stage_ref, wsem):
    @pl.when(pl.program_id(0) == 0)
    def _():
        _load_bf16_weights(((wout_hbm, wout_ref), (wq_hbm, wq_ref), (wo_hbm, wo_ref)), stage_ref, wsem)

    x2 = (x1_ref[...]
          + _dot(og_ref[...].astype(BF16), wout_ref[:GLA_WIDTH, :])
          + _dot(om_ref[...].astype(BF16), wout_ref[GLA_WIDTH:, :]))
    h = _rms(x2, gx_ref[...]).astype(BF16)
    q = _dot(h, wq_ref[...])
    gq = gq_ref[...]
    scale = (XATTN_DH ** -0.5) * LOG2_E
    cols = [slice(i * XATTN_DH, (i + 1) * XATTN_DH) for i in range(XATTN_HEADS)]
    qh = [(_rms(q[:, c], gq) * scale).astype(BF16) for c in cols]
    ss = [_dot_nt(qh[i], kx_ref[:, c].astype(BF16)) for i, c in enumerate(cols)]
    ps = []
    for s in ss:
        p = jnp.exp2(s - jnp.max(s, axis=-1, keepdims=True))
        ps.append((p * (1.0 / jnp.sum(p, axis=-1, keepdims=True))).astype(BF16))
    heads = [_dot(ps[i], vx_ref[:, c].astype(BF16)) for i, c in enumerate(cols)]
    o = jnp.concatenate(heads, axis=1).astype(BF16)
    o_ref[...] = x2 + _dot(o, wo_ref[...])


def _outx(x1, og, om, w_out, gx, w_q, gq, kx, vx, w_o, seq, mem_len):
    n, d = x1.shape
    any_spec = pl.BlockSpec(memory_space=pl.ANY)
    tiles_per_seq = seq // OUTX_TM
    const = lambda i: (0, 0)
    return pl.pallas_call(
        _outx_kernel,
        grid=(n // OUTX_TM,),
        in_specs=[
            pl.BlockSpec((OUTX_TM, d), lambda i: (i, 0)),
            pl.BlockSpec((OUTX_TM, GLA_WIDTH), lambda i: (i, 0)),
            pl.BlockSpec((OUTX_TM, MOBA_WIDTH), lambda i: (i, 0)),
            any_spec,
            pl.BlockSpec((1, d), const),
            any_spec,
            pl.BlockSpec((1, XATTN_DH), const),
            pl.BlockSpec((mem_len, XATTN_WIDTH), lambda i: (i // tiles_per_seq, 0)),
            pl.BlockSpec((mem_len, XATTN_WIDTH), lambda i: (i // tiles_per_seq, 0)),
            any_spec,
        ],
        out_specs=pl.BlockSpec((OUTX_TM, d), lambda i: (i, 0)),
        out_shape=jax.ShapeDtypeStruct((n, d), F32),
        scratch_shapes=[pltpu.VMEM(w_out.shape, BF16), pltpu.VMEM(w_q.shape, BF16), pltpu.VMEM(w_o.shape, BF16),
                        pltpu.VMEM((2, OUTX_WCHUNK, d), F32), pltpu.SemaphoreType.DMA((2,))],
        compiler_params=_params("arbitrary"),
        name="outx",
    )(x1, og, om, w_out, gx.reshape(1, d), w_q, gq.reshape(1, -1), kx, vx, w_o)


def kernel(x, mem, ffn1_norm, ffn1_w_gate, ffn1_w_up, ffn1_w_down, mix_norm, w_in, gla_w_gate2, gla_b_gate2, gla_out_norm, moba_q_norm, moba_k_norm, w_out, xattn_norm, mem_norm, xattn_w_q, xattn_w_kv, xattn_w_o, xattn_q_norm, xattn_k_norm, ffn2_norm, ffn2_w_gate, ffn2_w_up, ffn2_w_down):
    batch, seq, d = x.shape
    mem_len = mem.shape[1]
    depth = ffn1_norm.shape[0]
    n = batch * seq
    assert seq % GLA_TC == 0 and seq % (2 * MOBA_BLOCK) == 0 and seq % OUTX_TM == 0
    assert seq // MOBA_BLOCK + MOBA_DH <= MXU_DIM and (seq // MOBA_BLOCK) % BF16_SUBLANES == 0
    assert n % FFN_TM == 0 and n % NORM_TM == 0 and n % PROJ_TM == 0 and PROJ_TM % MOBA_BLOCK == 0

    xf = x.reshape(n, d)
    memf = mem.reshape(batch * mem_len, d)
    lr0 = GLA_COLS
    mq0 = lr0 + GLA_GATE_RANK
    w_in_t = jnp.swapaxes(w_in, 1, 2)
    for l in range(depth):
        w_lr_t = w_in_t[l, lr0:mq0]

        x1 = _ffn(xf, ffn1_norm[l], ffn1_w_gate[l], ffn1_w_up[l], ffn1_w_down[l])
        hn, la = _mixnorm(x1, mix_norm[l], w_lr_t, gla_w_gate2[l], gla_b_gate2[l])
        u, q_t, kb, kmean, v3 = _proj(hn, w_in_t, l, moba_q_norm[l], moba_k_norm[l])
        o_gla = _gla(u, la, gla_out_norm[l], batch, seq)
        o_moba = _moba(q_t, kb, kmean, v3, batch, seq)
        kx, vx = _memkv(memf, mem_norm[l], xattn_w_kv[l], xattn_k_norm[l])
        x3 = _outx(x1, o_gla, o_moba, w_out[l], xattn_norm[l], xattn_w_q[l], xattn_q_norm[l], kx, vx,
                   xattn_w_o[l], seq, mem_len)
        xf = _ffn(x3, ffn2_norm[l], ffn2_w_gate[l], ffn2_w_up[l], ffn2_w_down[l])
    return xf.reshape(batch, seq, d)
```

```python
import jax
import jax.numpy as jnp
from jax import lax
from jax.experimental import pallas as pl
from jax.experimental.pallas import tpu as pltpu

F32 = jnp.float32
BF16 = jnp.bfloat16

EPS = 1e-6
NEG_INF = -1e30
LOG2_E = 1.4426950408889634
D_FF = 5632
GLA_HEADS = 4
GLA_DV = 256
GLA_DK = 128
GLA_QK = GLA_HEADS * GLA_DK
GLA_WIDTH = GLA_HEADS * GLA_DV
GLA_COLS = 2 * GLA_QK + 2 * GLA_WIDTH
GLA_GATE_RANK = 16
GLA_GATE_TAU = 16.0
GLA_CHUNK = 64
MOBA_DH = 128
MOBA_HEADS = 8
MOBA_WIDTH = MOBA_HEADS * MOBA_DH
MOBA_BLOCK = 256
MOBA_TOPK = 3
XATTN_HEADS = 4
XATTN_DH = 128
XATTN_WIDTH = XATTN_HEADS * XATTN_DH

V7X_VMEM_BYTES = 64 * 1024 * 1024
VMEM_LIMIT = V7X_VMEM_BYTES - 6 * 1024 * 1024
FFN_VMEM_LIMIT = V7X_VMEM_BYTES - 2 * 1024 * 1024
MXU_DIM = 256

FFN_TM = 1024
FFN_TF = 512
NORM_TM = 1024
PROJ_TM = 2048
PROJ_TN = 512
GLA_TC = 512
GLA_SUB = 256
GLA_G = 4
MOBA_G = 4
MOBA_TQ = 2 * MOBA_BLOCK
BF16_SUBLANES = 16
MOBA_VPAD = BF16_SUBLANES
MOBA_VROWS = MOBA_DH + MOBA_VPAD
OUTX_TM = 512
OUTX_WCHUNK = 512


def _params(*semantics):
    return pltpu.CompilerParams(dimension_semantics=semantics, vmem_limit_bytes=VMEM_LIMIT)


def _rms(x, g):
    return x * lax.rsqrt(jnp.mean(x * x, axis=-1, keepdims=True) + EPS) * g


def _dot(a, b):
    return jnp.dot(a, b, preferred_element_type=F32)


def _dot_nt(a, b):
    return lax.dot_general(a, b, (((1,), (1,)), ((), ())), preferred_element_type=F32)


def _silu(x):
    return x * (1.0 / (1.0 + jnp.exp(-x)))


def _ffn_kernel(x_hbm, g_ref, wg_ref, wu_ref, wd_ref, o_hbm, h_ref, xbuf_ref, acc_ref, xsem, osem):
    i = pl.program_id(0)
    f = pl.program_id(1)
    last_i = pl.num_programs(0) - 1
    last_f = pl.num_programs(1) - 1

    def x_copy(tile):
        return pltpu.make_async_copy(x_hbm.at[pl.ds(tile * FFN_TM, FFN_TM)], xbuf_ref, xsem)

    def o_copy(tile):
        return pltpu.make_async_copy(acc_ref, o_hbm.at[pl.ds(tile * FFN_TM, FFN_TM)], osem)

    @pl.when(jnp.logical_and(i == 0, f == 0))
    def _():
        x_copy(0).start()

    @pl.when(f == 0)
    def _():
        x_copy(i).wait()
        x = xbuf_ref[...]
        h_ref[...] = _rms(x, g_ref[...]).astype(BF16)

        @pl.when(i > 0)
        def _():
            o_copy(i - 1).wait()

        acc_ref[...] = xbuf_ref[...]

    @pl.when(jnp.logical_and(f == 1, i < last_i))
    def _():
        x_copy(i + 1).start()

    h = h_ref[...]
    gate = _dot(h, wg_ref[...].astype(BF16))
    up = _dot(h, wu_ref[...].astype(BF16))
    act = (0.5 * _silu(gate) * up).astype(BF16)
    acc_ref[...] += _dot(act, wd_ref[...].astype(BF16))

    @pl.when(f == last_f)
    def _():
        o_copy(i).start()

        @pl.when(i == last_i)
        def _():
            o_copy(i).wait()


def _ffn(x, g, wg, wu, wd):
    n, d = x.shape
    grid = (n // FFN_TM, D_FF // FFN_TF)
    assert grid[1] >= 2
    return pl.pallas_call(
        _ffn_kernel,
        grid=grid,
        in_specs=[
            pl.BlockSpec(memory_space=pl.ANY),
            pl.BlockSpec((1, d), lambda i, f: (0, 0)),
            pl.BlockSpec((d, FFN_TF), lambda i, f: (0, f)),
            pl.BlockSpec((d, FFN_TF), lambda i, f: (0, f)),
            pl.BlockSpec((FFN_TF, d), lambda i, f: (f, 0)),
        ],
        out_specs=pl.BlockSpec(memory_space=pl.ANY),
        out_shape=jax.ShapeDtypeStruct((n, d), F32),
        scratch_shapes=[pltpu.VMEM((FFN_TM, d), BF16), pltpu.VMEM((FFN_TM, d), F32),
                        pltpu.VMEM((FFN_TM, d), F32),
                        pltpu.SemaphoreType.DMA, pltpu.SemaphoreType.DMA],
        compiler_params=pltpu.CompilerParams(dimension_semantics=("arbitrary", "arbitrary"),
                                             vmem_limit_bytes=FFN_VMEM_LIMIT),
        name="ffn",
    )(x, g.reshape(1, d), wg, wu, wd)


def _mixnorm_kernel(x_ref, g_ref, wlr_ref, wg2_ref, bg2_ref, hn_ref, la_ref):
    h = _rms(x_ref[...], g_ref[...]).astype(BF16)
    hn_ref[...] = h
    lr = _dot_nt(h, wlr_ref[...].astype(BF16))
    pre = _dot(lr.astype(BF16), wg2_ref[...].astype(BF16)) + bg2_ref[...]
    log_sig = -(jnp.maximum(-pre, 0.0) + jnp.log(1.0 + jnp.exp(-jnp.abs(pre))))
    la_ref[...] = log_sig * (1.0 / GLA_GATE_TAU)


def _mixnorm(x1, g, w_lr_t, wg2, bg2):
    n, d = x1.shape
    const = lambda i: (0, 0)
    return pl.pallas_call(
        _mixnorm_kernel,
        grid=(n // NORM_TM,),
        in_specs=[
            pl.BlockSpec((NORM_TM, d), lambda i: (i, 0)),
            pl.BlockSpec((1, d), const),
            pl.BlockSpec((GLA_GATE_RANK, d), const),
            pl.BlockSpec((GLA_GATE_RANK, GLA_QK), const),
            pl.BlockSpec((1, GLA_QK), const),
        ],
        out_specs=[pl.BlockSpec((NORM_TM, d), lambda i: (i, 0)),
                   pl.BlockSpec((NORM_TM, GLA_QK), lambda i: (i, 0))],
        out_shape=[jax.ShapeDtypeStruct((n, d), BF16),
                   jax.ShapeDtypeStruct((n, GLA_QK), F32)],
        compiler_params=_params("parallel"),
        name="mixnorm",
    )(x1, g.reshape(1, d), w_lr_t, wg2, bg2.reshape(1, -1))


def _projg_kernel(hn_ref, wt_ref, u_ref):
    u_ref[...] = _dot_nt(hn_ref[...], wt_ref[0].astype(BF16))


def _projq_kernel(hn_ref, wt_ref, g_ref, qt_ref):
    yt = _dot_nt(wt_ref[0].astype(BF16), hn_ref[...])
    g = g_ref[...]
    parts = []
    for k in range(PROJ_TN // MOBA_DH):
        p = yt[k * MOBA_DH:(k + 1) * MOBA_DH, :]
        inv = lax.rsqrt(jnp.mean(p * p, axis=0, keepdims=True) + EPS)
        parts.append(p * inv * g)
    qt_ref[...] = jnp.concatenate(parts, axis=0)


def _projk_kernel(hn_ref, wt_ref, g_ref, kb_ref, kmean_ref):
    y = _dot_nt(hn_ref[...], wt_ref[0].astype(BF16))
    g = g_ref[...]
    parts = [_rms(y[:, k * MOBA_DH:(k + 1) * MOBA_DH], g) for k in range(PROJ_TN // MOBA_DH)]
    kn = jnp.concatenate(parts, axis=1)
    kb_ref[...] = kn.astype(BF16)
    blocks = PROJ_TM // MOBA_BLOCK
    kmean_ref[0] = jnp.sum(kn.reshape(blocks, MOBA_BLOCK, PROJ_TN), axis=1) * (1.0 / MOBA_BLOCK)


def _projv_kernel(hn_ref, wt_ref, v3_ref):
    yt = _dot_nt(wt_ref[0].astype(BF16), hn_ref[...]).astype(BF16)
    ones = jnp.ones((MOBA_VPAD, PROJ_TM), BF16)
    parts = []
    for k in range(PROJ_TN // MOBA_DH):
        parts += [yt[k * MOBA_DH:(k + 1) * MOBA_DH], ones]
    ya = jnp.concatenate(parts, axis=0)
    for t in range(PROJ_TM // MOBA_BLOCK):
        v3_ref[t] = ya[:, t * MOBA_BLOCK:(t + 1) * MOBA_BLOCK]


def _proj_kernel(hn_ref, wt_ref, gq_ref, gk_ref, u_ref, qt_ref, kb_ref, kmean_ref, v3_ref):
    j = pl.program_id(1)
    gla = GLA_COLS // PROJ_TN
    steps = MOBA_WIDTH // PROJ_TN

    @pl.when(j < gla)
    def _():
        _projg_kernel(hn_ref, wt_ref, u_ref)

    @pl.when(jnp.logical_and(j >= gla, j < gla + steps))
    def _():
        _projq_kernel(hn_ref, wt_ref, gq_ref, qt_ref)

    @pl.when(jnp.logical_and(j >= gla + steps, j < gla + 2 * steps))
    def _():
        _projk_kernel(hn_ref, wt_ref, gk_ref, kb_ref, kmean_ref)

    @pl.when(j >= gla + 2 * steps)
    def _():
        _projv_kernel(hn_ref, wt_ref, v3_ref)


def _proj(hn, w_in_t, layer, gq, gk):
    n, d = hn.shape
    gla = GLA_COLS // PROJ_TN
    steps = MOBA_WIDTH // PROJ_TN
    blocks = PROJ_TM // MOBA_BLOCK
    moba_gap = GLA_GATE_RANK
    own = lambda j, first: jnp.clip(j - gla - first * steps, 0, steps - 1)
    const = lambda i, j: (0, 0)
    u, q_t, kb, kmean, v3 = pl.pallas_call(
        _proj_kernel,
        grid=(n // PROJ_TM, gla + 3 * steps),
        in_specs=[
            pl.BlockSpec((PROJ_TM, d), lambda i, j: (i, 0)),
            pl.BlockSpec((pl.Element(1), pl.Element(PROJ_TN), pl.Element(d)),
                         lambda i, j: (layer,
                                       pl.multiple_of(j * PROJ_TN + jnp.where(j >= gla, moba_gap, 0), 8), 0)),
            pl.BlockSpec((MOBA_DH, 1), const),
            pl.BlockSpec((1, MOBA_DH), const),
        ],
        out_specs=[
            pl.BlockSpec((PROJ_TM, PROJ_TN), lambda i, j: (i, jnp.minimum(j, gla - 1))),
            pl.BlockSpec((PROJ_TN, PROJ_TM), lambda i, j: (own(j, 0), i)),
            pl.BlockSpec((PROJ_TM, PROJ_TN), lambda i, j: (i, own(j, 1))),
            pl.BlockSpec((1, blocks, PROJ_TN), lambda i, j: (i, 0, own(j, 1))),
            pl.BlockSpec((blocks, PROJ_TN // MOBA_DH * MOBA_VROWS, MOBA_BLOCK),
                         lambda i, j: (i, own(j, 2), 0)),
        ],
        out_shape=[jax.ShapeDtypeStruct((n, GLA_COLS), F32),
                   jax.ShapeDtypeStruct((MOBA_WIDTH, n), F32),
                   jax.ShapeDtypeStruct((n, MOBA_WIDTH), BF16),
                   jax.ShapeDtypeStruct((n // PROJ_TM, blocks, MOBA_WIDTH), F32),
                   jax.ShapeDtypeStruct((n // MOBA_BLOCK, MOBA_HEADS * MOBA_VROWS, MOBA_BLOCK), BF16)],
        compiler_params=_params("arbitrary", "arbitrary"),
        name="proj",
    )(hn, w_in_t, gq.reshape(-1, 1), gk.reshape(1, -1))
    return u, q_t, kb, kmean.reshape(n // MOBA_BLOCK, MOBA_WIDTH), v3


def _dot_exact_rhs(m, a):
    a1 = a.astype(BF16)
    r1 = a - a1.astype(F32)
    a2 = r1.astype(BF16)
    a3 = (r1 - a2.astype(F32)).astype(BF16)
    return _dot(m, a1) + _dot(m, a2) + _dot(m, a3)


def _gla_head(q, k, v, r, la, g_out, state, upd_ref):
    tc = GLA_TC
    nchunk = tc // GLA_CHUNK
    sub = GLA_SUB

    crow = lax.broadcasted_iota(jnp.int32, (GLA_CHUNK, GLA_CHUNK), 0)
    ccol = lax.broadcasted_iota(jnp.int32, (GLA_CHUNK, GLA_CHUNK), 1)
    tri = (ccol <= crow).astype(BF16)
    la_wide = jnp.concatenate([la[c * GLA_CHUNK:(c + 1) * GLA_CHUNK] for c in range(nchunk)], axis=1)
    bc_wide = _dot_exact_rhs(tri, la_wide)
    bcum = jnp.concatenate([bc_wide[:, c * GLA_DK:(c + 1) * GLA_DK] for c in range(nchunk)], axis=0)
    btot = jnp.concatenate(
        [jnp.broadcast_to(bc_wide[GLA_CHUNK - 1:GLA_CHUNK, c * GLA_DK:(c + 1) * GLA_DK],
                          (GLA_CHUNK, GLA_DK)) for c in range(nchunk)], axis=0)

    q_dec_b = ((q * (GLA_DK ** -0.5)) * jnp.exp(bcum)).astype(BF16)
    k_inv_b = (k * jnp.exp(-bcum)).astype(BF16)
    k_tail_t = (k * jnp.exp(btot - bcum)).T.astype(BF16)
    decay_t = jnp.exp(btot).T
    vb = v.astype(BF16)

    row = lax.broadcasted_iota(jnp.int32, (sub, sub), 0)
    col = lax.broadcasted_iota(jnp.int32, (sub, sub), 1)
    causal = jnp.logical_and((row // GLA_CHUNK) == (col // GLA_CHUNK), col <= row)
    intra = []
    for p in range(tc // sub):
        rows = slice(p * sub, (p + 1) * sub)
        att = jnp.where(causal, _dot_nt(q_dec_b[rows], k_inv_b[rows]), 0.0)
        intra.append(_dot(att.astype(BF16), vb[rows]))

    for c in range(nchunk):
        rows = slice(c * GLA_CHUNK, (c + 1) * GLA_CHUNK)
        upd_ref[c] = _dot(k_tail_t[:, rows], vb[rows])

    inter = []
    for c in range(nchunk):
        rows = slice(c * GLA_CHUNK, (c + 1) * GLA_CHUNK)
        inter.append(_dot(q_dec_b[rows], state.astype(BF16)))
        dec_c = decay_t[:, c * GLA_CHUNK:c * GLA_CHUNK + 1]
        state = dec_c * state + upd_ref[c]
    o = jnp.concatenate(intra, axis=0) + jnp.concatenate(inter, axis=0)
    return _rms(o, g_out) * _silu(r), state


def _gla_kernel(q_ref, k_ref, v_ref, r_ref, la_ref, go_ref, o_ref, s_ref, upd_ref):
    @pl.when(pl.program_id(2) == 0)
    def _():
        s_ref[...] = jnp.zeros_like(s_ref)

    for g in range(GLA_G):
        qk = slice(g * GLA_DK, (g + 1) * GLA_DK)
        vr = slice(g * GLA_DV, (g + 1) * GLA_DV)
        o, state = _gla_head(q_ref[:, qk], k_ref[:, qk], v_ref[:, vr], r_ref[:, vr], la_ref[:, qk],
                             go_ref[...], s_ref[g], upd_ref.at[g])
        s_ref[g] = state
        o_ref[:, vr] = o


def _gla(u, la, g_out, batch, seq):
    n = u.shape[0]
    nt = seq // GLA_TC
    tok = lambda b, h, t: b * nt + t
    qk_w, v_w = GLA_G * GLA_DK, GLA_G * GLA_DV
    k_blk0 = GLA_QK // qk_w
    v_blk0 = 2 * GLA_QK // v_w
    r_blk0 = v_blk0 + GLA_WIDTH // v_w
    return pl.pallas_call(
        _gla_kernel,
        grid=(batch, GLA_HEADS // GLA_G, nt),
        in_specs=[
            pl.BlockSpec((GLA_TC, qk_w), lambda b, h, t: (tok(b, h, t), h)),
            pl.BlockSpec((GLA_TC, qk_w), lambda b, h, t: (tok(b, h, t), k_blk0 + h)),
            pl.BlockSpec((GLA_TC, v_w), lambda b, h, t: (tok(b, h, t), v_blk0 + h)),
            pl.BlockSpec((GLA_TC, v_w), lambda b, h, t: (tok(b, h, t), r_blk0 + h)),
            pl.BlockSpec((GLA_TC, qk_w), lambda b, h, t: (tok(b, h, t), h)),
            pl.BlockSpec((1, GLA_DV), lambda b, h, t: (0, 0)),
        ],
        out_specs=pl.BlockSpec((GLA_TC, v_w), lambda b, h, t: (tok(b, h, t), h)),
        out_shape=jax.ShapeDtypeStruct((n, GLA_WIDTH), F32),
        scratch_shapes=[pltpu.VMEM((GLA_G, GLA_DK, GLA_DV), F32),
                        pltpu.VMEM((GLA_G, GLA_TC // GLA_CHUNK, GLA_DK, GLA_DV), F32)],
        compiler_params=_params("parallel", "parallel", "arbitrary"),
        name="gla",
    )(u, u, u, u, la, g_out.reshape(1, -1))


def _moba_kernel(qt_ref, kb_ref, v3_ref, kmean_ref, o_ref, kaug_ref, snext_ref):
    ti = pl.program_id(2)
    last_tile = pl.num_programs(2) - 1
    bs = MOBA_BLOCK
    tq = MOBA_TQ
    nb = kb_ref.shape[0] // bs
    dh = MOBA_DH
    vr = MOBA_VROWS
    scale = (dh ** -0.5) * LOG2_E

    @pl.when(ti == 0)
    def _():
        seq = kb_ref.shape[0]
        row_blk = lax.broadcasted_iota(jnp.int32, (seq, MXU_DIM - dh), 0) // bs
        lane = lax.broadcasted_iota(jnp.int32, (seq, MXU_DIM - dh), 1)
        onehot = (row_blk == lane).astype(BF16)
        for g in range(MOBA_G):
            kaug_ref[g, :, :dh] = kb_ref[:, g * dh:(g + 1) * dh]
            kaug_ref[g, :, dh:] = onehot

    blk = lax.broadcasted_iota(jnp.int32, (nb, tq), 0)
    q_blk = 2 * ti + (lax.broadcasted_iota(jnp.int32, (nb, tq), 1) // bs)
    past = blk < q_blk
    causal = (lax.broadcasted_iota(jnp.int32, (bs, bs), 0)
              <= lax.broadcasted_iota(jnp.int32, (bs, bs), 1))
    diag = pl.multiple_of(ti * tq, tq)

    def pv_dot(g, t, p):
        return (_dot(v3_ref[2 * t, g * vr:(g + 1) * vr, :], p[:bs])
                + _dot(v3_ref[2 * t + 1, g * vr:(g + 1) * vr, :], p[bs:]))

    q_aug, biases, state = [], [], []

    def score(g, t):
        off = pl.multiple_of(t * tq, tq)
        s = _dot(kaug_ref[g, pl.ds(off, tq), :], q_aug[g])
        return s, jnp.max(s, axis=0, keepdims=True)

    for g in range(MOBA_G):
        qt = qt_ref[g * dh:(g + 1) * dh, :]
        gate = jnp.dot(kmean_ref[:, g * dh:(g + 1) * dh], qt,
                       precision=lax.Precision.HIGHEST, preferred_element_type=F32)
        gate = jnp.where(past, gate, NEG_INF)
        rank = jnp.zeros((nb, tq), jnp.int32)
        for j in range(nb):
            gj = gate[j:j + 1, :]
            ahead = jnp.logical_or(gj > gate, jnp.logical_and(gj == gate, blk > j))
            rank = rank + ahead.astype(jnp.int32)
        chosen = jnp.logical_and(past, rank < MOBA_TOPK)
        bias = jnp.where(chosen, 0.0, NEG_INF)
        qa = jnp.concatenate([(qt * scale).astype(BF16), bias.astype(BF16),
                              jnp.zeros((MXU_DIM - dh - nb, tq), BF16)], axis=0)
        q_aug.append(qa)
        biases.append(bias)

    own_scores, first_maxes = [], []
    for g in range(MOBA_G):
        qa, bias = q_aug[g], biases[g]
        first_bias = jnp.sum(jnp.where(blk == 2 * ti, bias, 0.0), axis=0, keepdims=True)[:, bs:]
        s0 = _dot(kaug_ref[g, pl.ds(diag, bs), :dh], qa[:dh])
        s00 = jnp.where(causal, s0[:, :bs], NEG_INF)
        s01 = s0[:, bs:] + first_bias
        s11 = jnp.where(causal, _dot(kaug_ref[g, pl.ds(pl.multiple_of(diag + bs, bs), bs), :dh],
                                     qa[:dh, bs:]), NEG_INF)
        own_scores.append((s00, s01, s11))
    for g in range(MOBA_G):
        s_first, first_max = score(g, 0)
        snext_ref[g] = s_first
        first_maxes.append(first_max)
    for g in range(MOBA_G):
        s00, s01, s11 = own_scores[g]
        m0 = jnp.max(s00, axis=0, keepdims=True)
        m1 = jnp.maximum(jnp.max(s01, axis=0, keepdims=True), jnp.max(s11, axis=0, keepdims=True))
        p0 = jnp.concatenate([jnp.exp2(s00 - m0), jnp.exp2(s01 - m1)], axis=1).astype(BF16)
        p11 = jnp.exp2(s11 - m1).astype(BF16)
        acc0 = _dot(v3_ref[2 * ti, g * vr:(g + 1) * vr, :], p0)
        acc1 = _dot(v3_ref[2 * ti + 1, g * vr:(g + 1) * vr, :], p11)
        acc = jnp.concatenate([acc0[:, :bs], acc0[:, bs:] + acc1], axis=1)
        state.append((jnp.concatenate([m0, m1], axis=1), acc, first_maxes[g]))

    def consume(g, t, s, tile_max, m, acc, valid=None):
        if valid is not None:
            tile_max = jnp.where(valid, tile_max, NEG_INF)
        m_new = jnp.maximum(m, tile_max)
        alpha = jnp.exp2(m - m_new)
        shift = m_new if valid is None else jnp.where(valid, m_new, -NEG_INF)
        p = jnp.exp2(s - shift).astype(BF16)
        return m_new, alpha * acc + pv_dot(g, t, p)

    def body(u, carry):
        out = []
        for g in range(MOBA_G):
            m, acc, max_a = carry[g]
            s_b, max_b = score(g, 2 * u + 1)
            m, acc = consume(g, 2 * u, snext_ref[g], max_a, m, acc)
            s_c, max_c = score(g, jnp.minimum(2 * u + 2, last_tile))
            snext_ref[g] = s_c
            m, acc = consume(g, 2 * u + 1, s_b, max_b, m, acc, valid=2 * u + 1 < ti)
            out.append((m, acc, max_c))
        return tuple(out)

    carry = lax.fori_loop(0, (ti + 1) // 2, body, tuple(state))
    for g in range(MOBA_G):
        m, acc, _ = carry[g]
        o_ref[:, g * dh:(g + 1) * dh] = (acc[:dh] * (1.0 / acc[dh:dh + 1])).T


def _moba(q_t, kb, kmean, v3, batch, seq):
    n = kb.shape[0]
    nb = seq // MOBA_BLOCK
    nt = seq // MOBA_TQ
    gw = MOBA_G * MOBA_DH
    return pl.pallas_call(
        _moba_kernel,
        grid=(batch, MOBA_HEADS // MOBA_G, nt),
        in_specs=[
            pl.BlockSpec((gw, MOBA_TQ), lambda b, h, i: (h, b * nt + i)),
            pl.BlockSpec((seq, gw), lambda b, h, i: (b, h)),
            pl.BlockSpec((nb, MOBA_G * MOBA_VROWS, MOBA_BLOCK), lambda b, h, i: (b, h, 0)),
            pl.BlockSpec((nb, gw), lambda b, h, i: (b, h)),
        ],
        out_specs=pl.BlockSpec((MOBA_TQ, gw), lambda b, h, i: (b * nt + i, h)),
        out_shape=jax.ShapeDtypeStruct((n, MOBA_WIDTH), F32),
        scratch_shapes=[pltpu.VMEM((MOBA_G, seq, MXU_DIM), BF16),
                        pltpu.VMEM((MOBA_G, MOBA_TQ, MOBA_TQ), F32)],
        compiler_params=_params("parallel", "parallel", "arbitrary"),
        name="moba",
    )(q_t, kb, v3, kmean)


def _memkv_kernel(mem_ref, g_ref, wkv_ref, gk_ref, k_ref, v_ref, h_ref):
    j = pl.program_id(0)
    half = pl.num_programs(0) // 2

    @pl.when(j == 0)
    def _():
        h_ref[...] = _rms(mem_ref[...], g_ref[...]).astype(BF16)

    y = _dot(h_ref[...], wkv_ref[...].astype(BF16))

    @pl.when(j < half)
    def _():
        gk = gk_ref[...]
        k_ref[...] = jnp.concatenate([_rms(y[:, i * XATTN_DH:(i + 1) * XATTN_DH], gk) for i in range(2)], axis=1)

    @pl.when(j >= half)
    def _():
        v_ref[...] = y


def _memkv(mem, g, w_kv, gk):
    m, d = mem.shape
    tn = 2 * XATTN_DH
    steps = XATTN_WIDTH // tn
    const = lambda j: (0, 0)
    return pl.pallas_call(
        _memkv_kernel,
        grid=(2 * steps,),
        in_specs=[pl.BlockSpec((m, d), const), pl.BlockSpec((1, d), const),
                  pl.BlockSpec((d, tn), lambda j: (0, j)), pl.BlockSpec((1, XATTN_DH), const)],
        out_specs=[pl.BlockSpec((m, tn), lambda j: (0, jnp.minimum(j, steps - 1))),
                   pl.BlockSpec((m, tn), lambda j: (0, jnp.maximum(j - steps, 0)))],
        out_shape=[jax.ShapeDtypeStruct((m, XATTN_WIDTH), F32),
                   jax.ShapeDtypeStruct((m, XATTN_WIDTH), F32)],
        scratch_shapes=[pltpu.VMEM((m, d), BF16)],
        compiler_params=_params("arbitrary"),
        name="memkv",
    )(mem, g.reshape(1, d), w_kv, gk.reshape(1, -1))


def _load_bf16_weights(pairs, stage_ref, sem):
    chunks = [(src, dst, r, min(OUTX_WCHUNK, dst.shape[0] - r))
              for src, dst in pairs for r in range(0, dst.shape[0], OUTX_WCHUNK)]

    def copy(k):
        src, dst, r, rows = chunks[k]
        return pltpu.make_async_copy(src.at[pl.ds(r, rows), :],
                                     stage_ref.at[k % 2, :rows, :dst.shape[1]], sem.at[k % 2])

    copy(0).start()
    for k, (_, dst, r, rows) in enumerate(chunks):
        if k + 1 < len(chunks):
            copy(k + 1).start()
        copy(k).wait()
        dst[r:r + rows, :] = stage_ref[k % 2, :rows, :dst.shape[1]].astype(BF16)


def _outx_kernel(x1_ref, og_ref, om_ref, wout_hbm, gx_ref, wq_hbm, gq_ref, kx_ref, vx_ref, wo_hbm,
                 o_ref, wout_ref, wq_ref, wo_ref, stage_ref, wsem):
    @pl.when(pl.program_id(0) == 0)
    def _():
        _load_bf16_weights(((wout_hbm, wout_ref), (wq_hbm, wq_ref), (wo_hbm, wo_ref)), stage_ref, wsem)

    x2 = (x1_ref[...]
          + _dot(og_ref[...].astype(BF16), wout_ref[:GLA_WIDTH, :])
          + _dot(om_ref[...].astype(BF16), wout_ref[GLA_WIDTH:, :]))
    h = _rms(x2, gx_ref[...]).astype(BF16)
    q = _dot(h, wq_ref[...])
    gq = gq_ref[...]
    scale = (XATTN_DH ** -0.5) * LOG2_E
    cols = [slice(i * XATTN_DH, (i + 1) * XATTN_DH) for i in range(XATTN_HEADS)]
    qh = [(_rms(q[:, c], gq) * scale).astype(BF16) for c in cols]
    ss = [_dot_nt(qh[i], kx_ref[:, c].astype(BF16)) for i, c in enumerate(cols)]
    ps = []
    for s in ss:
        p = jnp.exp2(s - jnp.max(s, axis=-1, keepdims=True))
        ps.append((p * (1.0 / jnp.sum(p, axis=-1, keepdims=True))).astype(BF16))
    heads = [_dot(ps[i], vx_ref[:, c].astype(BF16)) for i, c in enumerate(cols)]
    o = jnp.concatenate(heads, axis=1).astype(BF16)
    o_ref[...] = x2 + _dot(o, wo_ref[...])


def _outx(x1, og, om, w_out, gx, w_q, gq, kx, vx, w_o, seq, mem_len):
    n, d = x1.shape
    any_spec = pl.BlockSpec(memory_space=pl.ANY)
    tiles_per_seq = seq // OUTX_TM
    const = lambda i: (0, 0)
    return pl.pallas_call(
        _outx_kernel,
        grid=(n // OUTX_TM,),
        in_specs=[
            pl.BlockSpec((OUTX_TM, d), lambda i: (i, 0)),
            pl.BlockSpec((OUTX_TM, GLA_WIDTH), lambda i: (i, 0)),
            pl.BlockSpec((OUTX_TM, MOBA_WIDTH), lambda i: (i, 0)),
            any_spec,
            pl.BlockSpec((1, d), const),
            any_spec,
            pl.BlockSpec((1, XATTN_DH), const),
            pl.BlockSpec((mem_len, XATTN_WIDTH), lambda i: (i // tiles_per_seq, 0)),
            pl.BlockSpec((mem_len, XATTN_WIDTH), lambda i: (i // tiles_per_seq, 0)),
            any_spec,
        ],
        out_specs=pl.BlockSpec((OUTX_TM, d), lambda i: (i, 0)),
        out_shape=jax.ShapeDtypeStruct((n, d), F32),
        scratch_shapes=[pltpu.VMEM(w_out.shape, BF16), pltpu.VMEM(w_q.shape, BF16), pltpu.VMEM(w_o.shape, BF16),
                        pltpu.VMEM((2, OUTX_WCHUNK, d), F32), pltpu.SemaphoreType.DMA((2,))],
        compiler_params=_params("arbitrary"),
        name="outx",
    )(x1, og, om, w_out, gx.reshape(1, d), w_q, gq.reshape(1, -1), kx, vx, w_o)


def kernel(x, mem, ffn1_norm, ffn1_w_gate, ffn1_w_up, ffn1_w_down, mix_norm, w_in, gla_w_gate2, gla_b_gate2, gla_out_norm, moba_q_norm, moba_k_norm, w_out, xattn_norm, mem_norm, xattn_w_q, xattn_w_kv, xattn_w_o, xattn_q_norm, xattn_k_norm, ffn2_norm, ffn2_w_gate, ffn2_w_up, ffn2_w_down):
    batch, seq, d = x.shape
    mem_len = mem.shape[1]
    depth = ffn1_norm.shape[0]
    n = batch * seq
    assert seq % GLA_TC == 0 and seq % (2 * MOBA_BLOCK) == 0 and seq % OUTX_TM == 0
    assert seq // MOBA_BLOCK + MOBA_DH <= MXU_DIM and (seq // MOBA_BLOCK) % BF16_SUBLANES == 0
    assert n % FFN_TM == 0 and n % NORM_TM == 0 and n % PROJ_TM == 0 and PROJ_TM % MOBA_BLOCK == 0

    xf = x.reshape(n, d)
    memf = mem.reshape(batch * mem_len, d)
    lr0 = GLA_COLS
    mq0 = lr0 + GLA_GATE_RANK
    w_in_t = jnp.swapaxes(w_in, 1, 2)
    for l in range(depth):
        w_lr_t = w_in_t[l, lr0:mq0]

        x1 = _ffn(xf, ffn1_norm[l], ffn1_w_gate[l], ffn1_w_up[l], ffn1_w_down[l])
        hn, la = _mixnorm(x1, mix_norm[l], w_lr_t, gla_w_gate2[l], gla_b_gate2[l])
        u, q_t, kb, kmean, v3 = _proj(hn, w_in_t, l, moba_q_norm[l], moba_k_norm[l])
        o_gla = _gla(u, la, gla_out_norm[l], batch, seq)
        o_moba = _moba(q_t, kb, kmean, v3, batch, seq)
        kx, vx = _memkv(memf, mem_norm[l], xattn_w_kv[l], xattn_k_norm[l])
        x3 = _outx(x1, o_gla, o_moba, w_out[l], xattn_norm[l], xattn_w_q[l], xattn_q_norm[l], kx, vx,
                   xattn_w_o[l], seq, mem_len)
        xf = _ffn(x3, ffn2_norm[l], ffn2_w_gate[l], ffn2_w_up[l], ffn2_w_down[l])
    return xf.reshape(batch, seq, d)
```
